```python
import math, functools
import jax, jax.numpy as jnp
from jax import lax
import numpy as np

D_MODEL = 1024
BATCH = 8
SEQ = 4096
DEPTH = 4
DEC_BATCH = 8
DEC_SEQ = 2048
PAST_LEN = 128

S5_WIDTH = D_MODEL // 4
S5_GROUP = 16
S5_GROUPS = S5_WIDTH // S5_GROUP
S5_STATE = 64
CONV_WIDTH = D_MODEL // 4
GDN_WIDTH = D_MODEL // 2
GDN_HEADS = 4
GDN_HEAD_DIM = GDN_WIDTH // GDN_HEADS
GDN_CHUNK = 64
SHORT_CONV = 3
MIX_WIDTH = S5_WIDTH + CONV_WIDTH + GDN_WIDTH
IN_COLS = S5_WIDTH + 3 * CONV_WIDTH + 4 * GDN_WIDTH + 4 * GDN_HEADS
MOE_GROUPS = 4
EXPERTS_PER_GROUP = 8
N_EXPERTS = MOE_GROUPS * EXPERTS_PER_GROUP
TOP_K = 2
EXPERT_FF = 512
MOE_BLOCK = 512
EPS = 1e-6

kernel_name = 'hybrid_bidir_s5_conv_gdn_hmoe_encoder'


def rms_norm(x, w):
    xf = x.astype(jnp.float32)
    xf = xf * lax.rsqrt(jnp.mean(xf * xf, axis=-1, keepdims=True) + EPS)
    return xf.astype(x.dtype) * w


def conv3_centred(x, w):
    xp = jnp.pad(x, ((0, 0), (1, 1), (0, 0)))
    return xp[:, :-2] * w[0] + xp[:, 1:-1] * w[1] + xp[:, 2:] * w[2]


def _in_split_points():
    widths = [S5_WIDTH, CONV_WIDTH, CONV_WIDTH, CONV_WIDTH, 3 * GDN_WIDTH, GDN_WIDTH,
              GDN_HEADS, GDN_HEADS, GDN_HEADS]
    points, acc = [], 0
    for wd in widths:
        acc += wd
        points.append(acc)
    return points


def _cmul(ar, ai, br, bi):
    return ar * br - ai * bi, ar * bi + ai * br


def _ssm_combine(e1, e2):
    a1r, a1i, b1r, b1i = e1
    a2r, a2i, b2r, b2i = e2
    ar, ai = _cmul(a2r, a2i, a1r, a1i)
    br, bi = _cmul(a2r, a2i, b1r, b1i)
    return ar, ai, br + b2r, bi + b2i


def s5_mixer(u, lam_re, lam_im, log_step, b_re, b_im, c_re, c_im, d_skip, w_glu, b_glu):
    bsz, seq, _ = u.shape
    uf = u.astype(jnp.float32)
    ug = uf.reshape(bsz, seq, S5_GROUPS, S5_GROUP)
    y = uf * d_skip.astype(jnp.float32)
    for direction in range(2):
        lr = lam_re[direction].astype(jnp.float32)
        li = lam_im[direction].astype(jnp.float32)
        dt = jnp.exp(log_step[direction].astype(jnp.float32))[:, None]
        mag = jnp.exp(lr * dt)
        abr, abi = mag * jnp.cos(li * dt), mag * jnp.sin(li * dt)
        den = lr * lr + li * li
        cr = ((abr - 1.0) * lr + abi * li) / den
        ci = (abi * lr - (abr - 1.0) * li) / den
        bbr, bbi = _cmul(cr[..., None], ci[..., None],
                         b_re[direction].astype(jnp.float32), b_im[direction].astype(jnp.float32))
        xr = jnp.einsum('blgh,gph->blgp', ug, bbr)
        xi = jnp.einsum('blgh,gph->blgp', ug, bbi)
        ar = jnp.broadcast_to(abr, xr.shape)
        ai = jnp.broadcast_to(abi, xr.shape)
        _, _, hr, hi = lax.associative_scan(_ssm_combine, (ar, ai, xr, xi),
                                            reverse=(direction == 1), axis=1)
        yg = (jnp.einsum('blgp,ghp->blgh', hr, c_re[direction].astype(jnp.float32))
              - jnp.einsum('blgp,ghp->blgh', hi, c_im[direction].astype(jnp.float32)))
        y = y + yg.reshape(bsz, seq, S5_WIDTH)
    y = jax.nn.gelu(y)
    out = y * jax.nn.sigmoid(y @ w_glu.astype(jnp.float32) + b_glu.astype(jnp.float32))
    return out.astype(u.dtype)


def _l2norm(x):
    return x * lax.rsqrt(jnp.sum(x * x, axis=-1, keepdims=True) + EPS)


def gdn_chunked(q, k, v, g, beta):
    bsz, nh, seq, dk = q.shape
    dv = v.shape[-1]
    nc = seq // GDN_CHUNK
    chunk = lambda t: t.reshape(bsz, nh, nc, GDN_CHUNK, *t.shape[3:])
    q, k, v, g, beta = chunk(q), chunk(k), chunk(v), chunk(g), chunk(beta)
    g = jnp.cumsum(g, axis=-1)
    idx = jnp.arange(GDN_CHUNK)
    incl = idx[:, None] >= idx[None, :]
    strict = idx[:, None] > idx[None, :]
    decay = jnp.exp(jnp.where(incl, g[..., :, None] - g[..., None, :], -jnp.inf))
    k_beta = k * beta[..., None]
    lower = jnp.where(strict, jnp.einsum('bhncd,bhnsd->bhncs', k_beta, k) * decay, 0.0)
    solve = functools.partial(lax.linalg.triangular_solve, left_side=True, lower=True,
                              unit_diagonal=True)
    u = solve(lower, v * beta[..., None])
    w = solve(lower, k_beta * jnp.exp(g)[..., None])
    intra = jnp.einsum('bhncd,bhnsd->bhncs', q, k) * decay
    g_last = g[..., -1]
    k_end = k * jnp.exp(g_last[..., None] - g)[..., None]
    q_dec = q * jnp.exp(g)[..., None]

    def step(state, inp):
        qc, kc, uc, wc, ac, gl = inp
        v_new = uc - jnp.einsum('bhcd,bhde->bhce', wc, state)
        out = jnp.einsum('bhcd,bhde->bhce', qc, state) + jnp.einsum('bhcs,bhse->bhce', ac, v_new)
        state = state * jnp.exp(gl)[..., None, None] + jnp.einsum('bhcd,bhce->bhde', kc, v_new)
        return state, out

    xs = tuple(jnp.moveaxis(t, 2, 0) for t in (q_dec, k_end, u, w, intra, g_last))
    state0 = jnp.zeros((bsz, nh, dk, dv), jnp.float32)
    _, out = lax.scan(step, state0, xs)
    return jnp.moveaxis(out, 0, 2).reshape(bsz, nh, seq, dv)


def gdn_mixer(qkv, z, a_f, a_b, b_f, b_b, conv_w, a_log, dt_bias, norm_w):
    bsz, seq, _ = qkv.shape
    out_dtype = qkv.dtype
    qkv = jax.nn.silu(conv3_centred(qkv, conv_w)).astype(jnp.float32)
    q, k, v = jnp.split(qkv, 3, axis=-1)
    heads = lambda t: t.reshape(bsz, seq, GDN_HEADS, GDN_HEAD_DIM).transpose(0, 2, 1, 3)
    q = _l2norm(heads(q)) * (GDN_HEAD_DIM ** -0.5)
    k = _l2norm(heads(k))
    v = heads(v)

    def gates(a_in, b_in, direction):
        g = -jnp.exp(a_log[direction].astype(jnp.float32)) * jax.nn.softplus(
            a_in.astype(jnp.float32) + dt_bias[direction].astype(jnp.float32))
        beta = jax.nn.sigmoid(b_in.astype(jnp.float32))
        return g.transpose(0, 2, 1), beta.transpose(0, 2, 1)

    g_fw, beta_fw = gates(a_f, b_f, 0)
    g_bw, beta_bw = gates(a_b, b_b, 1)
    flip = lambda t: jnp.flip(t, axis=2)
    o_fw = gdn_chunked(q, k, v, g_fw, beta_fw)
    o_bw = flip(gdn_chunked(flip(q), flip(k), flip(v), flip(g_bw), flip(beta_bw)))
    o = (o_fw + o_bw).transpose(0, 2, 1, 3)
    zh = z.astype(jnp.float32).reshape(bsz, seq, GDN_HEADS, GDN_HEAD_DIM)
    o = rms_norm(o, norm_w.astype(jnp.float32)) * jax.nn.silu(zh)
    return o.reshape(bsz, seq, GDN_WIDTH).astype(out_dtype)


def hier_moe(h, w_group, b_group, w_expert, b_expert, w_gate_up, w_down):
    bsz, seq, d = h.shape
    n_tok = bsz * seq
    n_rows = n_tok * TOP_K
    hf = h.reshape(n_tok, d)
    gprob = jax.nn.softmax((hf @ w_group + b_group).astype(jnp.float32), axis=-1)
    g_w, g_idx = lax.top_k(gprob, 1)
    elog = (hf @ w_expert + b_expert).astype(jnp.float32).reshape(n_tok, MOE_GROUPS, EXPERTS_PER_GROUP)
    elog = jnp.take_along_axis(elog, g_idx[:, :, None], axis=1)[:, 0]
    e_w, e_i = lax.top_k(jax.nn.softmax(elog, axis=-1), TOP_K)
    e_w = e_w / jnp.sum(e_w, axis=-1, keepdims=True) * g_w
    expert = (g_idx * EXPERTS_PER_GROUP + e_i).reshape(n_rows)
    weight = e_w.reshape(n_rows)
    token = jnp.arange(n_rows, dtype=jnp.int32) // TOP_K
    order = jnp.argsort(expert)
    e_sorted = expert[order]
    tok_sorted = token[order]
    counts = jnp.bincount(expert, length=N_EXPERTS)
    padded = (counts + MOE_BLOCK - 1) // MOE_BLOCK * MOE_BLOCK
    pad_end = jnp.cumsum(padded)
    pad_start = pad_end - padded
    start = jnp.cumsum(counts) - counts
    dest = pad_start[e_sorted] + jnp.arange(n_rows, dtype=jnp.int32) - start[e_sorted]
    n_blocks = -(-n_rows // MOE_BLOCK) + N_EXPERTS
    xs = jnp.zeros((n_blocks * MOE_BLOCK, d), h.dtype).at[dest].set(hf[tok_sorted])
    blk_expert = jnp.minimum(
        jnp.searchsorted(pad_end, jnp.arange(n_blocks) * MOE_BLOCK, side='right'), N_EXPERTS - 1)

    def expert_block(args):
        xb, e = args
        gate, up = jnp.split(xb @ w_gate_up[e], 2, axis=-1)
        return (jax.nn.silu(gate) * up) @ w_down[e]

    yb = lax.map(expert_block, (xs.reshape(n_blocks, MOE_BLOCK, d), blk_expert))
    y_rows = yb.reshape(n_blocks * MOE_BLOCK, d)[dest] * weight[order][:, None].astype(h.dtype)
    y = jax.ops.segment_sum(y_rows, tok_sorted, num_segments=n_tok)
    return y.reshape(bsz, seq, d)


def _trunk(x, c, w_ada, b_ada, norm_mix, norm_ffn, w_in, w_out,
           s5_lam_re, s5_lam_im, s5_log_step, s5_b_re, s5_b_im, s5_c_re, s5_c_im,
           s5_d, s5_w_glu, s5_b_glu, conv_w, gdn_conv_w, gdn_a_log, gdn_dt_bias, gdn_norm_w,
           moe_w_group, moe_b_group, moe_w_expert, moe_b_expert, moe_w_gate_up, moe_w_down,
           final_norm):
    split_points = _in_split_points()
    for l in range(DEPTH):
        mod = jax.nn.silu(c) @ w_ada[l] + b_ada[l]
        sh1, sc1, g1, sh2, sc2, g2 = jnp.split(mod[:, None, :], 6, axis=-1)
        h = rms_norm(x, norm_mix[l]) * (1 + sc1) + sh1
        proj = h @ w_in[l]
        u_s5, cx, cb, cc, qkv, z, a_f, a_b, b_f, b_b = jnp.split(proj, split_points, axis=-1)
        y_s5 = s5_mixer(u_s5, s5_lam_re[l], s5_lam_im[l], s5_log_step[l], s5_b_re[l], s5_b_im[l],
                        s5_c_re[l], s5_c_im[l], s5_d[l], s5_w_glu[l], s5_b_glu[l])
        y_conv = cb * conv3_centred(cc * cx, conv_w[l])
        y_gdn = gdn_mixer(qkv, z, a_f, a_b, b_f, b_b, gdn_conv_w[l], gdn_a_log[l],
                          gdn_dt_bias[l], gdn_norm_w[l])
        x = x + g1 * (jnp.concatenate([y_s5, y_conv, y_gdn], axis=-1) @ w_out[l])
        h = rms_norm(x, norm_ffn[l]) * (1 + sc2) + sh2
        x = x + g2 * hier_moe(h, moe_w_group[l], moe_b_group[l], moe_w_expert[l], moe_b_expert[l],
                              moe_w_gate_up[l], moe_w_down[l])
    return rms_norm(x, final_norm)


def setup_inputs(seed: int = 0) -> dict:
    key = jax.random.key(seed)
    ks = jax.random.split(key, 40)
    f32 = jnp.float32

    def nrm(k, shape, scale):
        return jax.random.normal(k, shape, f32) * scale

    L, G, P, H = DEPTH, S5_GROUPS, S5_STATE, S5_GROUP
    s5_log_step = jax.random.uniform(ks[10], (L, 2, G), f32, math.log(1e-3), math.log(1e-1))
    gdn_dt = jnp.exp(jax.random.uniform(ks[21], (L, 2, GDN_HEADS), f32, math.log(1e-3), math.log(1e-1)))
    return {
        'x_prompt': nrm(ks[0], (BATCH, SEQ, D_MODEL), 1.0),
        'x_sample': nrm(ks[1], (DEC_BATCH, DEC_SEQ, D_MODEL), 1.0),
        'c_prompt': nrm(ks[2], (BATCH, D_MODEL), 1.0),
        'c_sample': nrm(ks[3], (DEC_BATCH, D_MODEL), 1.0),
        'w_ada': nrm(ks[4], (L, D_MODEL, 6 * D_MODEL), 0.5 * D_MODEL ** -0.5),
        'b_ada': nrm(ks[5], (L, 6 * D_MODEL), 0.02),
        'norm_mix': 1.0 + nrm(ks[6], (L, D_MODEL), 0.02),
        'norm_ffn': 1.0 + nrm(ks[7], (L, D_MODEL), 0.02),
        'w_in': nrm(ks[8], (L, D_MODEL, IN_COLS), D_MODEL ** -0.5),
        'w_out': nrm(ks[9], (L, MIX_WIDTH, D_MODEL), MIX_WIDTH ** -0.5),
        's5_lam_re': -0.5 + nrm(ks[11], (L, 2, G, P), 0.01),
        's5_lam_im': jnp.pi * jnp.arange(P, dtype=f32) + nrm(ks[12], (L, 2, G, P), 0.01),
        's5_log_step': s5_log_step,
        's5_b_re': nrm(ks[13], (L, 2, G, P, H), (2 * H) ** -0.5),
        's5_b_im': nrm(ks[14], (L, 2, G, P, H), (2 * H) ** -0.5),
        's5_c_re': nrm(ks[15], (L, 2, G, H, P), P ** -0.5),
        's5_c_im': nrm(ks[16], (L, 2, G, H, P), P ** -0.5),
        's5_d': nrm(ks[17], (L, S5_WIDTH), 1.0),
        's5_w_glu': nrm(ks[18], (L, S5_WIDTH, S5_WIDTH), S5_WIDTH ** -0.5),
        's5_b_glu': nrm(ks[19], (L, S5_WIDTH), 0.02),
        'conv_w': nrm(ks[20], (L, SHORT_CONV, CONV_WIDTH), SHORT_CONV ** -0.5),
        'gdn_conv_w': nrm(ks[22], (L, SHORT_CONV, 3 * GDN_WIDTH), SHORT_CONV ** -0.5),
        'gdn_a_log': jnp.log(jax.random.uniform(ks[23], (L, 2, GDN_HEADS), f32, 1.0, 16.0)),
        'gdn_dt_bias': gdn_dt + jnp.log(-jnp.expm1(-gdn_dt)),
        'gdn_norm_w': 1.0 + nrm(ks[24], (L, GDN_HEAD_DIM), 0.02),
        'moe_w_group': nrm(ks[25], (L, D_MODEL, MOE_GROUPS), D_MODEL ** -0.5),
        'moe_b_group': nrm(ks[26], (L, MOE_GROUPS), 0.01),
        'moe_w_expert': nrm(ks[27], (L, D_MODEL, N_EXPERTS), D_MODEL ** -0.5),
        'moe_b_expert': nrm(ks[28], (L, N_EXPERTS), 0.01),
        'moe_w_gate_up': nrm(ks[29], (L, N_EXPERTS, D_MODEL, 2 * EXPERT_FF), D_MODEL ** -0.5),
        'moe_w_down': nrm(ks[30], (L, N_EXPERTS, EXPERT_FF, D_MODEL), EXPERT_FF ** -0.5),
        'final_norm': 1.0 + nrm(ks[31], (D_MODEL,), 0.02),
    }


def reference(x_prompt, x_sample, c_prompt, c_sample, w_ada, b_ada, norm_mix, norm_ffn, w_in, w_out,
              s5_lam_re, s5_lam_im, s5_log_step, s5_b_re, s5_b_im, s5_c_re, s5_c_im,
              s5_d, s5_w_glu, s5_b_glu, conv_w, gdn_conv_w, gdn_a_log, gdn_dt_bias, gdn_norm_w,
              moe_w_group, moe_b_group, moe_w_expert, moe_b_expert, moe_w_gate_up, moe_w_down,
              final_norm):
    params = (w_ada, b_ada, norm_mix, norm_ffn, w_in, w_out,
              s5_lam_re, s5_lam_im, s5_log_step, s5_b_re, s5_b_im, s5_c_re, s5_c_im,
              s5_d, s5_w_glu, s5_b_glu, conv_w, gdn_conv_w, gdn_a_log, gdn_dt_bias, gdn_norm_w,
              moe_w_group, moe_b_group, moe_w_expert, moe_b_expert, moe_w_gate_up, moe_w_down,
              final_norm)
    y_prompt = _trunk(x_prompt, c_prompt, *params)
    y_sample = _trunk(x_sample, c_sample, *params)
    return (y_prompt, y_sample)
```

```python
import functools

import jax
import jax.numpy as jnp
from jax import lax
from jax.experimental import pallas as pl
from jax.experimental.pallas import tpu as pltpu

F32 = jnp.float32
BF16 = jnp.bfloat16
HIGHEST = lax.Precision.HIGHEST

D_MODEL = 1024
DEPTH = 4
S5_WIDTH = 256
S5_GROUP = 16
S5_GROUPS = 16
S5_STATE = 64
S5_LANES = S5_GROUPS * S5_STATE
CONV_WIDTH = 256
GDN_WIDTH = 512
GDN_HEADS = 4
GDN_HEAD_DIM = 128
GDN_CHUNK = 64
PROJ_MAIN = S5_WIDTH + 3 * CONV_WIDTH + 4 * GDN_WIDTH
N_GATE_COLS = 4 * GDN_HEADS
MOE_GROUPS = 4
EXPERTS_PER_GROUP = 8
N_EXPERTS = 32
EXPERT_FF = 512
MOE_BLOCK = 512
EPS = 1e-6

SUBLANES = 8
LANES = 128
SLAB = (SUBLANES, LANES)
VMEM_LIMIT = 56 * 1024 * 1024

ROW_TILE = 512
S5_TIME_TILE = 64
GDN_BLOCK_CHUNKS = 2
ROUTE_TILE = 512
COMBINE_TILE = 256


def _cparams(*sem):
    return pltpu.CompilerParams(dimension_semantics=sem, vmem_limit_bytes=VMEM_LIMIT)


def _dot(a, b, **kw):
    return jnp.dot(a, b, preferred_element_type=F32, **kw)


def _dot_nt(a, b, **kw):
    return lax.dot_general(a, b, (((1,), (1,)), ((), ())), preferred_element_type=F32, **kw)


def _dot_tn(a, b, **kw):
    return lax.dot_general(a, b, (((0,), (0,)), ((), ())), preferred_element_type=F32, **kw)


def _silu(x):
    return x * jax.nn.sigmoid(x)


def _gelu_tanh(x):
    return 0.5 * x * (1.0 + jnp.tanh(0.7978845608028654 * (x + 0.044715 * (x * x * x))))


def _softplus(x):
    return jnp.maximum(x, 0.0) + jnp.log(1.0 + jnp.exp(-jnp.abs(x)))


def _rms(x):
    return x * lax.rsqrt(jnp.mean(x * x, axis=-1, keepdims=True) + EPS)


def _ada_kernel(c_ref, w_ref, b_ref, o_ref):
    o_ref[0] = _dot(_silu(c_ref[...]), w_ref[0], precision=HIGHEST) + b_ref[0]


def ada_modulation(c_all, w_ada, b_ada):
    nb = c_all.shape[0]
    depth, d, n = w_ada.shape
    tn = 1536
    return pl.pallas_call(
        _ada_kernel,
        grid=(depth, n // tn),
        in_specs=[pl.BlockSpec((nb, d), lambda l, j: (0, 0)),
                  pl.BlockSpec((1, d, tn), lambda l, j: (l, 0, j)),
                  pl.BlockSpec((1, 1, tn), lambda l, j: (l, 0, j))],
        out_specs=pl.BlockSpec((1, nb, tn), lambda l, j: (l, 0, j)),
        out_shape=jax.ShapeDtypeStruct((depth, nb, n), F32),
        compiler_params=_cparams("parallel", "parallel"),
        name="ada_modulation",
    )(c_all, w_ada, b_ada.reshape(depth, 1, n))


def _in_proj_kernel(x_ref, mod_ref, nw_ref, w_ref, wg_ref, proj_ref, gate_ref):
    h = _rms(x_ref[...]) * nw_ref[...] * (1.0 + mod_ref[0, 1:2, :]) + mod_ref[0, 0:1, :]
    hb = h.astype(BF16)
    proj_ref[...] = _dot(hb, w_ref[...])
    gate_ref[...] = _dot(hb, wg_ref[...])


def in_projection(x2d, mod, norm_w, w_main, w_gate, seq):
    m, d = x2d.shape
    tm = ROW_TILE
    per_seq = seq // tm
    return pl.pallas_call(
        _in_proj_kernel,
        grid=(m // tm,),
        in_specs=[pl.BlockSpec((tm, d), lambda i: (i, 0)),
                  pl.BlockSpec((1, 6, d), lambda i: (i // per_seq, 0, 0)),
                  pl.BlockSpec((1, d), lambda i: (0, 0)),
                  pl.BlockSpec((d, PROJ_MAIN), lambda i: (0, 0)),
                  pl.BlockSpec((d, LANES), lambda i: (0, 0))],
        out_specs=[pl.BlockSpec((tm, PROJ_MAIN), lambda i: (i, 0)),
                   pl.BlockSpec((tm, LANES), lambda i: (i, 0))],
        out_shape=[jax.ShapeDtypeStruct((m, PROJ_MAIN), F32),
                   jax.ShapeDtypeStruct((m, LANES), F32)],
        compiler_params=_cparams("parallel"),
        name="in_projection",
    )(x2d, mod, norm_w, w_main, w_gate)


def _s5_scan(x_ref, h0_re, h0_im, a_re, a_im, steps, reverse):
    def body(s, carry):
        hr, hi = carry
        t = (steps - 1 - s) if reverse else s
        rows = pl.ds(pl.multiple_of(t * SUBLANES, SUBLANES), SUBLANES)
        xr = x_ref[rows, 0:S5_LANES]
        xi = x_ref[rows, S5_LANES:2 * S5_LANES]
        nr = a_re * hr - a_im * hi + xr
        ni = a_re * hi + a_im * hr + xi
        x_ref[rows, 0:S5_LANES] = nr
        x_ref[rows, S5_LANES:2 * S5_LANES] = ni
        return nr, ni
    return lax.fori_loop(0, steps, body, (h0_re, h0_im), unroll=4)


def _s5_direction(u_ref, a_ref, bmat_ref, cmat_ref, x_ref, st_ref, reverse):
    tc, nb, w = u_ref.shape
    first = pl.program_id(0) == 0

    @pl.when(first)
    def _():
        st_ref[...] = jnp.zeros_like(st_ref)

    u = u_ref[...].reshape(tc * nb, w)
    x_ref[...] = _dot(u.astype(BF16), bmat_ref[...])
    a_re = jnp.broadcast_to(a_ref[0:1, :], (nb, S5_LANES))
    a_im = jnp.broadcast_to(a_ref[1:2, :], (nb, S5_LANES))
    hr, hi = _s5_scan(x_ref, st_ref[0], st_ref[1], a_re, a_im, tc, reverse)
    st_ref[0] = hr
    st_ref[1] = hi
    return u, _dot(x_ref[...].astype(BF16), cmat_ref[...])


def _s5_fwd_kernel(u_ref, a_ref, bmat_ref, cmat_ref, y_ref, x_ref, st_ref):
    _, y = _s5_direction(u_ref, a_ref, bmat_ref, cmat_ref, x_ref, st_ref, reverse=False)
    y_ref[...] = y.reshape(y_ref.shape)


def _s5_bwd_kernel(u_ref, yf_ref, a_ref, bmat_ref, cmat_ref, d_ref, wglu_ref, bglu_ref, o_ref, x_ref, st_ref):
    u, y = _s5_direction(u_ref, a_ref, bmat_ref, cmat_ref, x_ref, st_ref, reverse=True)
    y = y + u * d_ref[...] + yf_ref[...].reshape(u.shape)
    y = _gelu_tanh(y)
    out = y * jax.nn.sigmoid(_dot(y.astype(BF16), wglu_ref[...]) + bglu_ref[...])
    o_ref[...] = out.reshape(o_ref.shape)


def s5_mixer(u_t, a_f, bmat_f, cmat_f, a_b, bmat_b, cmat_b, d_skip, w_glu, b_glu):
    seq, nb, w = u_t.shape
    assert nb == SUBLANES, "the scan keeps one batch row per sublane"
    tc = S5_TIME_TILE
    n = seq // tc
    rows = tc * nb
    const2 = lambda i: (0, 0)
    par_specs = [pl.BlockSpec((2, S5_LANES), const2),
                 pl.BlockSpec((w, 2 * S5_LANES), const2),
                 pl.BlockSpec((2 * S5_LANES, w), const2)]
    scratch = [pltpu.VMEM((rows, 2 * S5_LANES), F32), pltpu.VMEM((2, nb, S5_LANES), F32)]
    y_f = pl.pallas_call(
        _s5_fwd_kernel,
        grid=(n,),
        in_specs=[pl.BlockSpec((tc, nb, w), lambda i: (i, 0, 0))] + par_specs,
        out_specs=pl.BlockSpec((tc, nb, w), lambda i: (i, 0, 0)),
        out_shape=jax.ShapeDtypeStruct((seq, nb, w), F32),
        scratch_shapes=scratch,
        compiler_params=_cparams("arbitrary"),
        name="s5_forward",
    )(u_t, a_f, bmat_f, cmat_f)
    rev = lambda i: (n - 1 - i, 0, 0)
    return pl.pallas_call(
        _s5_bwd_kernel,
        grid=(n,),
        in_specs=[pl.BlockSpec((tc, nb, w), rev), pl.BlockSpec((tc, nb, w), rev)] + par_specs
                 + [pl.BlockSpec((1, w), const2), pl.BlockSpec((w, w), const2), pl.BlockSpec((1, w), const2)],
        out_specs=pl.BlockSpec((tc, nb, w), rev),
        out_shape=jax.ShapeDtypeStruct((seq, nb, w), F32),
        scratch_shapes=scratch,
        compiler_params=_cparams("arbitrary"),
        name="s5_backward_glu",
    )(u_t, y_f, a_b, bmat_b, cmat_b, d_skip, w_glu, b_glu)


def s5_discretise(lam_re, lam_im, log_step, b_re, b_im, c_re, c_im):
    g, p, h = S5_GROUPS, S5_STATE, S5_GROUP
    lr, li = lam_re.astype(F32), lam_im.astype(F32)
    dt = jnp.exp(log_step.astype(F32))[:, None]
    mag = jnp.exp(lr * dt)
    abr, abi = mag * jnp.cos(li * dt), mag * jnp.sin(li * dt)
    den = lr * lr + li * li
    cr = ((abr - 1.0) * lr + abi * li) / den
    ci = (abi * lr - (abr - 1.0) * li) / den
    bbr = cr[..., None] * b_re - ci[..., None] * b_im
    bbi = cr[..., None] * b_im + ci[..., None] * b_re
    eye = jnp.eye(g, dtype=F32)
    to_b = lambda t: jnp.einsum('gph,gk->ghkp', t, eye).reshape(g * h, g * p)
    bmat = jnp.concatenate([to_b(bbr), to_b(bbi)], axis=1)
    to_c = lambda t: jnp.einsum('ghp,gk->gpkh', t, eye).reshape(g * p, g * h)
    cmat = jnp.concatenate([to_c(c_re.astype(F32)), -to_c(c_im.astype(F32))], axis=0)
    a = jnp.stack([abr.reshape(g * p), abi.reshape(g * p)])
    return a, bmat.astype(BF16), cmat.astype(BF16)


def _conv3_rows(x, prev_row, next_row, w_ref):
    tm = x.shape[0]
    rows = lax.broadcasted_iota(jnp.int32, x.shape, 0)
    xp = jnp.where(rows == 0, prev_row, pltpu.roll(x, 1, 0))
    xn = jnp.where(rows == tm - 1, next_row, pltpu.roll(x, tm - 1, 0))
    return xp * w_ref[0:1, :] + x * w_ref[1:2, :] + xn * w_ref[2:3, :]


def _halo_specs(tm, width, col_block, n_rows):
    per = tm // SUBLANES
    last = n_rows // SUBLANES - 1
    prev = pl.BlockSpec((SUBLANES, width), lambda i, *_: (jnp.maximum(i * per - 1, 0), col_block(*_)))
    nxt = pl.BlockSpec((SUBLANES, width), lambda i, *_: (jnp.minimum((i + 1) * per, last), col_block(*_)))
    return prev, nxt


def _gdn_prep_kernel(x_ref, xp_ref, xn_ref, w_ref, o_ref, *, seq):
    tm = x_ref.shape[0]
    j = pl.program_id(1)
    t0 = (pl.program_id(0) * tm) % seq
    prev_row = jnp.where(t0 == 0, 0.0, xp_ref[SUBLANES - 1:SUBLANES, :])
    next_row = jnp.where(t0 + tm == seq, 0.0, xn_ref[0:1, :])
    y = _silu(_conv3_rows(x_ref[...], prev_row, next_row, w_ref))

    @pl.when(j == 2)
    def _():
        o_ref[...] = y

    @pl.when(j < 2)
    def _():
        scale = jnp.where(j == 0, GDN_HEAD_DIM ** -0.5, 1.0)
        for h in range(GDN_HEADS):
            cols = slice(h * GDN_HEAD_DIM, (h + 1) * GDN_HEAD_DIM)
            yh = y[:, cols]
            o_ref[:, cols] = yh * (lax.rsqrt(jnp.sum(yh * yh, axis=-1, keepdims=True) + EPS) * scale)


def gdn_prepare(proj, conv_w, seq):
    m = proj.shape[0]
    tm = ROW_TILE
    first = (S5_WIDTH + 3 * CONV_WIDTH) // GDN_WIDTH
    prev, nxt = _halo_specs(tm, GDN_WIDTH, lambda j: first + j, m)
    return pl.pallas_call(
        functools.partial(_gdn_prep_kernel, seq=seq),
        grid=(m // tm, 3),
        in_specs=[pl.BlockSpec((tm, GDN_WIDTH), lambda i, j: (i, first + j)), prev, nxt,
                  pl.BlockSpec((3, GDN_WIDTH), lambda i, j: (0, j))],
        out_specs=pl.BlockSpec((tm, GDN_WIDTH), lambda i, j: (i, j)),
        out_shape=jax.ShapeDtypeStruct((m, 3 * GDN_WIDTH), F32),
        compiler_params=_cparams("parallel", "parallel"),
        name="gdn_prepare",
    )(proj, proj, proj, conv_w)


def _gdn_gates_kernel(pre_ref, alog_ref, dtb_ref, o_ref):
    seq = pre_ref.shape[2]
    nh2 = 2 * GDN_HEADS
    g = -jnp.exp(alog_ref[...]) * _softplus(pre_ref[0, 0:nh2, :] + dtb_ref[...])
    o_ref[0, nh2:2 * nh2, :] = jax.nn.sigmoid(pre_ref[0, nh2:2 * nh2, :])
    s = lax.broadcasted_iota(jnp.int32, (LANES, LANES), 0)
    t = lax.broadcasted_iota(jnp.int32, (LANES, LANES), 1)
    same = (s // GDN_CHUNK) == (t // GDN_CHUNK)
    m_f = jnp.where(same & (s <= t), 1.0, 0.0)
    m_b = jnp.where(same & (s >= t), 1.0, 0.0)
    for jb in range(seq // LANES):
        cols = slice(jb * LANES, (jb + 1) * LANES)
        gb = g[:, cols]
        o_ref[0, 0:GDN_HEADS, cols] = _dot(gb, m_f, precision=HIGHEST)[0:GDN_HEADS]
        o_ref[0, GDN_HEADS:nh2, cols] = _dot(gb, m_b, precision=HIGHEST)[GDN_HEADS:nh2]


def gdn_gates(pre_t, a_log, dt_bias):
    nb, rows, seq = pre_t.shape
    col = lambda t: t.astype(F32).reshape(2 * GDN_HEADS, 1)
    return pl.pallas_call(
        _gdn_gates_kernel,
        grid=(nb,),
        in_specs=[pl.BlockSpec((1, rows, seq), lambda b: (b, 0, 0)),
                  pl.BlockSpec((2 * GDN_HEADS, 1), lambda b: (0, 0)),
                  pl.BlockSpec((2 * GDN_HEADS, 1), lambda b: (0, 0))],
        out_specs=pl.BlockSpec((1, rows, seq), lambda b: (b, 0, 0)),
        out_shape=jax.ShapeDtypeStruct((nb, rows, seq), F32),
        compiler_params=_cparams("parallel"),
        name="gdn_gates",
    )(pre_t, col(a_log), col(dt_bias))


def _unit_lower_inverse(a):
    n = a.shape[0]
    eye = jnp.where(lax.broadcasted_iota(jnp.int32, (n, n), 0) == lax.broadcasted_iota(jnp.int32, (n, n), 1), 1.0, 0.0)
    p = eye - a
    ab = a.astype(BF16)
    sq = _dot(ab, ab)
    k = 2
    while True:
        sqb = sq.astype(BF16)
        p = p + _dot(p.astype(BF16), sqb)
        k *= 2
        if k >= n:
            return p
        sq = _dot(sqb, sqb)


def _gdn_chunk(q, k, v, gc_col, gc_row, beta_col, state, lower):
    c = GDN_CHUNK
    ii = lax.broadcasted_iota(jnp.int32, (c, c), 0)
    jj = lax.broadcasted_iota(jnp.int32, (c, c), 1)
    incl = (ii >= jj) if lower else (ii <= jj)
    strict = (ii > jj) if lower else (ii < jj)
    decay = jnp.exp(jnp.where(incl, gc_col - gc_row, -1e30))
    kb = k * beta_col
    k16 = k.astype(BF16)
    a = jnp.where(strict, _dot_nt(kb.astype(BF16), k16) * decay, 0.0)
    t_inv = _unit_lower_inverse(a)
    eg = jnp.exp(gc_col)
    rhs = jnp.concatenate([v * beta_col, kb * eg], axis=1)
    uw = _dot(t_inv.astype(BF16), rhs.astype(BF16))
    u, w = uw[:, :GDN_HEAD_DIM], uw[:, GDN_HEAD_DIM:]
    intra = jnp.where(incl, _dot_nt(q.astype(BF16), k16) * decay, 0.0)
    g_last = gc_col[c - 1:c, :] if lower else gc_col[0:1, :]
    k_end = k * jnp.exp(g_last - gc_col)
    q_dec = q * eg
    sw = _dot(jnp.concatenate([w, q_dec], axis=0).astype(BF16), state.astype(BF16))
    v_new = u - sw[:c]
    v16 = v_new.astype(BF16)
    out = sw[c:] + _dot(intra.astype(BF16), v16)
    state = state * jnp.exp(g_last) + _dot_tn(k_end.astype(BF16), v16)
    return out, state


def _gdn_kernel(qf, kf, vf, gcf, grf, qb, kb, vb, gcb, grb, of_ref, ob_ref, st_ref):
    @pl.when(pl.program_id(2) == 0)
    def _():
        st_ref[...] = jnp.zeros_like(st_ref)

    nc = qf.shape[1] // GDN_CHUNK
    for d, (q_ref, k_ref, v_ref, gc_ref, gr_ref, o_ref) in enumerate(
            ((qf, kf, vf, gcf, grf, of_ref), (qb, kb, vb, gcb, grb, ob_ref))):
        state = st_ref[d]
        for ci in (range(nc) if d == 0 else range(nc - 1, -1, -1)):
            rows = slice(ci * GDN_CHUNK, (ci + 1) * GDN_CHUNK)
            out, state = _gdn_chunk(
                q_ref[0, rows, :], k_ref[0, rows, :], v_ref[0, rows, :],
                gc_ref[0, 0, rows, d:d + 1], gr_ref[0, 0, d:d + 1, rows], gc_ref[0, 0, rows, 2 + d:3 + d],
                state, lower=(d == 0))
            o_ref[0, rows, :] = out
        st_ref[d] = state


def gdn_scan(qkv, gates_col, gates_row):
    nb, seq, _ = qkv.shape
    r = GDN_BLOCK_CHUNKS * GDN_CHUNK
    n = seq // r
    hd = GDN_HEAD_DIM

    def specs(pos):
        return [pl.BlockSpec((1, r, hd), lambda b, h, i: (b, pos(i), h)),
                pl.BlockSpec((1, r, hd), lambda b, h, i: (b, pos(i), GDN_HEADS + h)),
                pl.BlockSpec((1, r, hd), lambda b, h, i: (b, pos(i), 2 * GDN_HEADS + h)),
                pl.BlockSpec((1, 1, r, 4), lambda b, h, i: (b, h, pos(i), 0)),
                pl.BlockSpec((1, 1, 4, r), lambda b, h, i: (b, h, 0, pos(i)))]

    fwd = lambda i: i
    bwd = lambda i: n - 1 - i
    out_sd = jax.ShapeDtypeStruct((nb, seq, GDN_WIDTH), F32)
    return pl.pallas_call(
        _gdn_kernel,
        grid=(nb, GDN_HEADS, n),
        in_specs=specs(fwd) + specs(bwd),
        out_specs=[pl.BlockSpec((1, r, hd), lambda b, h, i: (b, i, h)),
                   pl.BlockSpec((1, r, hd), lambda b, h, i: (b, n - 1 - i, h))],
        out_shape=[out_sd, out_sd],
        scratch_shapes=[pltpu.VMEM((2, hd, hd), F32)],
        compiler_params=_cparams("parallel", "parallel", "arbitrary"),
        name="gdn_scan",
    )(qkv, qkv, qkv, gates_col, gates_row, qkv, qkv, qkv, gates_col, gates_row)


def _out_proj_kernel(x_ref, mod_ref, ys5_ref, cx_ref, cb_ref, cc_ref, cxp_ref, cxn_ref, ccp_ref, ccn_ref,
                     cw_ref, of_ref, ob_ref, z_ref, gnw_ref, w_ref, o_ref, *, seq):
    tm = x_ref.shape[0]
    t0 = (pl.program_id(0) * tm) % seq
    last = SUBLANES - 1
    prev_row = jnp.where(t0 == 0, 0.0, ccp_ref[last:SUBLANES, :] * cxp_ref[last:SUBLANES, :])
    next_row = jnp.where(t0 + tm == seq, 0.0, ccn_ref[0:1, :] * cxn_ref[0:1, :])
    y_conv = cb_ref[...] * _conv3_rows(cc_ref[...] * cx_ref[...], prev_row, next_row, cw_ref)
    o = of_ref[...] + ob_ref[...]
    z = z_ref[...]
    heads = []
    for h in range(GDN_HEADS):
        cols = slice(h * GDN_HEAD_DIM, (h + 1) * GDN_HEAD_DIM)
        heads.append(_rms(o[:, cols]) * gnw_ref[...] * _silu(z[:, cols]))
    y_gdn = jnp.concatenate(heads, axis=1)
    c0, c1 = S5_WIDTH, S5_WIDTH + CONV_WIDTH
    mix = (_dot(ys5_ref[...].astype(BF16), w_ref[0:c0, :])
           + _dot(y_conv.astype(BF16), w_ref[c0:c1, :])
           + _dot(y_gdn.astype(BF16), w_ref[c1:, :]))
    o_ref[...] = x_ref[...] + mod_ref[0, 2:3, :] * mix


def out_projection(x2d, mod, y_s5, proj, conv_w, o_f, o_b, gdn_norm_w, w_out, seq):
    m, d = x2d.shape
    tm = ROW_TILE
    per_seq = seq // tm
    cw = CONV_WIDTH
    row = lambda c: pl.BlockSpec((tm, cw), lambda i: (i, c))
    cxp, cxn = _halo_specs(tm, cw, lambda: 1, m)
    ccp, ccn = _halo_specs(tm, cw, lambda: 3, m)
    gw = GDN_WIDTH
    return pl.pallas_call(
        functools.partial(_out_proj_kernel, seq=seq),
        grid=(m // tm,),
        in_specs=[pl.BlockSpec((tm, d), lambda i: (i, 0)),
                  pl.BlockSpec((1, 6, d), lambda i: (i // per_seq, 0, 0)),
                  pl.BlockSpec((tm, cw), lambda i: (i, 0)),
                  row(1), row(2), row(3), cxp, cxn, ccp, ccn,
                  pl.BlockSpec((3, cw), lambda i: (0, 0)),
                  pl.BlockSpec((tm, gw), lambda i: (i, 0)),
                  pl.BlockSpec((tm, gw), lambda i: (i, 0)),
                  pl.BlockSpec((tm, gw), lambda i: (i, (PROJ_MAIN - gw) // gw)),
                  pl.BlockSpec((1, GDN_HEAD_DIM), lambda i: (0, 0)),
                  pl.BlockSpec((d, d), lambda i: (0, 0))],
        out_specs=pl.BlockSpec((tm, d), lambda i: (i, 0)),
        out_shape=jax.ShapeDtypeStruct((m, d), F32),
        compiler_params=_cparams("parallel"),
        name="out_projection",
    )(x2d, mod, y_s5, proj, proj, proj, proj, proj, proj, proj, conv_w, o_f, o_b, proj, gdn_norm_w, w_out)


def _first_argmax(vals, rows, n):
    m = jnp.max(vals, axis=0, keepdims=True)
    return m, jnp.min(jnp.where(vals == m, rows, n), axis=0, keepdims=True)


def _route_kernel(x_ref, mod_ref, nw_ref, wr_ref, br_ref, h_ref, idx_ref, wgt_ref, cnt_ref, carry_ref):
    step = pl.program_id(0)

    @pl.when(step == 0)
    def _():
        carry_ref[...] = jnp.zeros_like(carry_ref)

    tm = x_ref.shape[0]
    h = _rms(x_ref[...]) * nw_ref[...] * (1.0 + mod_ref[0, 4:5, :]) + mod_ref[0, 3:4, :]
    for s in range(SUBLANES):
        h_ref[:, s, :] = h[:, s * LANES:(s + 1) * LANES]
    logits = _dot_nt(wr_ref[...], h, precision=HIGHEST) + br_ref[...]
    lg = logits[N_EXPERTS:N_EXPERTS + MOE_GROUPS, :]
    rows_g = lax.broadcasted_iota(jnp.int32, lg.shape, 0)
    g_max, g_idx = _first_argmax(lg, rows_g, MOE_GROUPS)
    g_w = 1.0 / jnp.sum(jnp.exp(lg - g_max), axis=0, keepdims=True)
    le = jnp.zeros((EXPERTS_PER_GROUP, tm), F32)
    for g in range(MOE_GROUPS):
        le = jnp.where(g_idx == g, logits[g * EXPERTS_PER_GROUP:(g + 1) * EXPERTS_PER_GROUP, :], le)
    rows_e = lax.broadcasted_iota(jnp.int32, le.shape, 0)
    m1, i1 = _first_argmax(le, rows_e, EXPERTS_PER_GROUP)
    m2, i2 = _first_argmax(jnp.where(rows_e == i1, -jnp.inf, le), rows_e, EXPERTS_PER_GROUP)
    r = jnp.exp(m2 - m1)
    w1 = g_w / (1.0 + r)
    e1 = g_idx * EXPERTS_PER_GROUP + i1
    e2 = g_idx * EXPERTS_PER_GROUP + i2
    rows_x = lax.broadcasted_iota(jnp.int32, (N_EXPERTS, tm), 0)
    hit1 = rows_x == e1
    hit2 = rows_x == e2
    onehot = jnp.where(hit1 | hit2, 1.0, 0.0)
    before = lax.broadcasted_iota(jnp.int32, (tm, tm), 0) < lax.broadcasted_iota(jnp.int32, (tm, tm), 1)
    ahead = _dot(onehot.astype(BF16), jnp.where(before, 1.0, 0.0).astype(BF16)) + carry_ref[:, 0:1]
    rank1 = jnp.sum(jnp.where(hit1, ahead, 0.0), axis=0, keepdims=True)
    rank2 = jnp.sum(jnp.where(hit2, ahead, 0.0), axis=0, keepdims=True)
    idx_ref[0:1, :] = e1
    idx_ref[1:2, :] = e2
    idx_ref[2:3, :] = rank1.astype(jnp.int32)
    idx_ref[3:4, :] = rank2.astype(jnp.int32)
    wgt_ref[0:1, :] = w1
    wgt_ref[1:2, :] = w1 * r
    carry_ref[...] = carry_ref[...] + jnp.sum(onehot, axis=1, keepdims=True)
    cnt_ref[...] = carry_ref[...]


def moe_route(x2d, mod, norm_w, w_router_t, b_router, seq):
    m, d = x2d.shape
    tm = ROUTE_TILE
    per_seq = seq // tm
    nr = w_router_t.shape[0]
    return pl.pallas_call(
        _route_kernel,
        grid=(m // tm,),
        in_specs=[pl.BlockSpec((tm, d), lambda i: (i, 0)),
                  pl.BlockSpec((1, 6, d), lambda i: (i // per_seq, 0, 0)),
                  pl.BlockSpec((1, d), lambda i: (0, 0)),
                  pl.BlockSpec((nr, d), lambda i: (0, 0)),
                  pl.BlockSpec((nr, 1), lambda i: (0, 0))],
        out_specs=[pl.BlockSpec((tm,) + SLAB, lambda i: (i, 0, 0)),
                   pl.BlockSpec((4, tm), lambda i: (0, i)),
                   pl.BlockSpec((2, tm), lambda i: (0, i)),
                   pl.BlockSpec((N_EXPERTS, LANES), lambda i: (0, 0))],
        out_shape=[jax.ShapeDtypeStruct((m,) + SLAB, F32),
                   jax.ShapeDtypeStruct((4, m), jnp.int32),
                   jax.ShapeDtypeStruct((2, m), F32),
                   jax.ShapeDtypeStruct((N_EXPERTS, LANES), F32)],
        scratch_shapes=[pltpu.VMEM((N_EXPERTS, LANES), F32)],
        compiler_params=_cparams("arbitrary"),
        name="moe_route",
    )(x2d, mod, norm_w, w_router_t, b_router)


def _row_copy(src_ref, src_row, dst_ref, dst_row, sem):
    return pltpu.make_async_copy(src_ref.at[pl.ds(src_row, 1)], dst_ref.at[pl.ds(dst_row, 1)], sem)


def _scatter_kernel(d1_ref, d2_ref, h_ref, xs_in_ref, xs_ref, sem):
    del xs_in_ref
    tm = h_ref.shape[0]

    def start(r, c):
        _row_copy(h_ref, r, xs_ref, d1_ref[0, 0, r], sem).start()
        _row_copy(h_ref, r, xs_ref, d2_ref[0, 0, r], sem).start()
        return c

    def wait(r, c):
        _row_copy(h_ref, 0, xs_ref, 0, sem).wait()
        _row_copy(h_ref, 0, xs_ref, 0, sem).wait()
        return c

    lax.fori_loop(0, tm, start, 0)
    lax.fori_loop(0, tm, wait, 0)


def moe_scatter(h_slab, dest1, dest2, n_slots):
    m = h_slab.shape[0]
    tm = ROUTE_TILE
    idx_spec = pl.BlockSpec((1, 1, tm), lambda i: (i, 0, 0), memory_space=pltpu.SMEM)
    return pl.pallas_call(
        _scatter_kernel,
        grid=(m // tm,),
        in_specs=[idx_spec, idx_spec,
                  pl.BlockSpec((tm,) + SLAB, lambda i: (i, 0, 0)),
                  pl.BlockSpec(memory_space=pl.ANY)],
        out_specs=pl.BlockSpec(memory_space=pl.ANY),
        out_shape=jax.ShapeDtypeStruct((n_slots,) + SLAB, F32),
        scratch_shapes=[pltpu.SemaphoreType.DMA(())],
        input_output_aliases={3: 0},
        compiler_params=_cparams("arbitrary"),
        name="moe_scatter",
    )(dest1.reshape(m // tm, 1, tm), dest2.reshape(m // tm, 1, tm), h_slab, jnp.zeros((n_slots,) + SLAB, F32))


def _expert_kernel(be_ref, nused_ref, xs_ref, wgu_ref, wd_ref, y_ref, wgu16_ref, wd16_ref):
    i = pl.program_id(0)
    changed = jnp.logical_or(i == 0, be_ref[i] != be_ref[jnp.maximum(i - 1, 0)])

    @pl.when(changed)
    def _():
        wgu16_ref[...] = wgu_ref[0].astype(BF16)
        wd16_ref[...] = wd_ref[0].astype(BF16)

    @pl.when(i < nused_ref[0])
    def _():
        x = jnp.concatenate([xs_ref[:, s, :] for s in range(SUBLANES)], axis=1).astype(BF16)
        gu = _dot(x, wgu16_ref[...])
        act = _silu(gu[:, :EXPERT_FF]) * gu[:, EXPERT_FF:]
        y = _dot(act.astype(BF16), wd16_ref[...])
        for s in range(SUBLANES):
            y_ref[:, s, :] = y[:, s * LANES:(s + 1) * LANES]

    @pl.when(i >= nused_ref[0])
    def _():
        y_ref[...] = jnp.zeros_like(y_ref)


def moe_experts(xs, blk_expert, n_used, w_gate_up, w_down):
    n_slots = xs.shape[0]
    n_blocks = n_slots // MOE_BLOCK
    d = D_MODEL
    return pl.pallas_call(
        _expert_kernel,
        grid_spec=pltpu.PrefetchScalarGridSpec(
            num_scalar_prefetch=2,
            grid=(n_blocks,),
            in_specs=[pl.BlockSpec((MOE_BLOCK,) + SLAB, lambda i, be, nu: (i, 0, 0)),
                      pl.BlockSpec((1, d, 2 * EXPERT_FF), lambda i, be, nu: (be[i], 0, 0)),
                      pl.BlockSpec((1, EXPERT_FF, d), lambda i, be, nu: (be[i], 0, 0))],
            out_specs=pl.BlockSpec((MOE_BLOCK,) + SLAB, lambda i, be, nu: (i, 0, 0)),
            scratch_shapes=[pltpu.VMEM((d, 2 * EXPERT_FF), BF16), pltpu.VMEM((EXPERT_FF, d), BF16)]),
        out_shape=jax.ShapeDtypeStruct((n_slots,) + SLAB, F32),
        compiler_params=_cparams("arbitrary"),
        name="moe_experts",
    )(blk_expert, n_used, xs, w_gate_up, w_down)


def _combine_kernel(d1_ref, d2_ref, x_ref, mod_ref, wgt_ref, fnw_ref, yb_ref, o_ref, y1_ref, y2_ref, sem, *, final):
    tm = x_ref.shape[0]

    def start(r, c):
        _row_copy(yb_ref, d1_ref[0, 0, r], y1_ref, r, sem).start()
        _row_copy(yb_ref, d2_ref[0, 0, r], y2_ref, r, sem).start()
        return c

    def wait(r, c):
        _row_copy(yb_ref, 0, y1_ref, 0, sem).wait()
        _row_copy(yb_ref, 0, y2_ref, 0, sem).wait()
        return c

    lax.fori_loop(0, tm, start, 0)
    lax.fori_loop(0, tm, wait, 0)
    w1 = wgt_ref[:, 0:1]
    w2 = wgt_ref[:, 1:2]
    for s in range(SUBLANES):
        cols = slice(s * LANES, (s + 1) * LANES)
        y = y1_ref[:, s, :] * w1 + y2_ref[:, s, :] * w2
        o_ref[:, cols] = x_ref[:, cols] + mod_ref[0, 5:6, cols] * y
    if final:
        o_ref[...] = _rms(o_ref[...]) * fnw_ref[...]


def moe_combine(x2d, mod, yb, dest1, dest2, weights, final_norm_w, seq, final):
    m, d = x2d.shape
    tm = COMBINE_TILE
    per_seq = seq // tm
    idx_spec = pl.BlockSpec((1, 1, tm), lambda i: (i, 0, 0), memory_space=pltpu.SMEM)
    return pl.pallas_call(
        functools.partial(_combine_kernel, final=final),
        grid=(m // tm,),
        in_specs=[idx_spec, idx_spec,
                  pl.BlockSpec((tm, d), lambda i: (i, 0)),
                  pl.BlockSpec((1, 6, d), lambda i: (i // per_seq, 0, 0)),
                  pl.BlockSpec((tm, 2), lambda i: (i, 0)),
                  pl.BlockSpec((1, d), lambda i: (0, 0)),
                  pl.BlockSpec(memory_space=pl.ANY)],
        out_specs=pl.BlockSpec((tm, d), lambda i: (i, 0)),
        out_shape=jax.ShapeDtypeStruct((m, d), F32),
        scratch_shapes=[pltpu.VMEM((tm,) + SLAB, F32), pltpu.VMEM((tm,) + SLAB, F32),
                        pltpu.SemaphoreType.DMA(())],
        compiler_params=_cparams("arbitrary"),
        name="moe_combine",
    )(dest1.reshape(m // tm, 1, tm), dest2.reshape(m // tm, 1, tm), x2d, mod, weights, final_norm_w, yb)


def hier_moe_layer(x2d, mod, norm_w, w_router_t, b_router, w_gate_up, w_down, final_norm_w, seq, final):
    m = x2d.shape[0]
    h_slab, idx, wgt, counts = moe_route(x2d, mod, norm_w, w_router_t, b_router, seq)
    counts = counts[:, 0].astype(jnp.int32)
    padded = (counts + MOE_BLOCK - 1) // MOE_BLOCK * MOE_BLOCK
    pad_end = jnp.cumsum(padded)
    pad_start = pad_end - padded
    dest1 = pad_start[idx[0]] + idx[2]
    dest2 = pad_start[idx[1]] + idx[3]
    n_blocks = 2 * m // MOE_BLOCK + N_EXPERTS
    blk_expert = jnp.minimum(
        jnp.searchsorted(pad_end, jnp.arange(n_blocks, dtype=jnp.int32) * MOE_BLOCK, side='right'),
        N_EXPERTS - 1).astype(jnp.int32)
    n_used = (pad_end[-1:] // MOE_BLOCK).astype(jnp.int32)
    xs = moe_scatter(h_slab, dest1, dest2, n_blocks * MOE_BLOCK)
    yb = moe_experts(xs, blk_expert, n_used, w_gate_up, w_down)
    return moe_combine(x2d, mod, yb, dest1, dest2, wgt.T, final_norm_w, seq, final)


def _layer_params(l, p):
    w_in = p['w_in'][l]
    w_gate = jnp.zeros((D_MODEL, LANES), F32).at[:, :N_GATE_COLS].set(w_in[:, PROJ_MAIN:])
    w_router_t = jnp.zeros((N_EXPERTS + SUBLANES, D_MODEL), F32)
    w_router_t = w_router_t.at[:N_EXPERTS].set(p['moe_w_expert'][l].T)
    w_router_t = w_router_t.at[N_EXPERTS:N_EXPERTS + MOE_GROUPS].set(p['moe_w_group'][l].T)
    b_router = jnp.zeros((N_EXPERTS + SUBLANES, 1), F32)
    b_router = b_router.at[:N_EXPERTS, 0].set(p['moe_b_expert'][l])
    b_router = b_router.at[N_EXPERTS:N_EXPERTS + MOE_GROUPS, 0].set(p['moe_b_group'][l])
    s5 = [s5_discretise(p['s5_lam_re'][l, d], p['s5_lam_im'][l, d], p['s5_log_step'][l, d],
                        p['s5_b_re'][l, d], p['s5_b_im'][l, d], p['s5_c_re'][l, d], p['s5_c_im'][l, d])
          for d in range(2)]
    return dict(
        norm_mix=p['norm_mix'][l][None, :], norm_ffn=p['norm_ffn'][l][None, :],
        w_main=w_in[:, :PROJ_MAIN].astype(BF16), w_gate=w_gate.astype(BF16),
        w_out=p['w_out'][l].astype(BF16), s5=s5,
        s5_d=p['s5_d'][l][None, :], s5_w_glu=p['s5_w_glu'][l].astype(BF16), s5_b_glu=p['s5_b_glu'][l][None, :],
        conv_w=p['conv_w'][l], gdn_conv_w=p['gdn_conv_w'][l], gdn_a_log=p['gdn_a_log'][l],
        gdn_dt_bias=p['gdn_dt_bias'][l], gdn_norm_w=p['gdn_norm_w'][l][None, :],
        w_router_t=w_router_t, b_router=b_router,
        w_gate_up=p['moe_w_gate_up'][l], w_down=p['moe_w_down'][l])


def _trunk(x, mods, layers, final_norm_w):
    nb, seq, d = x.shape
    m = nb * seq
    x2d = x.reshape(m, d)
    for l, lp in enumerate(layers):
        mod = mods[l].reshape(nb, 6, d)
        proj, gate_pre = in_projection(x2d, mod, lp['norm_mix'], lp['w_main'], lp['w_gate'], seq)
        u_t = proj[:, :S5_WIDTH].reshape(nb, seq, S5_WIDTH).transpose(1, 0, 2)
        (a_f, bm_f, cm_f), (a_b, bm_b, cm_b) = lp['s5']
        y_s5 = s5_mixer(u_t, a_f, bm_f, cm_f, a_b, bm_b, cm_b, lp['s5_d'], lp['s5_w_glu'], lp['s5_b_glu'])
        y_s5 = y_s5.transpose(1, 0, 2).reshape(m, S5_WIDTH)
        qkv = gdn_prepare(proj, lp['gdn_conv_w'], seq).reshape(nb, seq, 3 * GDN_WIDTH)
        pre_t = gate_pre[:, :N_GATE_COLS].reshape(nb, seq, N_GATE_COLS).transpose(0, 2, 1)
        gates = gdn_gates(pre_t, lp['gdn_a_log'], lp['gdn_dt_bias'])
        gates_row = gates.reshape(nb, 4, GDN_HEADS, seq).transpose(0, 2, 1, 3)
        gates_col = gates_row.transpose(0, 1, 3, 2)
        o_f, o_b = gdn_scan(qkv, gates_col, gates_row)
        x2d = out_projection(x2d, mod, y_s5, proj, lp['conv_w'], o_f.reshape(m, GDN_WIDTH),
                             o_b.reshape(m, GDN_WIDTH), lp['gdn_norm_w'], lp['w_out'], seq)
        x2d = hier_moe_layer(x2d, mod, lp['norm_ffn'], lp['w_router_t'], lp['b_router'],
                             lp['w_gate_up'], lp['w_down'], final_norm_w, seq, final=(l == len(layers) - 1))
    return x2d.reshape(nb, seq, d)


def kernel(x_prompt, x_sample, c_prompt, c_sample, w_ada, b_ada, norm_mix, norm_ffn, w_in, w_out, s5_lam_re, s5_lam_im, s5_log_step, s5_b_re, s5_b_im, s5_c_re, s5_c_im, s5_d, s5_w_glu, s5_b_glu, conv_w, gdn_conv_w, gdn_a_log, gdn_dt_bias, gdn_norm_w, moe_w_group, moe_b_group, moe_w_expert, moe_b_expert, moe_w_gate_up, moe_w_down, final_norm):
    p = dict(norm_mix=norm_mix, norm_ffn=norm_ffn, w_in=w_in, w_out=w_out, s5_lam_re=s5_lam_re,
             s5_lam_im=s5_lam_im, s5_log_step=s5_log_step, s5_b_re=s5_b_re, s5_b_im=s5_b_im, s5_c_re=s5_c_re,
             s5_c_im=s5_c_im, s5_d=s5_d, s5_w_glu=s5_w_glu, s5_b_glu=s5_b_glu, conv_w=conv_w,
             gdn_conv_w=gdn_conv_w, gdn_a_log=gdn_a_log, gdn_dt_bias=gdn_dt_bias, gdn_norm_w=gdn_norm_w,
             moe_w_group=moe_w_group, moe_b_group=moe_b_group, moe_w_expert=moe_w_expert,
             moe_b_expert=moe_b_expert, moe_w_gate_up=moe_w_gate_up, moe_w_down=moe_w_down)
    depth = w_ada.shape[0]
    layers = [_layer_params(l, p) for l in range(depth)]
    nbp = c_prompt.shape[0]
    mods = ada_modulation(jnp.concatenate([c_prompt, c_sample], axis=0), w_ada, b_ada)
    fnw = final_norm[None, :]
    y_prompt = _trunk(x_prompt, mods[:, :nbp], layers, fnw)
    y_sample = _trunk(x_sample, mods[:, nbp:], layers, fnw)
    return (y_prompt, y_sample)
```

```python
import functools

import jax
import jax.numpy as jnp
from jax import lax
from jax.experimental import pallas as pl
from jax.experimental.pallas import tpu as pltpu

F32 = jnp.float32
BF16 = jnp.bfloat16
HIGHEST = lax.Precision.HIGHEST

D_MODEL = 1024
DEPTH = 4
S5_WIDTH = 256
S5_GROUP = 16
S5_GROUPS = 16
S5_STATE = 64
S5_LANES = S5_GROUPS * S5_STATE
CONV_WIDTH = 256
GDN_WIDTH = 512
GDN_HEADS = 4
GDN_HEAD_DIM = 128
GDN_CHUNK = 64
PROJ_MAIN = S5_WIDTH + 3 * CONV_WIDTH + 4 * GDN_WIDTH
N_GATE_COLS = 4 * GDN_HEADS
MOE_GROUPS = 4
EXPERTS_PER_GROUP = 8
N_EXPERTS = 32
EXPERT_FF = 512
MOE_BLOCK = 512
EPS = 1e-6

SUBLANES = 8
LANES = 128
SLAB = (SUBLANES, LANES)
VMEM_LIMIT = 56 * 1024 * 1024

ROW_TILE = 512
S5_TIME_TILE = 64
GDN_STATE_CHUNKS = 8
ROUTE_TILE = 512
COMBINE_TILE = 256


def _cparams(*sem):
    return pltpu.CompilerParams(dimension_semantics=sem, vmem_limit_bytes=VMEM_LIMIT)


def _dot(a, b, **kw):
    return jnp.dot(a, b, preferred_element_type=F32, **kw)


def _dot_nt(a, b, **kw):
    return lax.dot_general(a, b, (((1,), (1,)), ((), ())), preferred_element_type=F32, **kw)


def _dot_tn(a, b, **kw):
    return lax.dot_general(a, b, (((0,), (0,)), ((), ())), preferred_element_type=F32, **kw)


def _silu(x):
    return x * jax.nn.sigmoid(x)


def _gelu_tanh(x):
    return 0.5 * x * (1.0 + jnp.tanh(0.7978845608028654 * (x + 0.044715 * (x * x * x))))


def _softplus(x):
    return jnp.maximum(x, 0.0) + jnp.log(1.0 + jnp.exp(-jnp.abs(x)))


def _rms(x):
    return x * lax.rsqrt(jnp.mean(x * x, axis=-1, keepdims=True) + EPS)


def _ada_kernel(c_ref, w_ref, b_ref, o_ref):
    o_ref[0] = _dot(_silu(c_ref[...]), w_ref[0], precision=HIGHEST) + b_ref[0]


def ada_modulation(c_all, w_ada, b_ada):
    nb = c_all.shape[0]
    depth, d, n = w_ada.shape
    tn = 1536
    return pl.pallas_call(
        _ada_kernel,
        grid=(depth, n // tn),
        in_specs=[pl.BlockSpec((nb, d), lambda l, j: (0, 0)),
                  pl.BlockSpec((1, d, tn), lambda l, j: (l, 0, j)),
                  pl.BlockSpec((1, 1, tn), lambda l, j: (l, 0, j))],
        out_specs=pl.BlockSpec((1, nb, tn), lambda l, j: (l, 0, j)),
        out_shape=jax.ShapeDtypeStruct((depth, nb, n), F32),
        compiler_params=_cparams("parallel", "parallel"),
        name="ada_modulation",
    )(c_all, w_ada, b_ada.reshape(depth, 1, n))


def _in_proj_kernel(x_ref, mod_ref, nw_ref, w_ref, wg_ref, proj_ref, gate_ref):
    h = _rms(x_ref[...]) * nw_ref[...] * (1.0 + mod_ref[0, 1:2, :]) + mod_ref[0, 0:1, :]
    hb = h.astype(BF16)
    proj_ref[...] = _dot(hb, w_ref[...])
    gate_ref[...] = _dot(hb, wg_ref[...])


def in_projection(x2d, mod, norm_w, w_main, w_gate, seq):
    m, d = x2d.shape
    tm = ROW_TILE
    per_seq = seq // tm
    return pl.pallas_call(
        _in_proj_kernel,
        grid=(m // tm,),
        in_specs=[pl.BlockSpec((tm, d), lambda i: (i, 0)),
                  pl.BlockSpec((1, 6, d), lambda i: (i // per_seq, 0, 0)),
                  pl.BlockSpec((1, d), lambda i: (0, 0)),
                  pl.BlockSpec((d, PROJ_MAIN), lambda i: (0, 0)),
                  pl.BlockSpec((d, LANES), lambda i: (0, 0))],
        out_specs=[pl.BlockSpec((tm, PROJ_MAIN), lambda i: (i, 0)),
                   pl.BlockSpec((tm, LANES), lambda i: (i, 0))],
        out_shape=[jax.ShapeDtypeStruct((m, PROJ_MAIN), F32),
                   jax.ShapeDtypeStruct((m, LANES), F32)],
        compiler_params=_cparams("parallel"),
        name="in_projection",
    )(x2d, mod, norm_w, w_main, w_gate)


def _s5_scan(x_ref, h0_re, h0_im, a_re, a_im, steps, reverse):
    def body(s, carry):
        hr, hi = carry
        t = (steps - 1 - s) if reverse else s
        rows = pl.ds(pl.multiple_of(t * SUBLANES, SUBLANES), SUBLANES)
        xr = x_ref[rows, 0:S5_LANES]
        xi = x_ref[rows, S5_LANES:2 * S5_LANES]
        nr = a_re * hr - a_im * hi + xr
        ni = a_re * hi + a_im * hr + xi
        x_ref[rows, 0:S5_LANES] = nr
        x_ref[rows, S5_LANES:2 * S5_LANES] = ni
        return nr, ni
    return lax.fori_loop(0, steps, body, (h0_re, h0_im), unroll=4)


def _s5_direction(u_ref, a_ref, bmat_ref, cmat_ref, x_ref, st_ref, reverse):
    tc, nb, w = u_ref.shape
    first = pl.program_id(0) == 0

    @pl.when(first)
    def _():
        st_ref[...] = jnp.zeros_like(st_ref)

    u = u_ref[...].reshape(tc * nb, w)
    x_ref[...] = _dot(u.astype(BF16), bmat_ref[...])
    a_re = jnp.broadcast_to(a_ref[0:1, :], (nb, S5_LANES))
    a_im = jnp.broadcast_to(a_ref[1:2, :], (nb, S5_LANES))
    hr, hi = _s5_scan(x_ref, st_ref[0], st_ref[1], a_re, a_im, tc, reverse)
    st_ref[0] = hr
    st_ref[1] = hi
    return u, _dot(x_ref[...].astype(BF16), cmat_ref[...])


def _s5_fwd_kernel(u_ref, a_ref, bmat_ref, cmat_ref, y_ref, x_ref, st_ref):
    _, y = _s5_direction(u_ref, a_ref, bmat_ref, cmat_ref, x_ref, st_ref, reverse=False)
    y_ref[...] = y.reshape(y_ref.shape)


def _s5_bwd_kernel(u_ref, yf_ref, a_ref, bmat_ref, cmat_ref, d_ref, wglu_ref, bglu_ref, o_ref, x_ref, st_ref):
    u, y = _s5_direction(u_ref, a_ref, bmat_ref, cmat_ref, x_ref, st_ref, reverse=True)
    y = y + u * d_ref[...] + yf_ref[...].reshape(u.shape)
    y = _gelu_tanh(y)
    out = y * jax.nn.sigmoid(_dot(y.astype(BF16), wglu_ref[...]) + bglu_ref[...])
    o_ref[...] = out.reshape(o_ref.shape)


def s5_mixer(u_t, a_f, bmat_f, cmat_f, a_b, bmat_b, cmat_b, d_skip, w_glu, b_glu):
    seq, nb, w = u_t.shape
    assert nb == SUBLANES, "the scan keeps one batch row per sublane"
    tc = S5_TIME_TILE
    n = seq // tc
    rows = tc * nb
    const2 = lambda i: (0, 0)
    par_specs = [pl.BlockSpec((2, S5_LANES), const2),
                 pl.BlockSpec((w, 2 * S5_LANES), const2),
                 pl.BlockSpec((2 * S5_LANES, w), const2)]
    scratch = [pltpu.VMEM((rows, 2 * S5_LANES), F32), pltpu.VMEM((2, nb, S5_LANES), F32)]
    y_f = pl.pallas_call(
        _s5_fwd_kernel,
        grid=(n,),
        in_specs=[pl.BlockSpec((tc, nb, w), lambda i: (i, 0, 0))] + par_specs,
        out_specs=pl.BlockSpec((tc, nb, w), lambda i: (i, 0, 0)),
        out_shape=jax.ShapeDtypeStruct((seq, nb, w), F32),
        scratch_shapes=scratch,
        compiler_params=_cparams("arbitrary"),
        name="s5_forward",
    )(u_t, a_f, bmat_f, cmat_f)
    rev = lambda i: (n - 1 - i, 0, 0)
    return pl.pallas_call(
        _s5_bwd_kernel,
        grid=(n,),
        in_specs=[pl.BlockSpec((tc, nb, w), rev), pl.BlockSpec((tc, nb, w), rev)] + par_specs
                 + [pl.BlockSpec((1, w), const2), pl.BlockSpec((w, w), const2), pl.BlockSpec((1, w), const2)],
        out_specs=pl.BlockSpec((tc, nb, w), rev),
        out_shape=jax.ShapeDtypeStruct((seq, nb, w), F32),
        scratch_shapes=scratch,
        compiler_params=_cparams("arbitrary"),
        name="s5_backward_glu",
    )(u_t, y_f, a_b, bmat_b, cmat_b, d_skip, w_glu, b_glu)


def s5_discretise(lam_re, lam_im, log_step, b_re, b_im, c_re, c_im):
    g, p, h = S5_GROUPS, S5_STATE, S5_GROUP
    lr, li = lam_re.astype(F32), lam_im.astype(F32)
    dt = jnp.exp(log_step.astype(F32))[:, None]
    mag = jnp.exp(lr * dt)
    abr, abi = mag * jnp.cos(li * dt), mag * jnp.sin(li * dt)
    den = lr * lr + li * li
    cr = ((abr - 1.0) * lr + abi * li) / den
    ci = (abi * lr - (abr - 1.0) * li) / den
    bbr = cr[..., None] * b_re - ci[..., None] * b_im
    bbi = cr[..., None] * b_im + ci[..., None] * b_re
    eye = jnp.eye(g, dtype=F32)
    to_b = lambda t: jnp.einsum('gph,gk->ghkp', t, eye).reshape(g * h, g * p)
    bmat = jnp.concatenate([to_b(bbr), to_b(bbi)], axis=1)
    to_c = lambda t: jnp.einsum('ghp,gk->gpkh', t, eye).reshape(g * p, g * h)
    cmat = jnp.concatenate([to_c(c_re.astype(F32)), -to_c(c_im.astype(F32))], axis=0)
    a = jnp.stack([abr.reshape(g * p), abi.reshape(g * p)])
    return a, bmat.astype(BF16), cmat.astype(BF16)


def _conv3_rows(x, prev_row, next_row, w_ref):
    tm = x.shape[0]
    rows = lax.broadcasted_iota(jnp.int32, x.shape, 0)
    xp = jnp.where(rows == 0, prev_row, pltpu.roll(x, 1, 0))
    xn = jnp.where(rows == tm - 1, next_row, pltpu.roll(x, tm - 1, 0))
    return xp * w_ref[0:1, :] + x * w_ref[1:2, :] + xn * w_ref[2:3, :]


def _halo_specs(tm, width, col_block, n_rows):
    per = tm // SUBLANES
    last = n_rows // SUBLANES - 1
    prev = pl.BlockSpec((SUBLANES, width), lambda i, *_: (jnp.maximum(i * per - 1, 0), col_block(*_)))
    nxt = pl.BlockSpec((SUBLANES, width), lambda i, *_: (jnp.minimum((i + 1) * per, last), col_block(*_)))
    return prev, nxt


def _gdn_prep_kernel(x_ref, xp_ref, xn_ref, w_ref, o_ref, *, seq):
    tm = x_ref.shape[0]
    j = pl.program_id(1)
    t0 = (pl.program_id(0) * tm) % seq
    prev_row = jnp.where(t0 == 0, 0.0, xp_ref[SUBLANES - 1:SUBLANES, :])
    next_row = jnp.where(t0 + tm == seq, 0.0, xn_ref[0:1, :])
    y = _silu(_conv3_rows(x_ref[...], prev_row, next_row, w_ref))

    @pl.when(j == 2)
    def _():
        o_ref[...] = y

    @pl.when(j < 2)
    def _():
        scale = jnp.where(j == 0, GDN_HEAD_DIM ** -0.5, 1.0)
        for h in range(GDN_HEADS):
            cols = slice(h * GDN_HEAD_DIM, (h + 1) * GDN_HEAD_DIM)
            yh = y[:, cols]
            o_ref[:, cols] = yh * (lax.rsqrt(jnp.sum(yh * yh, axis=-1, keepdims=True) + EPS) * scale)


def gdn_prepare(proj, conv_w, seq):
    m = proj.shape[0]
    tm = ROW_TILE
    first = (S5_WIDTH + 3 * CONV_WIDTH) // GDN_WIDTH
    prev, nxt = _halo_specs(tm, GDN_WIDTH, lambda j: first + j, m)
    return pl.pallas_call(
        functools.partial(_gdn_prep_kernel, seq=seq),
        grid=(m // tm, 3),
        in_specs=[pl.BlockSpec((tm, GDN_WIDTH), lambda i, j: (i, first + j)), prev, nxt,
                  pl.BlockSpec((3, GDN_WIDTH), lambda i, j: (0, j))],
        out_specs=pl.BlockSpec((tm, GDN_WIDTH), lambda i, j: (i, j)),
        out_shape=jax.ShapeDtypeStruct((m, 3 * GDN_WIDTH), F32),
        compiler_params=_cparams("parallel", "parallel"),
        name="gdn_prepare",
    )(proj, proj, proj, conv_w)


def _gdn_gates_kernel(pre_ref, alog_ref, dtb_ref, o_ref):
    seq = pre_ref.shape[2]
    nh2 = 2 * GDN_HEADS
    g = -jnp.exp(alog_ref[...]) * _softplus(pre_ref[0, 0:nh2, :] + dtb_ref[...])
    o_ref[0, nh2:2 * nh2, :] = jax.nn.sigmoid(pre_ref[0, nh2:2 * nh2, :])
    s = lax.broadcasted_iota(jnp.int32, (LANES, LANES), 0)
    t = lax.broadcasted_iota(jnp.int32, (LANES, LANES), 1)
    same = (s // GDN_CHUNK) == (t // GDN_CHUNK)
    m_f = jnp.where(same & (s <= t), 1.0, 0.0)
    m_b = jnp.where(same & (s >= t), 1.0, 0.0)
    for jb in range(seq // LANES):
        cols = slice(jb * LANES, (jb + 1) * LANES)
        gb = g[:, cols]
        o_ref[0, 0:GDN_HEADS, cols] = _dot(gb, m_f, precision=HIGHEST)[0:GDN_HEADS]
        o_ref[0, GDN_HEADS:nh2, cols] = _dot(gb, m_b, precision=HIGHEST)[GDN_HEADS:nh2]


def gdn_gates(pre_t, a_log, dt_bias):
    nb, rows, seq = pre_t.shape
    col = lambda t: t.astype(F32).reshape(2 * GDN_HEADS, 1)
    return pl.pallas_call(
        _gdn_gates_kernel,
        grid=(nb,),
        in_specs=[pl.BlockSpec((1, rows, seq), lambda b: (b, 0, 0)),
                  pl.BlockSpec((2 * GDN_HEADS, 1), lambda b: (0, 0)),
                  pl.BlockSpec((2 * GDN_HEADS, 1), lambda b: (0, 0))],
        out_specs=pl.BlockSpec((1, rows, seq), lambda b: (b, 0, 0)),
        out_shape=jax.ShapeDtypeStruct((nb, rows, seq), F32),
        compiler_params=_cparams("parallel"),
        name="gdn_gates",
    )(pre_t, col(a_log), col(dt_bias))


def _nilpotent_inverses(mats, nilpotency):
    n = mats[0].shape[0]
    eye = jnp.where(lax.broadcasted_iota(jnp.int32, (n, n), 0) == lax.broadcasted_iota(jnp.int32, (n, n), 1), 1.0, 0.0)
    ps = [eye - a for a in mats]
    a16 = [a.astype(BF16) for a in mats]
    sqs = [_dot(a, a) for a in a16]
    k = 2
    while True:
        sq16 = [s.astype(BF16) for s in sqs]
        ps = [p + _dot(p.astype(BF16), s) for p, s in zip(ps, sq16)]
        k *= 2
        if k >= nilpotency:
            return ps
        sqs = [_dot(s, s) for s in sq16]


def _gdn_local_kernel(q_ref, k_ref, v_ref, gcol_ref, grow_ref, uwqf_ref, uwqb_ref, ktf_ref, ktb_ref, in_ref, eg_ref):
    r = q_ref.shape[1]
    c = GDN_CHUNK
    hd = GDN_HEAD_DIM
    ii = lax.broadcasted_iota(jnp.int32, (r, r), 0)
    jj = lax.broadcasted_iota(jnp.int32, (r, r), 1)
    same = (ii >= c) == (jj >= c)
    incl = (same & (ii >= jj), same & (ii <= jj))
    strict = (same & (ii > jj), same & (ii < jj))
    first_chunk = lax.broadcasted_iota(jnp.int32, (r, 1), 0) < c
    zeros = jnp.zeros((r, r), F32)
    big_a, rhs, keep = [], [], []
    for h in range(GDN_HEADS):
        cols = slice(h * hd, (h + 1) * hd)
        q, k, v = q_ref[0, :, cols], k_ref[0, :, cols], v_ref[0, :, cols]
        k16 = k.astype(BF16)
        kk = _dot_nt(k16, k16)
        qk = _dot_nt(q.astype(BF16), k16)
        a_dir, rhs_dir = [], []
        for d in range(2):
            lane = d * GDN_HEADS + h
            gc_col = gcol_ref[0, :, lane:lane + 1]
            gc_row = grow_ref[0, lane:lane + 1, :]
            beta = gcol_ref[0, :, 2 * GDN_HEADS + lane:2 * GDN_HEADS + lane + 1]
            decay = jnp.exp(jnp.where(incl[d], gc_col - gc_row, -1e30))
            a_dir.append(jnp.where(strict[d], kk * beta * decay, 0.0))
            intra = jnp.where(incl[d], qk * decay, 0.0)
            ends = (gc_col[c - 1:c], gc_col[2 * c - 1:2 * c]) if d == 0 else (gc_col[0:1], gc_col[c:c + 1])
            g_last = jnp.where(first_chunk, ends[0], ends[1])
            eg = jnp.exp(gc_col)
            kb = k * beta
            rhs_dir.append(jnp.concatenate([v * beta, kb * eg], axis=1))
            keep.append((q * eg, (k * jnp.exp(g_last - gc_col)).T, intra[:, :c] + intra[:, c:]))
            for ch in range(2):
                eg_ref[0, ch, lane:lane + 1, :] = jnp.broadcast_to(jnp.exp(ends[ch]), (1, LANES))
        big_a.append(jnp.concatenate([jnp.concatenate([a_dir[0], zeros], axis=1),
                                      jnp.concatenate([zeros, a_dir[1]], axis=1)], axis=0))
        rhs.append(jnp.concatenate(rhs_dir, axis=0))
    t_inv = _nilpotent_inverses(big_a, c)
    for h in range(GDN_HEADS):
        uw = _dot(t_inv[h].astype(BF16), rhs[h].astype(BF16))
        (qd_f, kt_f, in_f), (qd_b, kt_b, in_b) = keep[2 * h], keep[2 * h + 1]
        uwqf_ref[0, h] = jnp.concatenate([uw[:r], qd_f], axis=1).astype(BF16)
        uwqb_ref[0, h] = jnp.concatenate([uw[r:], qd_b], axis=1).astype(BF16)
        ktf_ref[0, h] = kt_f.astype(BF16)
        ktb_ref[0, h] = kt_b.astype(BF16)
        in_ref[0, h] = jnp.concatenate([in_f, in_b], axis=1).astype(BF16)


def gdn_local(qkv, gates_col, gates_row):
    nb, seq, _ = qkv.shape
    r = 2 * GDN_CHUNK
    hd, nh = GDN_HEAD_DIM, GDN_HEADS
    uwq_sd = jax.ShapeDtypeStruct((nb, nh, seq, 3 * hd), BF16)
    kt_sd = jax.ShapeDtypeStruct((nb, nh, hd, seq), BF16)
    uwq_spec = pl.BlockSpec((1, nh, r, 3 * hd), lambda b, i: (b, 0, i, 0))
    kt_spec = pl.BlockSpec((1, nh, hd, r), lambda b, i: (b, 0, 0, i))
    return pl.pallas_call(
        _gdn_local_kernel,
        grid=(nb, seq // r),
        in_specs=[pl.BlockSpec((1, r, GDN_WIDTH), lambda b, i: (b, i, 0)),
                  pl.BlockSpec((1, r, GDN_WIDTH), lambda b, i: (b, i, 1)),
                  pl.BlockSpec((1, r, GDN_WIDTH), lambda b, i: (b, i, 2)),
                  pl.BlockSpec((1, r, 4 * nh), lambda b, i: (b, i, 0)),
                  pl.BlockSpec((1, 4 * nh, r), lambda b, i: (b, 0, i))],
        out_specs=[uwq_spec, uwq_spec, kt_spec, kt_spec,
                   pl.BlockSpec((1, nh, r, 2 * GDN_CHUNK), lambda b, i: (b, 0, i, 0)),
                   pl.BlockSpec((1, 2, SUBLANES, LANES), lambda b, i: (b, i, 0, 0))],
        out_shape=[uwq_sd, uwq_sd, kt_sd, kt_sd,
                   jax.ShapeDtypeStruct((nb, nh, seq, 2 * GDN_CHUNK), BF16),
                   jax.ShapeDtypeStruct((nb, seq // GDN_CHUNK, SUBLANES, LANES), F32)],
        compiler_params=_cparams("parallel", "parallel"),
        name="gdn_local",
    )(qkv, qkv, qkv, gates_col, gates_row)


def _gdn_state_kernel(uwqf, ktf, inf, egf, uwqb, ktb, inb, egb, of_ref, ob_ref, st_ref):
    @pl.when(pl.program_id(1) == 0)
    def _():
        st_ref[...] = jnp.zeros_like(st_ref)

    c = GDN_CHUNK
    hd = GDN_HEAD_DIM
    nc = uwqf.shape[2] // c
    chains = [(d, h) for d in range(2) for h in range(GDN_HEADS)]
    refs = ((uwqf, ktf, inf, egf, of_ref), (uwqb, ktb, inb, egb, ob_ref))
    states = [st_ref[j] for j in range(len(chains))]
    for step in range(nc):
        sws = []
        for j, (d, h) in enumerate(chains):
            ci = step if d == 0 else nc - 1 - step
            rows = slice(ci * c, (ci + 1) * c)
            uwq = refs[d][0]
            wq = jnp.concatenate([uwq[0, h, rows, hd:2 * hd], uwq[0, h, rows, 2 * hd:3 * hd]], axis=0)
            sws.append(_dot(wq, states[j].astype(BF16)))
        for j, (d, h) in enumerate(chains):
            ci = step if d == 0 else nc - 1 - step
            rows = slice(ci * c, (ci + 1) * c)
            uwq, kt, intra, eg, o_ref = refs[d]
            sw = sws[j]
            v16 = (uwq[0, h, rows, 0:hd].astype(F32) - sw[:c]).astype(BF16)
            o_ref[0, rows, h * hd:(h + 1) * hd] = sw[c:] + _dot(intra[0, h, rows, d * c:(d + 1) * c], v16)
            lane = d * GDN_HEADS + h
            states[j] = states[j] * eg[0, ci, lane:lane + 1, :] + _dot(kt[0, h, :, rows], v16)
    for j in range(len(chains)):
        st_ref[j] = states[j]


def gdn_state_scan(uwq_f, uwq_b, kt_f, kt_b, intra, eg):
    nb, nh, seq, _ = uwq_f.shape
    hd = GDN_HEAD_DIM
    nc = GDN_STATE_CHUNKS
    r = nc * GDN_CHUNK
    n = seq // r

    def specs(pos):
        return [pl.BlockSpec((1, nh, r, 3 * hd), lambda b, i: (b, 0, pos(i), 0)),
                pl.BlockSpec((1, nh, hd, r), lambda b, i: (b, 0, 0, pos(i))),
                pl.BlockSpec((1, nh, r, 2 * GDN_CHUNK), lambda b, i: (b, 0, pos(i), 0)),
                pl.BlockSpec((1, nc, SUBLANES, LANES), lambda b, i: (b, pos(i), 0, 0))]

    fwd = lambda i: i
    bwd = lambda i: n - 1 - i
    out_sd = jax.ShapeDtypeStruct((nb, seq, GDN_WIDTH), F32)
    return pl.pallas_call(
        _gdn_state_kernel,
        grid=(nb, n),
        in_specs=specs(fwd) + specs(bwd),
        out_specs=[pl.BlockSpec((1, r, GDN_WIDTH), lambda b, i: (b, i, 0)),
                   pl.BlockSpec((1, r, GDN_WIDTH), lambda b, i: (b, n - 1 - i, 0))],
        out_shape=[out_sd, out_sd],
        scratch_shapes=[pltpu.VMEM((2 * nh, hd, hd), F32)],
        compiler_params=_cparams("parallel", "arbitrary"),
        name="gdn_state_scan",
    )(uwq_f, kt_f, intra, eg, uwq_b, kt_b, intra, eg)


def _out_proj_kernel(x_ref, mod_ref, ys5_ref, cx_ref, cb_ref, cc_ref, cxp_ref, cxn_ref, ccp_ref, ccn_ref,
                     cw_ref, of_ref, ob_ref, z_ref, gnw_ref, w_ref, o_ref, *, seq):
    tm = x_ref.shape[0]
    t0 = (pl.program_id(0) * tm) % seq
    last = SUBLANES - 1
    prev_row = jnp.where(t0 == 0, 0.0, ccp_ref[last:SUBLANES, :] * cxp_ref[last:SUBLANES, :])
    next_row = jnp.where(t0 + tm == seq, 0.0, ccn_ref[0:1, :] * cxn_ref[0:1, :])
    y_conv = cb_ref[...] * _conv3_rows(cc_ref[...] * cx_ref[...], prev_row, next_row, cw_ref)
    o = of_ref[...] + ob_ref[...]
    z = z_ref[...]
    heads = []
    for h in range(GDN_HEADS):
        cols = slice(h * GDN_HEAD_DIM, (h + 1) * GDN_HEAD_DIM)
        heads.append(_rms(o[:, cols]) * gnw_ref[...] * _silu(z[:, cols]))
    y_gdn = jnp.concatenate(heads, axis=1)
    c0, c1 = S5_WIDTH, S5_WIDTH + CONV_WIDTH
    mix = (_dot(ys5_ref[...].astype(BF16), w_ref[0:c0, :])
           + _dot(y_conv.astype(BF16), w_ref[c0:c1, :])
           + _dot(y_gdn.astype(BF16), w_ref[c1:, :]))
    o_ref[...] = x_ref[...] + mod_ref[0, 2:3, :] * mix


def out_projection(x2d, mod, y_s5, proj, conv_w, o_f, o_b, gdn_norm_w, w_out, seq):
    m, d = x2d.shape
    tm = ROW_TILE
    per_seq = seq // tm
    cw = CONV_WIDTH
    row = lambda c: pl.BlockSpec((tm, cw), lambda i: (i, c))
    cxp, cxn = _halo_specs(tm, cw, lambda: 1, m)
    ccp, ccn = _halo_specs(tm, cw, lambda: 3, m)
    gw = GDN_WIDTH
    return pl.pallas_call(
        functools.partial(_out_proj_kernel, seq=seq),
        grid=(m // tm,),
        in_specs=[pl.BlockSpec((tm, d), lambda i: (i, 0)),
                  pl.BlockSpec((1, 6, d), lambda i: (i // per_seq, 0, 0)),
                  pl.BlockSpec((tm, cw), lambda i: (i, 0)),
                  row(1), row(2), row(3), cxp, cxn, ccp, ccn,
                  pl.BlockSpec((3, cw), lambda i: (0, 0)),
                  pl.BlockSpec((tm, gw), lambda i: (i, 0)),
                  pl.BlockSpec((tm, gw), lambda i: (i, 0)),
                  pl.BlockSpec((tm, gw), lambda i: (i, (PROJ_MAIN - gw) // gw)),
                  pl.BlockSpec((1, GDN_HEAD_DIM), lambda i: (0, 0)),
                  pl.BlockSpec((d, d), lambda i: (0, 0))],
        out_specs=pl.BlockSpec((tm, d), lambda i: (i, 0)),
        out_shape=jax.ShapeDtypeStruct((m, d), F32),
        compiler_params=_cparams("parallel"),
        name="out_projection",
    )(x2d, mod, y_s5, proj, proj, proj, proj, proj, proj, proj, conv_w, o_f, o_b, proj, gdn_norm_w, w_out)


def _first_argmax(vals, rows, n):
    m = jnp.max(vals, axis=0, keepdims=True)
    return m, jnp.min(jnp.where(vals == m, rows, n), axis=0, keepdims=True)


def _route_kernel(x_ref, mod_ref, nw_ref, wr_ref, br_ref, h_ref, idx_ref, wgt_ref, cnt_ref, carry_ref):
    step = pl.program_id(0)

    @pl.when(step == 0)
    def _():
        carry_ref[...] = jnp.zeros_like(carry_ref)

    tm = x_ref.shape[0]
    h = _rms(x_ref[...]) * nw_ref[...] * (1.0 + mod_ref[0, 4:5, :]) + mod_ref[0, 3:4, :]
    for s in range(SUBLANES):
        h_ref[:, s, :] = h[:, s * LANES:(s + 1) * LANES]
    logits = _dot_nt(wr_ref[...], h, precision=HIGHEST) + br_ref[...]
    lg = logits[N_EXPERTS:N_EXPERTS + MOE_GROUPS, :]
    rows_g = lax.broadcasted_iota(jnp.int32, lg.shape, 0)
    g_max, g_idx = _first_argmax(lg, rows_g, MOE_GROUPS)
    g_w = 1.0 / jnp.sum(jnp.exp(lg - g_max), axis=0, keepdims=True)
    le = jnp.zeros((EXPERTS_PER_GROUP, tm), F32)
    for g in range(MOE_GROUPS):
        le = jnp.where(g_idx == g, logits[g * EXPERTS_PER_GROUP:(g + 1) * EXPERTS_PER_GROUP, :], le)
    rows_e = lax.broadcasted_iota(jnp.int32, le.shape, 0)
    m1, i1 = _first_argmax(le, rows_e, EXPERTS_PER_GROUP)
    m2, i2 = _first_argmax(jnp.where(rows_e == i1, -jnp.inf, le), rows_e, EXPERTS_PER_GROUP)
    r = jnp.exp(m2 - m1)
    w1 = g_w / (1.0 + r)
    e1 = g_idx * EXPERTS_PER_GROUP + i1
    e2 = g_idx * EXPERTS_PER_GROUP + i2
    rows_x = lax.broadcasted_iota(jnp.int32, (N_EXPERTS, tm), 0)
    hit1 = rows_x == e1
    hit2 = rows_x == e2
    onehot = jnp.where(hit1 | hit2, 1.0, 0.0)
    before = lax.broadcasted_iota(jnp.int32, (tm, tm), 0) < lax.broadcasted_iota(jnp.int32, (tm, tm), 1)
    ahead = _dot(onehot.astype(BF16), jnp.where(before, 1.0, 0.0).astype(BF16)) + carry_ref[:, 0:1]
    rank1 = jnp.sum(jnp.where(hit1, ahead, 0.0), axis=0, keepdims=True)
    rank2 = jnp.sum(jnp.where(hit2, ahead, 0.0), axis=0, keepdims=True)
    idx_ref[0:1, :] = e1
    idx_ref[1:2, :] = e2
    idx_ref[2:3, :] = rank1.astype(jnp.int32)
    idx_ref[3:4, :] = rank2.astype(jnp.int32)
    wgt_ref[0:1, :] = w1
    wgt_ref[1:2, :] = w1 * r
    carry_ref[...] = carry_ref[...] + jnp.sum(onehot, axis=1, keepdims=True)
    cnt_ref[...] = carry_ref[...]


def moe_route(x2d, mod, norm_w, w_router_t, b_router, seq):
    m, d = x2d.shape
    tm = ROUTE_TILE
    per_seq = seq // tm
    nr = w_router_t.shape[0]
    return pl.pallas_call(
        _route_kernel,
        grid=(m // tm,),
        in_specs=[pl.BlockSpec((tm, d), lambda i: (i, 0)),
                  pl.BlockSpec((1, 6, d), lambda i: (i // per_seq, 0, 0)),
                  pl.BlockSpec((1, d), lambda i: (0, 0)),
                  pl.BlockSpec((nr, d), lambda i: (0, 0)),
                  pl.BlockSpec((nr, 1), lambda i: (0, 0))],
        out_specs=[pl.BlockSpec((tm,) + SLAB, lambda i: (i, 0, 0)),
                   pl.BlockSpec((4, tm), lambda i: (0, i)),
                   pl.BlockSpec((2, tm), lambda i: (0, i)),
                   pl.BlockSpec((N_EXPERTS, LANES), lambda i: (0, 0))],
        out_shape=[jax.ShapeDtypeStruct((m,) + SLAB, F32),
                   jax.ShapeDtypeStruct((4, m), jnp.int32),
                   jax.ShapeDtypeStruct((2, m), F32),
                   jax.ShapeDtypeStruct((N_EXPERTS, LANES), F32)],
        scratch_shapes=[pltpu.VMEM((N_EXPERTS, LANES), F32)],
        compiler_params=_cparams("arbitrary"),
        name="moe_route",
    )(x2d, mod, norm_w, w_router_t, b_router)


def _row_copy(src_ref, src_row, dst_ref, dst_row, sem):
    return pltpu.make_async_copy(src_ref.at[pl.ds(src_row, 1)], dst_ref.at[pl.ds(dst_row, 1)], sem)


def _scatter_kernel(d1_ref, d2_ref, h_ref, xs_in_ref, xs_ref, sem):
    del xs_in_ref
    tm = h_ref.shape[0]

    def start(r, c):
        _row_copy(h_ref, r, xs_ref, d1_ref[0, 0, r], sem).start()
        _row_copy(h_ref, r, xs_ref, d2_ref[0, 0, r], sem).start()
        return c

    def wait(r, c):
        _row_copy(h_ref, 0, xs_ref, 0, sem).wait()
        _row_copy(h_ref, 0, xs_ref, 0, sem).wait()
        return c

    lax.fori_loop(0, tm, start, 0)
    lax.fori_loop(0, tm, wait, 0)


def moe_scatter(h_slab, dest1, dest2, n_slots):
    m = h_slab.shape[0]
    tm = ROUTE_TILE
    idx_spec = pl.BlockSpec((1, 1, tm), lambda i: (i, 0, 0), memory_space=pltpu.SMEM)
    return pl.pallas_call(
        _scatter_kernel,
        grid=(m // tm,),
        in_specs=[idx_spec, idx_spec,
                  pl.BlockSpec((tm,) + SLAB, lambda i: (i, 0, 0)),
                  pl.BlockSpec(memory_space=pl.ANY)],
        out_specs=pl.BlockSpec(memory_space=pl.ANY),
        out_shape=jax.ShapeDtypeStruct((n_slots,) + SLAB, F32),
        scratch_shapes=[pltpu.SemaphoreType.DMA(())],
        input_output_aliases={3: 0},
        compiler_params=_cparams("arbitrary"),
        name="moe_scatter",
    )(dest1.reshape(m // tm, 1, tm), dest2.reshape(m // tm, 1, tm), h_slab, jnp.zeros((n_slots,) + SLAB, F32))


def _expert_kernel(be_ref, nused_ref, xs_ref, wgu_ref, wd_ref, y_ref, wgu16_ref, wd16_ref):
    i = pl.program_id(0)
    changed = jnp.logical_or(i == 0, be_ref[i] != be_ref[jnp.maximum(i - 1, 0)])

    @pl.when(changed)
    def _():
        wgu16_ref[...] = wgu_ref[0].astype(BF16)
        wd16_ref[...] = wd_ref[0].astype(BF16)

    @pl.when(i < nused_ref[0])
    def _():
        x = jnp.concatenate([xs_ref[:, s, :] for s in range(SUBLANES)], axis=1).astype(BF16)
        gu = _dot(x, wgu16_ref[...])
        act = _silu(gu[:, :EXPERT_FF]) * gu[:, EXPERT_FF:]
        y = _dot(act.astype(BF16), wd16_ref[...])
        for s in range(SUBLANES):
            y_ref[:, s, :] = y[:, s * LANES:(s + 1) * LANES]

    @pl.when(i >= nused_ref[0])
    def _():
        y_ref[...] = jnp.zeros_like(y_ref)


def moe_experts(xs, blk_expert, n_used, w_gate_up, w_down):
    n_slots = xs.shape[0]
    n_blocks = n_slots // MOE_BLOCK
    d = D_MODEL
    return pl.pallas_call(
        _expert_kernel,
        grid_spec=pltpu.PrefetchScalarGridSpec(
            num_scalar_prefetch=2,
            grid=(n_blocks,),
            in_specs=[pl.BlockSpec((MOE_BLOCK,) + SLAB, lambda i, be, nu: (i, 0, 0)),
                      pl.BlockSpec((1, d, 2 * EXPERT_FF), lambda i, be, nu: (be[i], 0, 0)),
                      pl.BlockSpec((1, EXPERT_FF, d), lambda i, be, nu: (be[i], 0, 0))],
            out_specs=pl.BlockSpec((MOE_BLOCK,) + SLAB, lambda i, be, nu: (i, 0, 0)),
            scratch_shapes=[pltpu.VMEM((d, 2 * EXPERT_FF), BF16), pltpu.VMEM((EXPERT_FF, d), BF16)]),
        out_shape=jax.ShapeDtypeStruct((n_slots,) + SLAB, F32),
        compiler_params=_cparams("arbitrary"),
        name="moe_experts",
    )(blk_expert, n_used, xs, w_gate_up, w_down)


def _combine_kernel(d1_ref, d2_ref, x_ref, mod_ref, wgt_ref, fnw_ref, yb_ref, o_ref, y1_ref, y2_ref, sem, *, final):
    tm = x_ref.shape[0]

    def start(r, c):
        _row_copy(yb_ref, d1_ref[0, 0, r], y1_ref, r, sem).start()
        _row_copy(yb_ref, d2_ref[0, 0, r], y2_ref, r, sem).start()
        return c

    def wait(r, c):
        _row_copy(yb_ref, 0, y1_ref, 0, sem).wait()
        _row_copy(yb_ref, 0, y2_ref, 0, sem).wait()
        return c

    lax.fori_loop(0, tm, start, 0)
    lax.fori_loop(0, tm, wait, 0)
    w1 = wgt_ref[:, 0:1]
    w2 = wgt_ref[:, 1:2]
    for s in range(SUBLANES):
        cols = slice(s * LANES, (s + 1) * LANES)
        y = y1_ref[:, s, :] * w1 + y2_ref[:, s, :] * w2
        o_ref[:, cols] = x_ref[:, cols] + mod_ref[0, 5:6, cols] * y
    if final:
        o_ref[...] = _rms(o_ref[...]) * fnw_ref[...]


def moe_combine(x2d, mod, yb, dest1, dest2, weights, final_norm_w, seq, final):
    m, d = x2d.shape
    tm = COMBINE_TILE
    per_seq = seq // tm
    idx_spec = pl.BlockSpec((1, 1, tm), lambda i: (i, 0, 0), memory_space=pltpu.SMEM)
    return pl.pallas_call(
        functools.partial(_combine_kernel, final=final),
        grid=(m // tm,),
        in_specs=[idx_spec, idx_spec,
                  pl.BlockSpec((tm, d), lambda i: (i, 0)),
                  pl.BlockSpec((1, 6, d), lambda i: (i // per_seq, 0, 0)),
                  pl.BlockSpec((tm, 2), lambda i: (i, 0)),
                  pl.BlockSpec((1, d), lambda i: (0, 0)),
                  pl.BlockSpec(memory_space=pl.ANY)],
        out_specs=pl.BlockSpec((tm, d), lambda i: (i, 0)),
        out_shape=jax.ShapeDtypeStruct((m, d), F32),
        scratch_shapes=[pltpu.VMEM((tm,) + SLAB, F32), pltpu.VMEM((tm,) + SLAB, F32),
                        pltpu.SemaphoreType.DMA(())],
        compiler_params=_cparams("arbitrary"),
        name="moe_combine",
    )(dest1.reshape(m // tm, 1, tm), dest2.reshape(m // tm, 1, tm), x2d, mod, weights, final_norm_w, yb)


def hier_moe_layer(x2d, mod, norm_w, w_router_t, b_router, w_gate_up, w_down, final_norm_w, seq, final):
    m = x2d.shape[0]
    h_slab, idx, wgt, counts = moe_route(x2d, mod, norm_w, w_router_t, b_router, seq)
    counts = counts[:, 0].astype(jnp.int32)
    padded = (counts + MOE_BLOCK - 1) // MOE_BLOCK * MOE_BLOCK
    pad_end = jnp.cumsum(padded)
    pad_start = pad_end - padded
    dest1 = pad_start[idx[0]] + idx[2]
    dest2 = pad_start[idx[1]] + idx[3]
    n_blocks = 2 * m // MOE_BLOCK + N_EXPERTS
    blk_expert = jnp.minimum(
        jnp.searchsorted(pad_end, jnp.arange(n_blocks, dtype=jnp.int32) * MOE_BLOCK, side='right'),
        N_EXPERTS - 1).astype(jnp.int32)
    n_used = (pad_end[-1:] // MOE_BLOCK).astype(jnp.int32)
    xs = moe_scatter(h_slab, dest1, dest2, n_blocks * MOE_BLOCK)
    yb = moe_experts(xs, blk_expert, n_used, w_gate_up, w_down)
    return moe_combine(x2d, mod, yb, dest1, dest2, wgt.T, final_norm_w, seq, final)


def _layer_params(l, p):
    w_in = p['w_in'][l]
    w_gate = jnp.zeros((D_MODEL, LANES), F32).at[:, :N_GATE_COLS].set(w_in[:, PROJ_MAIN:])
    w_router_t = jnp.zeros((N_EXPERTS + SUBLANES, D_MODEL), F32)
    w_router_t = w_router_t.at[:N_EXPERTS].set(p['moe_w_expert'][l].T)
    w_router_t = w_router_t.at[N_EXPERTS:N_EXPERTS + MOE_GROUPS].set(p['moe_w_group'][l].T)
    b_router = jnp.zeros((N_EXPERTS + SUBLANES, 1), F32)
    b_router = b_router.at[:N_EXPERTS, 0].set(p['moe_b_expert'][l])
    b_router = b_router.at[N_EXPERTS:N_EXPERTS + MOE_GROUPS, 0].set(p['moe_b_group'][l])
    s5 = [s5_discretise(p['s5_lam_re'][l, d], p['s5_lam_im'][l, d], p['s5_log_step'][l, d],
                        p['s5_b_re'][l, d], p['s5_b_im'][l, d], p['s5_c_re'][l, d], p['s5_c_im'][l, d])
          for d in range(2)]
    return dict(
        norm_mix=p['norm_mix'][l][None, :], norm_ffn=p['norm_ffn'][l][None, :],
        w_main=w_in[:, :PROJ_MAIN].astype(BF16), w_gate=w_gate.astype(BF16),
        w_out=p['w_out'][l].astype(BF16), s5=s5,
        s5_d=p['s5_d'][l][None, :], s5_w_glu=p['s5_w_glu'][l].astype(BF16), s5_b_glu=p['s5_b_glu'][l][None, :],
        conv_w=p['conv_w'][l], gdn_conv_w=p['gdn_conv_w'][l], gdn_a_log=p['gdn_a_log'][l],
        gdn_dt_bias=p['gdn_dt_bias'][l], gdn_norm_w=p['gdn_norm_w'][l][None, :],
        w_router_t=w_router_t, b_router=b_router,
        w_gate_up=p['moe_w_gate_up'][l], w_down=p['moe_w_down'][l])


def _trunk(x, mods, layers, final_norm_w):
    nb, seq, d = x.shape
    m = nb * seq
    x2d = x.reshape(m, d)
    for l, lp in enumerate(layers):
        mod = mods[l].reshape(nb, 6, d)
        proj, gate_pre = in_projection(x2d, mod, lp['norm_mix'], lp['w_main'], lp['w_gate'], seq)
        u_t = proj[:, :S5_WIDTH].reshape(nb, seq, S5_WIDTH).transpose(1, 0, 2)
        (a_f, bm_f, cm_f), (a_b, bm_b, cm_b) = lp['s5']
        y_s5 = s5_mixer(u_t, a_f, bm_f, cm_f, a_b, bm_b, cm_b, lp['s5_d'], lp['s5_w_glu'], lp['s5_b_glu'])
        y_s5 = y_s5.transpose(1, 0, 2).reshape(m, S5_WIDTH)
        qkv = gdn_prepare(proj, lp['gdn_conv_w'], seq).reshape(nb, seq, 3 * GDN_WIDTH)
        pre_t = gate_pre[:, :N_GATE_COLS].reshape(nb, seq, N_GATE_COLS).transpose(0, 2, 1)
        gates = gdn_gates(pre_t, lp['gdn_a_log'], lp['gdn_dt_bias'])
        uwq_f, uwq_b, kt_f, kt_b, intra, eg = gdn_local(qkv, gates.transpose(0, 2, 1), gates)
        o_f, o_b = gdn_state_scan(uwq_f, uwq_b, kt_f, kt_b, intra, eg)
        x2d = out_projection(x2d, mod, y_s5, proj, lp['conv_w'], o_f.reshape(m, GDN_WIDTH),
                             o_b.reshape(m, GDN_WIDTH), lp['gdn_norm_w'], lp['w_out'], seq)
        x2d = hier_moe_layer(x2d, mod, lp['norm_ffn'], lp['w_router_t'], lp['b_router'],
                             lp['w_gate_up'], lp['w_down'], final_norm_w, seq, final=(l == len(layers) - 1))
    return x2d.reshape(nb, seq, d)


def kernel(x_prompt, x_sample, c_prompt, c_sample, w_ada, b_ada, norm_mix, norm_ffn, w_in, w_out, s5_lam_re, s5_lam_im, s5_log_step, s5_b_re, s5_b_im, s5_c_re, s5_c_im, s5_d, s5_w_glu, s5_b_glu, conv_w, gdn_conv_w, gdn_a_log, gdn_dt_bias, gdn_norm_w, moe_w_group, moe_b_group, moe_w_expert, moe_b_expert, moe_w_gate_up, moe_w_down, final_norm):
    p = dict(norm_mix=norm_mix, norm_ffn=norm_ffn, w_in=w_in, w_out=w_out, s5_lam_re=s5_lam_re,
             s5_lam_im=s5_lam_im, s5_log_step=s5_log_step, s5_b_re=s5_b_re, s5_b_im=s5_b_im, s5_c_re=s5_c_re,
             s5_c_im=s5_c_im, s5_d=s5_d, s5_w_glu=s5_w_glu, s5_b_glu=s5_b_glu, conv_w=conv_w,
             gdn_conv_w=gdn_conv_w, gdn_a_log=gdn_a_log, gdn_dt_bias=gdn_dt_bias, gdn_norm_w=gdn_norm_w,
             moe_w_group=moe_w_group, moe_b_group=moe_b_group, moe_w_expert=moe_w_expert,
             moe_b_expert=moe_b_expert, moe_w_gate_up=moe_w_gate_up, moe_w_down=moe_w_down)
    depth = w_ada.shape[0]
    layers = [_layer_params(l, p) for l in range(depth)]
    nbp = c_prompt.shape[0]
    mods = ada_modulation(jnp.concatenate([c_prompt, c_sample], axis=0), w_ada, b_ada)
    fnw = final_norm[None, :]
    y_prompt = _trunk(x_prompt, mods[:, :nbp], layers, fnw)
    y_sample = _trunk(x_sample, mods[:, nbp:], layers, fnw)
    return (y_prompt, y_sample)
```

```python
import functools

import jax
import jax.numpy as jnp
from jax import lax
from jax.experimental import pallas as pl
from jax.experimental.pallas import tpu as pltpu

F32 = jnp.float32
BF16 = jnp.bfloat16
HIGHEST = lax.Precision.HIGHEST

D_MODEL = 1024
DEPTH = 4
S5_WIDTH = 256
S5_GROUP = 16
S5_GROUPS = 16
S5_STATE = 64
S5_LANES = S5_GROUPS * S5_STATE
CONV_WIDTH = 256
GDN_WIDTH = 512
GDN_HEADS = 4
GDN_HEAD_DIM = 128
GDN_CHUNK = 64
PROJ_MAIN = S5_WIDTH + 3 * CONV_WIDTH + 4 * GDN_WIDTH
N_GATE_COLS = 4 * GDN_HEADS
MOE_GROUPS = 4
EXPERTS_PER_GROUP = 8
N_EXPERTS = 32
EXPERT_FF = 512
MOE_BLOCK = 512
EPS = 1e-6

SUBLANES = 8
LANES = 128
SLAB = (SUBLANES, LANES)
VMEM_LIMIT = 56 * 1024 * 1024

ROW_TILE = 512
S5_TIME_TILE = 64
GDN_STATE_CHUNKS = 8
ROUTE_TILE = 512
COMBINE_TILE = 256
DMA_UNROLL = 8


def _cparams(*sem):
    return pltpu.CompilerParams(dimension_semantics=sem, vmem_limit_bytes=VMEM_LIMIT)


def _dot(a, b, **kw):
    return jnp.dot(a, b, preferred_element_type=F32, **kw)


def _dot_nt(a, b, **kw):
    return lax.dot_general(a, b, (((1,), (1,)), ((), ())), preferred_element_type=F32, **kw)


def _dot_tn(a, b, **kw):
    return lax.dot_general(a, b, (((0,), (0,)), ((), ())), preferred_element_type=F32, **kw)


def _silu(x):
    return x * jax.nn.sigmoid(x)


def _gelu_tanh(x):
    return 0.5 * x * (1.0 + jnp.tanh(0.7978845608028654 * (x + 0.044715 * (x * x * x))))


def _softplus(x):
    return jnp.maximum(x, 0.0) + jnp.log(1.0 + jnp.exp(-jnp.abs(x)))


def _rms(x):
    return x * lax.rsqrt(jnp.mean(x * x, axis=-1, keepdims=True) + EPS)


def _ada_kernel(c_ref, w_ref, b_ref, o_ref):
    o_ref[0] = _dot(_silu(c_ref[...]), w_ref[0], precision=HIGHEST) + b_ref[0]


def ada_modulation(c_all, w_ada, b_ada):
    nb = c_all.shape[0]
    depth, d, n = w_ada.shape
    tn = 1536
    return pl.pallas_call(
        _ada_kernel,
        grid=(depth, n // tn),
        in_specs=[pl.BlockSpec((nb, d), lambda l, j: (0, 0)),
                  pl.BlockSpec((1, d, tn), lambda l, j: (l, 0, j)),
                  pl.BlockSpec((1, 1, tn), lambda l, j: (l, 0, j))],
        out_specs=pl.BlockSpec((1, nb, tn), lambda l, j: (l, 0, j)),
        out_shape=jax.ShapeDtypeStruct((depth, nb, n), F32),
        compiler_params=_cparams("parallel", "parallel"),
        name="ada_modulation",
    )(c_all, w_ada, b_ada.reshape(depth, 1, n))


def _in_proj_kernel(x_ref, mod_ref, nw_ref, w_ref, wg_ref, proj_ref, gate_ref):
    h = _rms(x_ref[...]) * nw_ref[...] * (1.0 + mod_ref[0, 1:2, :]) + mod_ref[0, 0:1, :]
    hb = h.astype(BF16)
    proj_ref[...] = _dot(hb, w_ref[...])
    gate_ref[...] = _dot(hb, wg_ref[...])


def in_projection(x2d, mod, norm_w, w_main, w_gate, seq):
    m, d = x2d.shape
    tm = ROW_TILE
    per_seq = seq // tm
    return pl.pallas_call(
        _in_proj_kernel,
        grid=(m // tm,),
        in_specs=[pl.BlockSpec((tm, d), lambda i: (i, 0)),
                  pl.BlockSpec((1, 6, d), lambda i: (i // per_seq, 0, 0)),
                  pl.BlockSpec((1, d), lambda i: (0, 0)),
                  pl.BlockSpec((d, PROJ_MAIN), lambda i: (0, 0)),
                  pl.BlockSpec((d, LANES), lambda i: (0, 0))],
        out_specs=[pl.BlockSpec((tm, PROJ_MAIN), lambda i: (i, 0)),
                   pl.BlockSpec((tm, LANES), lambda i: (i, 0))],
        out_shape=[jax.ShapeDtypeStruct((m, PROJ_MAIN), F32),
                   jax.ShapeDtypeStruct((m, LANES), F32)],
        compiler_params=_cparams("parallel"),
        name="in_projection",
    )(x2d, mod, norm_w, w_main, w_gate)


S5_LANE_BLOCKS = S5_LANES // LANES


def _s5_scan(x_ref, h0, a, steps, reverse):
    nblk = S5_LANE_BLOCKS

    def body(s, h):
        t = (steps - 1 - s) if reverse else s
        rows = pl.ds(t, SUBLANES, stride=steps)
        new = [None] * (2 * nblk)
        for j in range(nblk):
            hr, hi = h[j], h[nblk + j]
            new[j] = a[j] * hr - a[nblk + j] * hi + x_ref[j, rows, :]
            new[nblk + j] = a[j] * hi + a[nblk + j] * hr + x_ref[nblk + j, rows, :]
            x_ref[j, rows, :] = new[j]
            x_ref[nblk + j, rows, :] = new[nblk + j]
        return tuple(new)

    return lax.fori_loop(0, steps, body, h0, unroll=4)


def _s5_direction(u_ref, a_ref, bmat_ref, cmat_ref, x_ref, st_ref, reverse):
    nb, tc, w = u_ref.shape
    nblk = 2 * S5_LANE_BLOCKS
    first = pl.program_id(0) == 0

    @pl.when(first)
    def _():
        st_ref[...] = jnp.zeros_like(st_ref)

    u = u_ref[...].reshape(nb * tc, w)
    x = _dot(u.astype(BF16), bmat_ref[...])
    for j in range(nblk):
        x_ref[j] = x[:, j * LANES:(j + 1) * LANES]
    a = tuple(jnp.broadcast_to(a_ref[j:j + 1, :], (nb, LANES)) for j in range(nblk))
    h = _s5_scan(x_ref, tuple(st_ref[j] for j in range(nblk)), a, tc, reverse)
    for j in range(nblk):
        st_ref[j] = h[j]
    hs = jnp.concatenate([x_ref[j] for j in range(nblk)], axis=1)
    return u, _dot(hs.astype(BF16), cmat_ref[...])


def _s5_fwd_kernel(u_ref, a_ref, bmat_ref, cmat_ref, y_ref, x_ref, st_ref):
    _, y = _s5_direction(u_ref, a_ref, bmat_ref, cmat_ref, x_ref, st_ref, reverse=False)
    y_ref[...] = y.reshape(y_ref.shape)


def _s5_bwd_kernel(u_ref, yf_ref, a_ref, bmat_ref, cmat_ref, d_ref, wglu_ref, bglu_ref, o_ref, x_ref, st_ref):
    u, y = _s5_direction(u_ref, a_ref, bmat_ref, cmat_ref, x_ref, st_ref, reverse=True)
    y = y + u * d_ref[...] + yf_ref[...].reshape(u.shape)
    y = _gelu_tanh(y)
    out = y * jax.nn.sigmoid(_dot(y.astype(BF16), wglu_ref[...]) + bglu_ref[...])
    o_ref[...] = out.reshape(o_ref.shape)


def s5_mixer(proj3, a_f, bmat_f, cmat_f, a_b, bmat_b, cmat_b, d_skip, w_glu, b_glu):
    nb, seq, _ = proj3.shape
    assert nb == SUBLANES, "the scan keeps one batch row per sublane"
    w = S5_WIDTH
    tc = S5_TIME_TILE
    n = seq // tc
    const2 = lambda i: (0, 0)
    par_specs = [pl.BlockSpec((2 * S5_LANE_BLOCKS, LANES), const2),
                 pl.BlockSpec((w, 2 * S5_LANES), const2),
                 pl.BlockSpec((2 * S5_LANES, w), const2)]
    scratch = [pltpu.VMEM((2 * S5_LANE_BLOCKS, nb * tc, LANES), F32),
               pltpu.VMEM((2 * S5_LANE_BLOCKS, nb, LANES), F32)]
    out_sd = jax.ShapeDtypeStruct((nb, seq, w), F32)
    y_f = pl.pallas_call(
        _s5_fwd_kernel,
        grid=(n,),
        in_specs=[pl.BlockSpec((nb, tc, w), lambda i: (0, i, 0))] + par_specs,
        out_specs=pl.BlockSpec((nb, tc, w), lambda i: (0, i, 0)),
        out_shape=out_sd,
        scratch_shapes=scratch,
        compiler_params=_cparams("arbitrary"),
        name="s5_forward",
    )(proj3, a_f, bmat_f, cmat_f)
    rev = lambda i: (0, n - 1 - i, 0)
    return pl.pallas_call(
        _s5_bwd_kernel,
        grid=(n,),
        in_specs=[pl.BlockSpec((nb, tc, w), rev), pl.BlockSpec((nb, tc, w), rev)] + par_specs
                 + [pl.BlockSpec((1, w), const2), pl.BlockSpec((w, w), const2), pl.BlockSpec((1, w), const2)],
        out_specs=pl.BlockSpec((nb, tc, w), rev),
        out_shape=out_sd,
        scratch_shapes=scratch,
        compiler_params=_cparams("arbitrary"),
        name="s5_backward_glu",
    )(proj3, y_f, a_b, bmat_b, cmat_b, d_skip, w_glu, b_glu)


def s5_discretise(lam_re, lam_im, log_step, b_re, b_im, c_re, c_im):
    g, p, h = S5_GROUPS, S5_STATE, S5_GROUP
    lr, li = lam_re.astype(F32), lam_im.astype(F32)
    dt = jnp.exp(log_step.astype(F32))[:, None]
    mag = jnp.exp(lr * dt)
    abr, abi = mag * jnp.cos(li * dt), mag * jnp.sin(li * dt)
    den = lr * lr + li * li
    cr = ((abr - 1.0) * lr + abi * li) / den
    ci = (abi * lr - (abr - 1.0) * li) / den
    bbr = cr[..., None] * b_re - ci[..., None] * b_im
    bbi = cr[..., None] * b_im + ci[..., None] * b_re
    eye = jnp.eye(g, dtype=F32)
    to_b = lambda t: jnp.einsum('gph,gk->ghkp', t, eye).reshape(g * h, g * p)
    bmat = jnp.concatenate([to_b(bbr), to_b(bbi)], axis=1)
    to_c = lambda t: jnp.einsum('ghp,gk->gpkh', t, eye).reshape(g * p, g * h)
    cmat = jnp.concatenate([to_c(c_re.astype(F32)), -to_c(c_im.astype(F32))], axis=0)
    a = jnp.stack([abr.reshape(g * p), abi.reshape(g * p)]).reshape(2 * S5_LANE_BLOCKS, LANES)
    return a, bmat.astype(BF16), cmat.astype(BF16)


def _conv3_rows(x, prev_row, next_row, w_ref):
    tm = x.shape[0]
    rows = lax.broadcasted_iota(jnp.int32, x.shape, 0)
    xp = jnp.where(rows == 0, prev_row, pltpu.roll(x, 1, 0))
    xn = jnp.where(rows == tm - 1, next_row, pltpu.roll(x, tm - 1, 0))
    return xp * w_ref[0:1, :] + x * w_ref[1:2, :] + xn * w_ref[2:3, :]


def _halo_specs(tm, width, col_block, n_rows):
    per = tm // SUBLANES
    last = n_rows // SUBLANES - 1
    prev = pl.BlockSpec((SUBLANES, width), lambda i, *_: (jnp.maximum(i * per - 1, 0), col_block(*_)))
    nxt = pl.BlockSpec((SUBLANES, width), lambda i, *_: (jnp.minimum((i + 1) * per, last), col_block(*_)))
    return prev, nxt


def _gdn_prep_kernel(x_ref, xp_ref, xn_ref, w_ref, o_ref, *, seq):
    tm = x_ref.shape[0]
    j = pl.program_id(1)
    t0 = (pl.program_id(0) * tm) % seq
    prev_row = jnp.where(t0 == 0, 0.0, xp_ref[SUBLANES - 1:SUBLANES, :])
    next_row = jnp.where(t0 + tm == seq, 0.0, xn_ref[0:1, :])
    y = _silu(_conv3_rows(x_ref[...], prev_row, next_row, w_ref))

    @pl.when(j == 2)
    def _():
        o_ref[...] = y

    @pl.when(j < 2)
    def _():
        scale = jnp.where(j == 0, GDN_HEAD_DIM ** -0.5, 1.0)
        for h in range(GDN_HEADS):
            cols = slice(h * GDN_HEAD_DIM, (h + 1) * GDN_HEAD_DIM)
            yh = y[:, cols]
            o_ref[:, cols] = yh * (lax.rsqrt(jnp.sum(yh * yh, axis=-1, keepdims=True) + EPS) * scale)


def gdn_prepare(proj, conv_w, seq):
    m = proj.shape[0]
    tm = ROW_TILE
    first = (S5_WIDTH + 3 * CONV_WIDTH) // GDN_WIDTH
    prev, nxt = _halo_specs(tm, GDN_WIDTH, lambda j: first + j, m)
    return pl.pallas_call(
        functools.partial(_gdn_prep_kernel, seq=seq),
        grid=(m // tm, 3),
        in_specs=[pl.BlockSpec((tm, GDN_WIDTH), lambda i, j: (i, first + j)), prev, nxt,
                  pl.BlockSpec((3, GDN_WIDTH), lambda i, j: (0, j))],
        out_specs=pl.BlockSpec((tm, GDN_WIDTH), lambda i, j: (i, j)),
        out_shape=jax.ShapeDtypeStruct((m, 3 * GDN_WIDTH), F32),
        compiler_params=_cparams("parallel", "parallel"),
        name="gdn_prepare",
    )(proj, proj, proj, conv_w)


def _gdn_gates_kernel(pre_ref, alog_ref, dtb_ref, o_ref):
    seq = pre_ref.shape[2]
    nh2 = 2 * GDN_HEADS
    g = -jnp.exp(alog_ref[...]) * _softplus(pre_ref[0, 0:nh2, :] + dtb_ref[...])
    o_ref[0, nh2:2 * nh2, :] = jax.nn.sigmoid(pre_ref[0, nh2:2 * nh2, :])
    s = lax.broadcasted_iota(jnp.int32, (LANES, LANES), 0)
    t = lax.broadcasted_iota(jnp.int32, (LANES, LANES), 1)
    same = (s // GDN_CHUNK) == (t // GDN_CHUNK)
    m_f = jnp.where(same & (s <= t), 1.0, 0.0)
    m_b = jnp.where(same & (s >= t), 1.0, 0.0)
    for jb in range(seq // LANES):
        cols = slice(jb * LANES, (jb + 1) * LANES)
        gb = g[:, cols]
        o_ref[0, 0:GDN_HEADS, cols] = _dot(gb, m_f, precision=HIGHEST)[0:GDN_HEADS]
        o_ref[0, GDN_HEADS:nh2, cols] = _dot(gb, m_b, precision=HIGHEST)[GDN_HEADS:nh2]


def gdn_gates(pre_t, a_log, dt_bias):
    nb, rows, seq = pre_t.shape
    col = lambda t: t.astype(F32).reshape(2 * GDN_HEADS, 1)
    return pl.pallas_call(
        _gdn_gates_kernel,
        grid=(nb,),
        in_specs=[pl.BlockSpec((1, rows, seq), lambda b: (b, 0, 0)),
                  pl.BlockSpec((2 * GDN_HEADS, 1), lambda b: (0, 0)),
                  pl.BlockSpec((2 * GDN_HEADS, 1), lambda b: (0, 0))],
        out_specs=pl.BlockSpec((1, rows, seq), lambda b: (b, 0, 0)),
        out_shape=jax.ShapeDtypeStruct((nb, rows, seq), F32),
        compiler_params=_cparams("parallel"),
        name="gdn_gates",
    )(pre_t, col(a_log), col(dt_bias))


def _nilpotent_inverses(mats, nilpotency):
    n = mats[0].shape[0]
    eye = jnp.where(lax.broadcasted_iota(jnp.int32, (n, n), 0) == lax.broadcasted_iota(jnp.int32, (n, n), 1), 1.0, 0.0)
    ps = [eye - a for a in mats]
    a16 = [a.astype(BF16) for a in mats]
    sqs = [_dot(a, a) for a in a16]
    k = 2
    while True:
        sq16 = [s.astype(BF16) for s in sqs]
        ps = [p + _dot(p.astype(BF16), s) for p, s in zip(ps, sq16)]
        k *= 2
        if k >= nilpotency:
            return ps
        sqs = [_dot(s, s) for s in sq16]


def _gdn_local_kernel(q_ref, k_ref, v_ref, gcol_ref, grow_ref, uwqf_ref, uwqb_ref, ktf_ref, ktb_ref, in_ref, eg_ref):
    r = q_ref.shape[1]
    c = GDN_CHUNK
    hd = GDN_HEAD_DIM
    ii = lax.broadcasted_iota(jnp.int32, (r, r), 0)
    jj = lax.broadcasted_iota(jnp.int32, (r, r), 1)
    same = (ii >= c) == (jj >= c)
    incl = (same & (ii >= jj), same & (ii <= jj))
    strict = (same & (ii > jj), same & (ii < jj))
    first_chunk = lax.broadcasted_iota(jnp.int32, (r, 1), 0) < c
    zeros = jnp.zeros((r, r), F32)
    big_a, rhs, keep = [], [], []
    for h in range(GDN_HEADS):
        cols = slice(h * hd, (h + 1) * hd)
        q, k, v = q_ref[0, :, cols], k_ref[0, :, cols], v_ref[0, :, cols]
        k16 = k.astype(BF16)
        kk = _dot_nt(k16, k16)
        qk = _dot_nt(q.astype(BF16), k16)
        a_dir, rhs_dir = [], []
        for d in range(2):
            lane = d * GDN_HEADS + h
            gc_col = gcol_ref[0, :, lane:lane + 1]
            gc_row = grow_ref[0, lane:lane + 1, :]
            beta = gcol_ref[0, :, 2 * GDN_HEADS + lane:2 * GDN_HEADS + lane + 1]
            decay = jnp.exp(jnp.where(incl[d], gc_col - gc_row, -1e30))
            a_dir.append(jnp.where(strict[d], kk * beta * decay, 0.0))
            intra = jnp.where(incl[d], qk * decay, 0.0)
            ends = (gc_col[c - 1:c], gc_col[2 * c - 1:2 * c]) if d == 0 else (gc_col[0:1], gc_col[c:c + 1])
            g_last = jnp.where(first_chunk, ends[0], ends[1])
            eg = jnp.exp(gc_col)
            kb = k * beta
            rhs_dir.append(jnp.concatenate([v * beta, kb * eg], axis=1))
            keep.append((q * eg, (k * jnp.exp(g_last - gc_col)).T, intra[:, :c] + intra[:, c:]))
            for ch in range(2):
                eg_ref[0, ch, lane:lane + 1, :] = jnp.broadcast_to(jnp.exp(ends[ch]), (1, LANES))
        big_a.append(jnp.concatenate([jnp.concatenate([a_dir[0], zeros], axis=1),
                                      jnp.concatenate([zeros, a_dir[1]], axis=1)], axis=0))
        rhs.append(jnp.concatenate(rhs_dir, axis=0))
    t_inv = _nilpotent_inverses(big_a, c)
    for h in range(GDN_HEADS):
        uw = _dot(t_inv[h].astype(BF16), rhs[h].astype(BF16))
        (qd_f, kt_f, in_f), (qd_b, kt_b, in_b) = keep[2 * h], keep[2 * h + 1]
        uwqf_ref[0, h] = jnp.concatenate([uw[:r], qd_f], axis=1).astype(BF16)
        uwqb_ref[0, h] = jnp.concatenate([uw[r:], qd_b], axis=1).astype(BF16)
        ktf_ref[0, h] = kt_f.astype(BF16)
        ktb_ref[0, h] = kt_b.astype(BF16)
        in_ref[0, h] = jnp.concatenate([in_f, in_b], axis=1).astype(BF16)


def gdn_local(qkv, gates_col, gates_row):
    nb, seq, _ = qkv.shape
    r = 2 * GDN_CHUNK
    hd, nh = GDN_HEAD_DIM, GDN_HEADS
    uwq_sd = jax.ShapeDtypeStruct((nb, nh, seq, 3 * hd), BF16)
    kt_sd = jax.ShapeDtypeStruct((nb, nh, hd, seq), BF16)
    uwq_spec = pl.BlockSpec((1, nh, r, 3 * hd), lambda b, i: (b, 0, i, 0))
    kt_spec = pl.BlockSpec((1, nh, hd, r), lambda b, i: (b, 0, 0, i))
    return pl.pallas_call(
        _gdn_local_kernel,
        grid=(nb, seq // r),
        in_specs=[pl.BlockSpec((1, r, GDN_WIDTH), lambda b, i: (b, i, 0)),
                  pl.BlockSpec((1, r, GDN_WIDTH), lambda b, i: (b, i, 1)),
                  pl.BlockSpec((1, r, GDN_WIDTH), lambda b, i: (b, i, 2)),
                  pl.BlockSpec((1, r, 4 * nh), lambda b, i: (b, i, 0)),
                  pl.BlockSpec((1, 4 * nh, r), lambda b, i: (b, 0, i))],
        out_specs=[uwq_spec, uwq_spec, kt_spec, kt_spec,
                   pl.BlockSpec((1, nh, r, 2 * GDN_CHUNK), lambda b, i: (b, 0, i, 0)),
                   pl.BlockSpec((1, 2, SUBLANES, LANES), lambda b, i: (b, i, 0, 0))],
        out_shape=[uwq_sd, uwq_sd, kt_sd, kt_sd,
                   jax.ShapeDtypeStruct((nb, nh, seq, 2 * GDN_CHUNK), BF16),
                   jax.ShapeDtypeStruct((nb, seq // GDN_CHUNK, SUBLANES, LANES), F32)],
        compiler_params=_cparams("parallel", "parallel"),
        name="gdn_local",
    )(qkv, qkv, qkv, gates_col, gates_row)


def _gdn_state_kernel(uwqf, ktf, inf, egf, uwqb, ktb, inb, egb, of_ref, ob_ref, st_ref):
    @pl.when(pl.program_id(1) == 0)
    def _():
        st_ref[...] = jnp.zeros_like(st_ref)

    c = GDN_CHUNK
    hd = GDN_HEAD_DIM
    nc = uwqf.shape[2] // c
    chains = [(d, h) for d in range(2) for h in range(GDN_HEADS)]
    refs = ((uwqf, ktf, inf, egf, of_ref), (uwqb, ktb, inb, egb, ob_ref))
    states = [st_ref[j] for j in range(len(chains))]
    for step in range(nc):
        sws = []
        for j, (d, h) in enumerate(chains):
            ci = step if d == 0 else nc - 1 - step
            rows = slice(ci * c, (ci + 1) * c)
            uwq = refs[d][0]
            wq = jnp.concatenate([uwq[0, h, rows, hd:2 * hd], uwq[0, h, rows, 2 * hd:3 * hd]], axis=0)
            sws.append(_dot(wq, states[j].astype(BF16)))
        for j, (d, h) in enumerate(chains):
            ci = step if d == 0 else nc - 1 - step
            rows = slice(ci * c, (ci + 1) * c)
            uwq, kt, intra, eg, o_ref = refs[d]
            sw = sws[j]
            v16 = (uwq[0, h, rows, 0:hd].astype(F32) - sw[:c]).astype(BF16)
            o_ref[0, rows, h * hd:(h + 1) * hd] = sw[c:] + _dot(intra[0, h, rows, d * c:(d + 1) * c], v16)
            lane = d * GDN_HEADS + h
            states[j] = states[j] * eg[0, ci, lane:lane + 1, :] + _dot(kt[0, h, :, rows], v16)
    for j in range(len(chains)):
        st_ref[j] = states[j]


def gdn_state_scan(uwq_f, uwq_b, kt_f, kt_b, intra, eg):
    nb, nh, seq, _ = uwq_f.shape
    hd = GDN_HEAD_DIM
    nc = GDN_STATE_CHUNKS
    r = nc * GDN_CHUNK
    n = seq // r

    def specs(pos):
        return [pl.BlockSpec((1, nh, r, 3 * hd), lambda b, i: (b, 0, pos(i), 0)),
                pl.BlockSpec((1, nh, hd, r), lambda b, i: (b, 0, 0, pos(i))),
                pl.BlockSpec((1, nh, r, 2 * GDN_CHUNK), lambda b, i: (b, 0, pos(i), 0)),
                pl.BlockSpec((1, nc, SUBLANES, LANES), lambda b, i: (b, pos(i), 0, 0))]

    fwd = lambda i: i
    bwd = lambda i: n - 1 - i
    out_sd = jax.ShapeDtypeStruct((nb, seq, GDN_WIDTH), F32)
    return pl.pallas_call(
        _gdn_state_kernel,
        grid=(nb, n),
        in_specs=specs(fwd) + specs(bwd),
        out_specs=[pl.BlockSpec((1, r, GDN_WIDTH), lambda b, i: (b, i, 0)),
                   pl.BlockSpec((1, r, GDN_WIDTH), lambda b, i: (b, n - 1 - i, 0))],
        out_shape=[out_sd, out_sd],
        scratch_shapes=[pltpu.VMEM((2 * nh, hd, hd), F32)],
        compiler_params=_cparams("parallel", "arbitrary"),
        name="gdn_state_scan",
    )(uwq_f, kt_f, intra, eg, uwq_b, kt_b, intra, eg)


def _out_proj_kernel(x_ref, mod_ref, ys5_ref, cx_ref, cb_ref, cc_ref, cxp_ref, cxn_ref, ccp_ref, ccn_ref,
                     cw_ref, of_ref, ob_ref, z_ref, gnw_ref, w_ref, o_ref, *, seq):
    tm = x_ref.shape[0]
    t0 = (pl.program_id(0) * tm) % seq
    last = SUBLANES - 1
    prev_row = jnp.where(t0 == 0, 0.0, ccp_ref[last:SUBLANES, :] * cxp_ref[last:SUBLANES, :])
    next_row = jnp.where(t0 + tm == seq, 0.0, ccn_ref[0:1, :] * cxn_ref[0:1, :])
    y_conv = cb_ref[...] * _conv3_rows(cc_ref[...] * cx_ref[...], prev_row, next_row, cw_ref)
    o = of_ref[...] + ob_ref[...]
    z = z_ref[...]
    heads = []
    for h in range(GDN_HEADS):
        cols = slice(h * GDN_HEAD_DIM, (h + 1) * GDN_HEAD_DIM)
        heads.append(_rms(o[:, cols]) * gnw_ref[...] * _silu(z[:, cols]))
    y_gdn = jnp.concatenate(heads, axis=1)
    c0, c1 = S5_WIDTH, S5_WIDTH + CONV_WIDTH
    mix = (_dot(ys5_ref[...].astype(BF16), w_ref[0:c0, :])
           + _dot(y_conv.astype(BF16), w_ref[c0:c1, :])
           + _dot(y_gdn.astype(BF16), w_ref[c1:, :]))
    o_ref[...] = x_ref[...] + mod_ref[0, 2:3, :] * mix


def out_projection(x2d, mod, y_s5, proj, conv_w, o_f, o_b, gdn_norm_w, w_out, seq):
    m, d = x2d.shape
    tm = ROW_TILE
    per_seq = seq // tm
    cw = CONV_WIDTH
    row = lambda c: pl.BlockSpec((tm, cw), lambda i: (i, c))
    cxp, cxn = _halo_specs(tm, cw, lambda: 1, m)
    ccp, ccn = _halo_specs(tm, cw, lambda: 3, m)
    gw = GDN_WIDTH
    return pl.pallas_call(
        functools.partial(_out_proj_kernel, seq=seq),
        grid=(m // tm,),
        in_specs=[pl.BlockSpec((tm, d), lambda i: (i, 0)),
                  pl.BlockSpec((1, 6, d), lambda i: (i // per_seq, 0, 0)),
                  pl.BlockSpec((tm, cw), lambda i: (i, 0)),
                  row(1), row(2), row(3), cxp, cxn, ccp, ccn,
                  pl.BlockSpec((3, cw), lambda i: (0, 0)),
                  pl.BlockSpec((tm, gw), lambda i: (i, 0)),
                  pl.BlockSpec((tm, gw), lambda i: (i, 0)),
                  pl.BlockSpec((tm, gw), lambda i: (i, (PROJ_MAIN - gw) // gw)),
                  pl.BlockSpec((1, GDN_HEAD_DIM), lambda i: (0, 0)),
                  pl.BlockSpec((d, d), lambda i: (0, 0))],
        out_specs=pl.BlockSpec((tm, d), lambda i: (i, 0)),
        out_shape=jax.ShapeDtypeStruct((m, d), F32),
        compiler_params=_cparams("parallel"),
        name="out_projection",
    )(x2d, mod, y_s5, proj, proj, proj, proj, proj, proj, proj, conv_w, o_f, o_b, proj, gdn_norm_w, w_out)


def _first_argmax(vals, rows, n):
    m = jnp.max(vals, axis=0, keepdims=True)
    return m, jnp.min(jnp.where(vals == m, rows, n), axis=0, keepdims=True)


def _route_kernel(x_ref, mod_ref, nw_ref, wr_ref, br_ref, h_ref, idx_ref, wgt_ref, cnt_ref, carry_ref):
    step = pl.program_id(0)

    @pl.when(step == 0)
    def _():
        carry_ref[...] = jnp.zeros_like(carry_ref)

    tm = x_ref.shape[0]
    h = _rms(x_ref[...]) * nw_ref[...] * (1.0 + mod_ref[0, 4:5, :]) + mod_ref[0, 3:4, :]
    for s in range(SUBLANES):
        h_ref[:, s, :] = h[:, s * LANES:(s + 1) * LANES]
    logits = _dot_nt(wr_ref[...], h, precision=HIGHEST) + br_ref[...]
    lg = logits[N_EXPERTS:N_EXPERTS + MOE_GROUPS, :]
    rows_g = lax.broadcasted_iota(jnp.int32, lg.shape, 0)
    g_max, g_idx = _first_argmax(lg, rows_g, MOE_GROUPS)
    g_w = 1.0 / jnp.sum(jnp.exp(lg - g_max), axis=0, keepdims=True)
    le = jnp.zeros((EXPERTS_PER_GROUP, tm), F32)
    for g in range(MOE_GROUPS):
        le = jnp.where(g_idx == g, logits[g * EXPERTS_PER_GROUP:(g + 1) * EXPERTS_PER_GROUP, :], le)
    rows_e = lax.broadcasted_iota(jnp.int32, le.shape, 0)
    m1, i1 = _first_argmax(le, rows_e, EXPERTS_PER_GROUP)
    m2, i2 = _first_argmax(jnp.where(rows_e == i1, -jnp.inf, le), rows_e, EXPERTS_PER_GROUP)
    r = jnp.exp(m2 - m1)
    w1 = g_w / (1.0 + r)
    e1 = g_idx * EXPERTS_PER_GROUP + i1
    e2 = g_idx * EXPERTS_PER_GROUP + i2
    rows_x = lax.broadcasted_iota(jnp.int32, (N_EXPERTS, tm), 0)
    hit1 = rows_x == e1
    hit2 = rows_x == e2
    onehot = jnp.where(hit1 | hit2, 1.0, 0.0)
    before = lax.broadcasted_iota(jnp.int32, (tm, tm), 0) < lax.broadcasted_iota(jnp.int32, (tm, tm), 1)
    ahead = _dot(onehot.astype(BF16), jnp.where(before, 1.0, 0.0).astype(BF16)) + carry_ref[:, 0:1]
    rank1 = jnp.sum(jnp.where(hit1, ahead, 0.0), axis=0, keepdims=True)
    rank2 = jnp.sum(jnp.where(hit2, ahead, 0.0), axis=0, keepdims=True)
    idx_ref[0:1, :] = e1
    idx_ref[1:2, :] = e2
    idx_ref[2:3, :] = rank1.astype(jnp.int32)
    idx_ref[3:4, :] = rank2.astype(jnp.int32)
    wgt_ref[0:1, :] = w1
    wgt_ref[1:2, :] = w1 * r
    carry_ref[...] = carry_ref[...] + jnp.sum(onehot, axis=1, keepdims=True)
    cnt_ref[...] = carry_ref[...]


def moe_route(x2d, mod, norm_w, w_router_t, b_router, seq):
    m, d = x2d.shape
    tm = ROUTE_TILE
    per_seq = seq // tm
    nr = w_router_t.shape[0]
    return pl.pallas_call(
        _route_kernel,
        grid=(m // tm,),
        in_specs=[pl.BlockSpec((tm, d), lambda i: (i, 0)),
                  pl.BlockSpec((1, 6, d), lambda i: (i // per_seq, 0, 0)),
                  pl.BlockSpec((1, d), lambda i: (0, 0)),
                  pl.BlockSpec((nr, d), lambda i: (0, 0)),
                  pl.BlockSpec((nr, 1), lambda i: (0, 0))],
        out_specs=[pl.BlockSpec((tm,) + SLAB, lambda i: (i, 0, 0)),
                   pl.BlockSpec((4, tm), lambda i: (0, i)),
                   pl.BlockSpec((2, tm), lambda i: (0, i)),
                   pl.BlockSpec((N_EXPERTS, LANES), lambda i: (0, 0))],
        out_shape=[jax.ShapeDtypeStruct((m,) + SLAB, F32),
                   jax.ShapeDtypeStruct((4, m), jnp.int32),
                   jax.ShapeDtypeStruct((2, m), F32),
                   jax.ShapeDtypeStruct((N_EXPERTS, LANES), F32)],
        scratch_shapes=[pltpu.VMEM((N_EXPERTS, LANES), F32)],
        compiler_params=_cparams("arbitrary"),
        name="moe_route",
    )(x2d, mod, norm_w, w_router_t, b_router)


def _row_copy(src_ref, src_row, dst_ref, dst_row, sem):
    return pltpu.make_async_copy(src_ref.at[pl.ds(src_row, 1)], dst_ref.at[pl.ds(dst_row, 1)], sem)


def _scatter_kernel(last_ref, d1_ref, d2_ref, h_ref, xs_ref, zero_ref, sem):
    tm = h_ref.shape[0]

    @pl.when(pl.program_id(0) == 0)
    def _():
        zero_ref[...] = jnp.zeros_like(zero_ref)

        def fill(e, c):
            @pl.when(last_ref[e] >= 0)
            def _():
                pltpu.make_async_copy(zero_ref, xs_ref.at[pl.ds(last_ref[e], MOE_BLOCK)], sem).start()
            return c

        def drain(e, c):
            @pl.when(last_ref[e] >= 0)
            def _():
                pltpu.make_async_copy(zero_ref, xs_ref.at[pl.ds(0, MOE_BLOCK)], sem).wait()
            return c

        lax.fori_loop(0, N_EXPERTS, fill, 0)
        lax.fori_loop(0, N_EXPERTS, drain, 0)

        n_blocks = xs_ref.shape[0] // MOE_BLOCK

        def fill_unused(b, c):
            pltpu.make_async_copy(zero_ref, xs_ref.at[pl.ds(b * MOE_BLOCK, MOE_BLOCK)], sem).start()
            return c

        def drain_unused(b, c):
            pltpu.make_async_copy(zero_ref, xs_ref.at[pl.ds(0, MOE_BLOCK)], sem).wait()
            return c

        lax.fori_loop(last_ref[N_EXPERTS], n_blocks, fill_unused, 0)
        lax.fori_loop(last_ref[N_EXPERTS], n_blocks, drain_unused, 0)

    def start(r, c):
        _row_copy(h_ref, r, xs_ref, d1_ref[0, 0, r], sem).start()
        _row_copy(h_ref, r, xs_ref, d2_ref[0, 0, r], sem).start()
        return c

    def wait(r, c):
        _row_copy(h_ref, 0, xs_ref, 0, sem).wait()
        _row_copy(h_ref, 0, xs_ref, 0, sem).wait()
        return c

    lax.fori_loop(0, tm, start, 0, unroll=DMA_UNROLL)
    lax.fori_loop(0, tm, wait, 0, unroll=DMA_UNROLL)


def moe_scatter(h_slab, dest1, dest2, last_block_start, n_slots):
    m = h_slab.shape[0]
    tm = ROUTE_TILE
    idx_spec = pl.BlockSpec((1, 1, tm), lambda i, last: (i, 0, 0), memory_space=pltpu.SMEM)
    return pl.pallas_call(
        _scatter_kernel,
        grid_spec=pltpu.PrefetchScalarGridSpec(
            num_scalar_prefetch=1,
            grid=(m // tm,),
            in_specs=[idx_spec, idx_spec, pl.BlockSpec((tm,) + SLAB, lambda i, last: (i, 0, 0))],
            out_specs=pl.BlockSpec(memory_space=pl.ANY),
            scratch_shapes=[pltpu.VMEM((MOE_BLOCK,) + SLAB, F32), pltpu.SemaphoreType.DMA(())]),
        out_shape=jax.ShapeDtypeStruct((n_slots,) + SLAB, F32),
        compiler_params=_cparams("arbitrary"),
        name="moe_scatter",
    )(last_block_start, dest1.reshape(m // tm, 1, tm), dest2.reshape(m // tm, 1, tm), h_slab)


def _expert_kernel(be_ref, nused_ref, xs_ref, wgu_ref, wd_ref, y_ref, wgu16_ref, wd16_ref):
    i = pl.program_id(0)
    changed = jnp.logical_or(i == 0, be_ref[i] != be_ref[jnp.maximum(i - 1, 0)])

    @pl.when(changed)
    def _():
        wgu16_ref[...] = wgu_ref[0].astype(BF16)
        wd16_ref[...] = wd_ref[0].astype(BF16)

    @pl.when(i < nused_ref[0])
    def _():
        x = jnp.concatenate([xs_ref[:, s, :] for s in range(SUBLANES)], axis=1).astype(BF16)
        gu = _dot(x, wgu16_ref[...])
        act = _silu(gu[:, :EXPERT_FF]) * gu[:, EXPERT_FF:]
        y = _dot(act.astype(BF16), wd16_ref[...])
        for s in range(SUBLANES):
            y_ref[:, s, :] = y[:, s * LANES:(s + 1) * LANES]

    @pl.when(i >= nused_ref[0])
    def _():
        y_ref[...] = jnp.zeros_like(y_ref)


def moe_experts(xs, blk_expert, n_used, w_gate_up, w_down):
    n_slots = xs.shape[0]
    n_blocks = n_slots // MOE_BLOCK
    d = D_MODEL
    return pl.pallas_call(
        _expert_kernel,
        grid_spec=pltpu.PrefetchScalarGridSpec(
            num_scalar_prefetch=2,
            grid=(n_blocks,),
            in_specs=[pl.BlockSpec((MOE_BLOCK,) + SLAB, lambda i, be, nu: (jnp.minimum(i, nu[0] - 1), 0, 0)),
                      pl.BlockSpec((1, d, 2 * EXPERT_FF), lambda i, be, nu: (be[i], 0, 0)),
                      pl.BlockSpec((1, EXPERT_FF, d), lambda i, be, nu: (be[i], 0, 0))],
            out_specs=pl.BlockSpec((MOE_BLOCK,) + SLAB, lambda i, be, nu: (i, 0, 0)),
            scratch_shapes=[pltpu.VMEM((d, 2 * EXPERT_FF), BF16), pltpu.VMEM((EXPERT_FF, d), BF16)]),
        out_shape=jax.ShapeDtypeStruct((n_slots,) + SLAB, F32),
        compiler_params=_cparams("arbitrary"),
        name="moe_experts",
    )(blk_expert, n_used, xs, w_gate_up, w_down)


def _combine_kernel(d1_ref, d2_ref, x_ref, mod_ref, wgt_ref, fnw_ref, yb_ref, o_ref, y1_ref, y2_ref, sem, *, final):
    tm = x_ref.shape[0]

    def start(r, c):
        _row_copy(yb_ref, d1_ref[0, 0, r], y1_ref, r, sem).start()
        _row_copy(yb_ref, d2_ref[0, 0, r], y2_ref, r, sem).start()
        return c

    def wait(r, c):
        _row_copy(yb_ref, 0, y1_ref, 0, sem).wait()
        _row_copy(yb_ref, 0, y2_ref, 0, sem).wait()
        return c

    lax.fori_loop(0, tm, start, 0, unroll=DMA_UNROLL)
    lax.fori_loop(0, tm, wait, 0, unroll=DMA_UNROLL)
    w1 = wgt_ref[:, 0:1]
    w2 = wgt_ref[:, 1:2]
    for s in range(SUBLANES):
        cols = slice(s * LANES, (s + 1) * LANES)
        y = y1_ref[:, s, :] * w1 + y2_ref[:, s, :] * w2
        o_ref[:, cols] = x_ref[:, cols] + mod_ref[0, 5:6, cols] * y
    if final:
        o_ref[...] = _rms(o_ref[...]) * fnw_ref[...]


def moe_combine(x2d, mod, yb, dest1, dest2, weights, final_norm_w, seq, final):
    m, d = x2d.shape
    tm = COMBINE_TILE
    per_seq = seq // tm
    idx_spec = pl.BlockSpec((1, 1, tm), lambda i: (i, 0, 0), memory_space=pltpu.SMEM)
    return pl.pallas_call(
        functools.partial(_combine_kernel, final=final),
        grid=(m // tm,),
        in_specs=[idx_spec, idx_spec,
                  pl.BlockSpec((tm, d), lambda i: (i, 0)),
                  pl.BlockSpec((1, 6, d), lambda i: (i // per_seq, 0, 0)),
                  pl.BlockSpec((tm, 2), lambda i: (i, 0)),
                  pl.BlockSpec((1, d), lambda i: (0, 0)),
                  pl.BlockSpec(memory_space=pl.ANY)],
        out_specs=pl.BlockSpec((tm, d), lambda i: (i, 0)),
        out_shape=jax.ShapeDtypeStruct((m, d), F32),
        scratch_shapes=[pltpu.VMEM((tm,) + SLAB, F32), pltpu.VMEM((tm,) + SLAB, F32),
                        pltpu.SemaphoreType.DMA(())],
        compiler_params=_cparams("arbitrary"),
        name="moe_combine",
    )(dest1.reshape(m // tm, 1, tm), dest2.reshape(m // tm, 1, tm), x2d, mod, weights, final_norm_w, yb)


def hier_moe_layer(x2d, mod, norm_w, w_router_t, b_router, w_gate_up, w_down, final_norm_w, seq, final):
    m = x2d.shape[0]
    h_slab, idx, wgt, counts = moe_route(x2d, mod, norm_w, w_router_t, b_router, seq)
    counts = counts[:, 0].astype(jnp.int32)
    padded = (counts + MOE_BLOCK - 1) // MOE_BLOCK * MOE_BLOCK
    pad_end = jnp.cumsum(padded)
    pad_start = pad_end - padded
    dest1 = pad_start[idx[0]] + idx[2]
    dest2 = pad_start[idx[1]] + idx[3]
    n_blocks = 2 * m // MOE_BLOCK + N_EXPERTS
    blk_start = jnp.arange(n_blocks, dtype=jnp.int32) * MOE_BLOCK
    blk_expert = jnp.minimum(jnp.sum((pad_end[None, :] <= blk_start[:, None]).astype(jnp.int32), axis=1),
                             N_EXPERTS - 1)
    n_used = (pad_end[-1:] // MOE_BLOCK).astype(jnp.int32)
    fill_plan = jnp.concatenate([jnp.where(padded > 0, pad_end - MOE_BLOCK, -1), n_used]).astype(jnp.int32)
    xs = moe_scatter(h_slab, dest1, dest2, fill_plan, n_blocks * MOE_BLOCK)
    yb = moe_experts(xs, blk_expert, n_used, w_gate_up, w_down)
    return moe_combine(x2d, mod, yb, dest1, dest2, wgt.T, final_norm_w, seq, final)


def _layer_params(l, p):
    w_in = p['w_in'][l]
    w_gate = jnp.zeros((D_MODEL, LANES), F32).at[:, :N_GATE_COLS].set(w_in[:, PROJ_MAIN:])
    w_router_t = jnp.zeros((N_EXPERTS + SUBLANES, D_MODEL), F32)
    w_router_t = w_router_t.at[:N_EXPERTS].set(p['moe_w_expert'][l].T)
    w_router_t = w_router_t.at[N_EXPERTS:N_EXPERTS + MOE_GROUPS].set(p['moe_w_group'][l].T)
    b_router = jnp.zeros((N_EXPERTS + SUBLANES, 1), F32)
    b_router = b_router.at[:N_EXPERTS, 0].set(p['moe_b_expert'][l])
    b_router = b_router.at[N_EXPERTS:N_EXPERTS + MOE_GROUPS, 0].set(p['moe_b_group'][l])
    s5 = [s5_discretise(p['s5_lam_re'][l, d], p['s5_lam_im'][l, d], p['s5_log_step'][l, d],
                        p['s5_b_re'][l, d], p['s5_b_im'][l, d], p['s5_c_re'][l, d], p['s5_c_im'][l, d])
          for d in range(2)]
    return dict(
        norm_mix=p['norm_mix'][l][None, :], norm_ffn=p['norm_ffn'][l][None, :],
        w_main=w_in[:, :PROJ_MAIN].astype(BF16), w_gate=w_gate.astype(BF16),
        w_out=p['w_out'][l].astype(BF16), s5=s5,
        s5_d=p['s5_d'][l][None, :], s5_w_glu=p['s5_w_glu'][l].astype(BF16), s5_b_glu=p['s5_b_glu'][l][None, :],
        conv_w=p['conv_w'][l], gdn_conv_w=p['gdn_conv_w'][l], gdn_a_log=p['gdn_a_log'][l],
        gdn_dt_bias=p['gdn_dt_bias'][l], gdn_norm_w=p['gdn_norm_w'][l][None, :],
        w_router_t=w_router_t, b_router=b_router,
        w_gate_up=p['moe_w_gate_up'][l], w_down=p['moe_w_down'][l])


def _trunk(x, mods, layers, final_norm_w):
    nb, seq, d = x.shape
    m = nb * seq
    x2d = x.reshape(m, d)
    for l, lp in enumerate(layers):
        mod = mods[l].reshape(nb, 6, d)
        proj, gate_pre = in_projection(x2d, mod, lp['norm_mix'], lp['w_main'], lp['w_gate'], seq)
        (a_f, bm_f, cm_f), (a_b, bm_b, cm_b) = lp['s5']
        y_s5 = s5_mixer(proj.reshape(nb, seq, PROJ_MAIN), a_f, bm_f, cm_f, a_b, bm_b, cm_b,
                        lp['s5_d'], lp['s5_w_glu'], lp['s5_b_glu']).reshape(m, S5_WIDTH)
        qkv = gdn_prepare(proj, lp['gdn_conv_w'], seq).reshape(nb, seq, 3 * GDN_WIDTH)
        pre_t = gate_pre[:, :N_GATE_COLS].reshape(nb, seq, N_GATE_COLS).transpose(0, 2, 1)
        gates = gdn_gates(pre_t, lp['gdn_a_log'], lp['gdn_dt_bias'])
        uwq_f, uwq_b, kt_f, kt_b, intra, eg = gdn_local(qkv, gates.transpose(0, 2, 1), gates)
        o_f, o_b = gdn_state_scan(uwq_f, uwq_b, kt_f, kt_b, intra, eg)
        x2d = out_projection(x2d, mod, y_s5, proj, lp['conv_w'], o_f.reshape(m, GDN_WIDTH),
                             o_b.reshape(m, GDN_WIDTH), lp['gdn_norm_w'], lp['w_out'], seq)
        x2d = hier_moe_layer(x2d, mod, lp['norm_ffn'], lp['w_router_t'], lp['b_router'],
                             lp['w_gate_up'], lp['w_down'], final_norm_w, seq, final=(l == len(layers) - 1))
    return x2d.reshape(nb, seq, d)


def kernel(x_prompt, x_sample, c_prompt, c_sample, w_ada, b_ada, norm_mix, norm_ffn, w_in, w_out, s5_lam_re, s5_lam_im, s5_log_step, s5_b_re, s5_b_im, s5_c_re, s5_c_im, s5_d, s5_w_glu, s5_b_glu, conv_w, gdn_conv_w, gdn_a_log, gdn_dt_bias, gdn_norm_w, moe_w_group, moe_b_group, moe_w_expert, moe_b_expert, moe_w_gate_up, moe_w_down, final_norm):
    p = dict(norm_mix=norm_mix, norm_ffn=norm_ffn, w_in=w_in, w_out=w_out, s5_lam_re=s5_lam_re,
             s5_lam_im=s5_lam_im, s5_log_step=s5_log_step, s5_b_re=s5_b_re, s5_b_im=s5_b_im, s5_c_re=s5_c_re,
             s5_c_im=s5_c_im, s5_d=s5_d, s5_w_glu=s5_w_glu, s5_b_glu=s5_b_glu, conv_w=conv_w,
             gdn_conv_w=gdn_conv_w, gdn_a_log=gdn_a_log, gdn_dt_bias=gdn_dt_bias, gdn_norm_w=gdn_norm_w,
             moe_w_group=moe_w_group, moe_b_group=moe_b_group, moe_w_expert=moe_w_expert,
             moe_b_expert=moe_b_expert, moe_w_gate_up=moe_w_gate_up, moe_w_down=moe_w_down)
    depth = w_ada.shape[0]
    layers = [_layer_params(l, p) for l in range(depth)]
    nbp = c_prompt.shape[0]
    mods = ada_modulation(jnp.concatenate([c_prompt, c_sample], axis=0), w_ada, b_ada)
    fnw = final_norm[None, :]
    y_prompt = _trunk(x_prompt, mods[:, :nbp], layers, fnw)
    y_sample = _trunk(x_sample, mods[:, nbp:], layers, fnw)
    return (y_prompt, y_sample)
```

```python
import functools

import jax
import jax.numpy as jnp
from jax import lax
from jax.experimental import pallas as pl
from jax.experimental.pallas import tpu as pltpu

F32 = jnp.float32
BF16 = jnp.bfloat16
HIGHEST = lax.Precision.HIGHEST

D_MODEL = 1024
DEPTH = 4
S5_WIDTH = 256
S5_GROUP = 16
S5_GROUPS = 16
S5_STATE = 64
S5_LANES = S5_GROUPS * S5_STATE
CONV_WIDTH = 256
GDN_WIDTH = 512
GDN_HEADS = 4
GDN_HEAD_DIM = 128
GDN_CHUNK = 64
PROJ_MAIN = S5_WIDTH + 3 * CONV_WIDTH + 4 * GDN_WIDTH
N_GATE_COLS = 4 * GDN_HEADS
MOE_GROUPS = 4
EXPERTS_PER_GROUP = 8
N_EXPERTS = 32
EXPERT_FF = 512
MOE_BLOCK = 512
EPS = 1e-6

SUBLANES = 8
LANES = 128
SLAB = (SUBLANES, LANES)
VMEM_LIMIT = 56 * 1024 * 1024

ROW_TILE = 512
S5_TIME_TILE = 64
GDN_STATE_CHUNKS = 8
ROUTE_TILE = 512
COMBINE_TILE = 256
DMA_UNROLL = 8


def _cparams(*sem):
    return pltpu.CompilerParams(dimension_semantics=sem, vmem_limit_bytes=VMEM_LIMIT)


def _dot(a, b, **kw):
    return jnp.dot(a, b, preferred_element_type=F32, **kw)


def _dot_nt(a, b, **kw):
    return lax.dot_general(a, b, (((1,), (1,)), ((), ())), preferred_element_type=F32, **kw)


def _dot_tn(a, b, **kw):
    return lax.dot_general(a, b, (((0,), (0,)), ((), ())), preferred_element_type=F32, **kw)


def _silu(x):
    return x * jax.nn.sigmoid(x)


def _gelu_tanh(x):
    return 0.5 * x * (1.0 + jnp.tanh(0.7978845608028654 * (x + 0.044715 * (x * x * x))))


def _softplus(x):
    return jnp.maximum(x, 0.0) + jnp.log(1.0 + jnp.exp(-jnp.abs(x)))


def _rms(x):
    return x * lax.rsqrt(jnp.mean(x * x, axis=-1, keepdims=True) + EPS)


def _ada_kernel(c_ref, w_ref, b_ref, o_ref):
    o_ref[0] = _dot(_silu(c_ref[...]), w_ref[0], precision=HIGHEST) + b_ref[0]


def ada_modulation(c_all, w_ada, b_ada):
    nb = c_all.shape[0]
    depth, d, n = w_ada.shape
    tn = 1536
    return pl.pallas_call(
        _ada_kernel,
        grid=(depth, n // tn),
        in_specs=[pl.BlockSpec((nb, d), lambda l, j: (0, 0)),
                  pl.BlockSpec((1, d, tn), lambda l, j: (l, 0, j)),
                  pl.BlockSpec((1, 1, tn), lambda l, j: (l, 0, j))],
        out_specs=pl.BlockSpec((1, nb, tn), lambda l, j: (l, 0, j)),
        out_shape=jax.ShapeDtypeStruct((depth, nb, n), F32),
        compiler_params=_cparams("parallel", "parallel"),
        name="ada_modulation",
    )(c_all, w_ada, b_ada.reshape(depth, 1, n))


def _in_proj_kernel(x_ref, mod_ref, nw_ref, w_ref, wg_ref, proj_ref, gate_ref):
    h = _rms(x_ref[...]) * nw_ref[...] * (1.0 + mod_ref[0, 1:2, :]) + mod_ref[0, 0:1, :]
    hb = h.astype(BF16)
    proj_ref[...] = _dot(hb, w_ref[...]).astype(proj_ref.dtype)
    gate_ref[...] = _dot(hb, wg_ref[...])


def in_projection(x2d, mod, norm_w, w_main, w_gate, seq):
    m, d = x2d.shape
    tm = ROW_TILE
    per_seq = seq // tm
    return pl.pallas_call(
        _in_proj_kernel,
        grid=(m // tm,),
        in_specs=[pl.BlockSpec((tm, d), lambda i: (i, 0)),
                  pl.BlockSpec((1, 6, d), lambda i: (i // per_seq, 0, 0)),
                  pl.BlockSpec((1, d), lambda i: (0, 0)),
                  pl.BlockSpec((d, PROJ_MAIN), lambda i: (0, 0)),
                  pl.BlockSpec((d, LANES), lambda i: (0, 0))],
        out_specs=[pl.BlockSpec((tm, PROJ_MAIN), lambda i: (i, 0)),
                   pl.BlockSpec((tm, LANES), lambda i: (i, 0))],
        out_shape=[jax.ShapeDtypeStruct((m, PROJ_MAIN), BF16),
                   jax.ShapeDtypeStruct((m, LANES), F32)],
        compiler_params=_cparams("parallel"),
        name="in_projection",
    )(x2d, mod, norm_w, w_main, w_gate)


def _s5_scan(x_ref, h0_re, h0_im, a_re, a_im, steps, reverse):
    def body(s, carry):
        hr, hi = carry
        t = (steps - 1 - s) if reverse else s
        rows = pl.ds(pl.multiple_of(t * SUBLANES, SUBLANES), SUBLANES)
        xr = x_ref[rows, 0:S5_LANES]
        xi = x_ref[rows, S5_LANES:2 * S5_LANES]
        nr = a_re * hr - a_im * hi + xr
        ni = a_re * hi + a_im * hr + xi
        x_ref[rows, 0:S5_LANES] = nr
        x_ref[rows, S5_LANES:2 * S5_LANES] = ni
        return nr, ni
    return lax.fori_loop(0, steps, body, (h0_re, h0_im), unroll=4)


def _s5_direction(u_ref, a_ref, bmat_ref, cmat_ref, x_ref, st_ref, reverse):
    tc, nb, w = u_ref.shape
    first = pl.program_id(0) == 0

    @pl.when(first)
    def _():
        st_ref[...] = jnp.zeros_like(st_ref)

    u = u_ref[...].reshape(tc * nb, w)
    x_ref[...] = _dot(u.astype(BF16), bmat_ref[...])
    a_re = jnp.broadcast_to(a_ref[0:1, :], (nb, S5_LANES))
    a_im = jnp.broadcast_to(a_ref[1:2, :], (nb, S5_LANES))
    hr, hi = _s5_scan(x_ref, st_ref[0], st_ref[1], a_re, a_im, tc, reverse)
    st_ref[0] = hr
    st_ref[1] = hi
    return u, _dot(x_ref[...].astype(BF16), cmat_ref[...])


def _s5_fwd_kernel(u_ref, a_ref, bmat_ref, cmat_ref, y_ref, x_ref, st_ref):
    _, y = _s5_direction(u_ref, a_ref, bmat_ref, cmat_ref, x_ref, st_ref, reverse=False)
    y_ref[...] = y.reshape(y_ref.shape)


def _s5_bwd_kernel(u_ref, yf_ref, a_ref, bmat_ref, cmat_ref, d_ref, wglu_ref, bglu_ref, o_ref, x_ref, st_ref):
    u, y = _s5_direction(u_ref, a_ref, bmat_ref, cmat_ref, x_ref, st_ref, reverse=True)
    y = y + u * d_ref[...] + yf_ref[...].reshape(u.shape)
    y = _gelu_tanh(y)
    out = y * jax.nn.sigmoid(_dot(y.astype(BF16), wglu_ref[...]) + bglu_ref[...])
    o_ref[...] = out.reshape(o_ref.shape)


def s5_mixer(u_t, a_f, bmat_f, cmat_f, a_b, bmat_b, cmat_b, d_skip, w_glu, b_glu):
    seq, nb, w = u_t.shape
    assert nb == SUBLANES, "the scan keeps one batch row per sublane"
    tc = S5_TIME_TILE
    n = seq // tc
    rows = tc * nb
    const2 = lambda i: (0, 0)
    par_specs = [pl.BlockSpec((2, S5_LANES), const2),
                 pl.BlockSpec((w, 2 * S5_LANES), const2),
                 pl.BlockSpec((2 * S5_LANES, w), const2)]
    scratch = [pltpu.VMEM((rows, 2 * S5_LANES), F32), pltpu.VMEM((2, nb, S5_LANES), F32)]
    y_f = pl.pallas_call(
        _s5_fwd_kernel,
        grid=(n,),
        in_specs=[pl.BlockSpec((tc, nb, w), lambda i: (i, 0, 0))] + par_specs,
        out_specs=pl.BlockSpec((tc, nb, w), lambda i: (i, 0, 0)),
        out_shape=jax.ShapeDtypeStruct((seq, nb, w), F32),
        scratch_shapes=scratch,
        compiler_params=_cparams("arbitrary"),
        name="s5_forward",
    )(u_t, a_f, bmat_f, cmat_f)
    rev = lambda i: (n - 1 - i, 0, 0)
    return pl.pallas_call(
        _s5_bwd_kernel,
        grid=(n,),
        in_specs=[pl.BlockSpec((tc, nb, w), rev), pl.BlockSpec((tc, nb, w), rev)] + par_specs
                 + [pl.BlockSpec((1, w), const2), pl.BlockSpec((w, w), const2), pl.BlockSpec((1, w), const2)],
        out_specs=pl.BlockSpec((tc, nb, w), rev),
        out_shape=jax.ShapeDtypeStruct((seq, nb, w), F32),
        scratch_shapes=scratch,
        compiler_params=_cparams("arbitrary"),
        name="s5_backward_glu",
    )(u_t, y_f, a_b, bmat_b, cmat_b, d_skip, w_glu, b_glu)


def s5_discretise(lam_re, lam_im, log_step, b_re, b_im, c_re, c_im):
    g, p, h = S5_GROUPS, S5_STATE, S5_GROUP
    lr, li = lam_re.astype(F32), lam_im.astype(F32)
    dt = jnp.exp(log_step.astype(F32))[:, None]
    mag = jnp.exp(lr * dt)
    abr, abi = mag * jnp.cos(li * dt), mag * jnp.sin(li * dt)
    den = lr * lr + li * li
    cr = ((abr - 1.0) * lr + abi * li) / den
    ci = (abi * lr - (abr - 1.0) * li) / den
    bbr = cr[..., None] * b_re - ci[..., None] * b_im
    bbi = cr[..., None] * b_im + ci[..., None] * b_re
    eye = jnp.eye(g, dtype=F32)
    to_b = lambda t: jnp.einsum('gph,gk->ghkp', t, eye).reshape(g * h, g * p)
    bmat = jnp.concatenate([to_b(bbr), to_b(bbi)], axis=1)
    to_c = lambda t: jnp.einsum('ghp,gk->gpkh', t, eye).reshape(g * p, g * h)
    cmat = jnp.concatenate([to_c(c_re.astype(F32)), -to_c(c_im.astype(F32))], axis=0)
    a = jnp.stack([abr.reshape(g * p), abi.reshape(g * p)])
    return a, bmat.astype(BF16), cmat.astype(BF16)


def _conv3_rows(x, prev_row, next_row, w_ref):
    tm = x.shape[0]
    rows = lax.broadcasted_iota(jnp.int32, x.shape, 0)
    xp = jnp.where(rows == 0, prev_row, pltpu.roll(x, 1, 0))
    xn = jnp.where(rows == tm - 1, next_row, pltpu.roll(x, tm - 1, 0))
    return xp * w_ref[0:1, :] + x * w_ref[1:2, :] + xn * w_ref[2:3, :]


HALO_ROWS = 16


def _halo_specs(tm, width, col_block, n_rows):
    per = tm // HALO_ROWS
    last = n_rows // HALO_ROWS - 1
    prev = pl.BlockSpec((HALO_ROWS, width), lambda i, *_: (jnp.maximum(i * per - 1, 0), col_block(*_)))
    nxt = pl.BlockSpec((HALO_ROWS, width), lambda i, *_: (jnp.minimum((i + 1) * per, last), col_block(*_)))
    return prev, nxt


def _gdn_prep_kernel(x_ref, xp_ref, xn_ref, w_ref, o_ref, *, seq):
    tm = x_ref.shape[0]
    j = pl.program_id(1)
    t0 = (pl.program_id(0) * tm) % seq
    prev_row = jnp.where(t0 == 0, 0.0, xp_ref[HALO_ROWS - 1:HALO_ROWS, :].astype(F32))
    next_row = jnp.where(t0 + tm == seq, 0.0, xn_ref[0:1, :].astype(F32))
    y = _silu(_conv3_rows(x_ref[...].astype(F32), prev_row, next_row, w_ref))

    @pl.when(j == 2)
    def _():
        o_ref[...] = y.astype(o_ref.dtype)

    @pl.when(j < 2)
    def _():
        scale = jnp.where(j == 0, GDN_HEAD_DIM ** -0.5, 1.0)
        for h in range(GDN_HEADS):
            cols = slice(h * GDN_HEAD_DIM, (h + 1) * GDN_HEAD_DIM)
            yh = y[:, cols]
            yh = yh * (lax.rsqrt(jnp.sum(yh * yh, axis=-1, keepdims=True) + EPS) * scale)
            o_ref[:, cols] = yh.astype(o_ref.dtype)


def gdn_prepare(proj, conv_w, seq):
    m = proj.shape[0]
    tm = ROW_TILE
    first = (S5_WIDTH + 3 * CONV_WIDTH) // GDN_WIDTH
    prev, nxt = _halo_specs(tm, GDN_WIDTH, lambda j: first + j, m)
    return pl.pallas_call(
        functools.partial(_gdn_prep_kernel, seq=seq),
        grid=(m // tm, 3),
        in_specs=[pl.BlockSpec((tm, GDN_WIDTH), lambda i, j: (i, first + j)), prev, nxt,
                  pl.BlockSpec((3, GDN_WIDTH), lambda i, j: (0, j))],
        out_specs=pl.BlockSpec((tm, GDN_WIDTH), lambda i, j: (i, j)),
        out_shape=jax.ShapeDtypeStruct((m, 3 * GDN_WIDTH), BF16),
        compiler_params=_cparams("parallel", "parallel"),
        name="gdn_prepare",
    )(proj, proj, proj, conv_w)


def _gdn_gates_kernel(pre_ref, alog_ref, dtb_ref, o_ref):
    seq = pre_ref.shape[2]
    nh2 = 2 * GDN_HEADS
    g = -jnp.exp(alog_ref[...]) * _softplus(pre_ref[0, 0:nh2, :] + dtb_ref[...])
    o_ref[0, nh2:2 * nh2, :] = jax.nn.sigmoid(pre_ref[0, nh2:2 * nh2, :])
    s = lax.broadcasted_iota(jnp.int32, (LANES, LANES), 0)
    t = lax.broadcasted_iota(jnp.int32, (LANES, LANES), 1)
    same = (s // GDN_CHUNK) == (t // GDN_CHUNK)
    m_f = jnp.where(same & (s <= t), 1.0, 0.0)
    m_b = jnp.where(same & (s >= t), 1.0, 0.0)
    for jb in range(seq // LANES):
        cols = slice(jb * LANES, (jb + 1) * LANES)
        gb = g[:, cols]
        o_ref[0, 0:GDN_HEADS, cols] = _dot(gb, m_f, precision=HIGHEST)[0:GDN_HEADS]
        o_ref[0, GDN_HEADS:nh2, cols] = _dot(gb, m_b, precision=HIGHEST)[GDN_HEADS:nh2]


def gdn_gates(pre_t, a_log, dt_bias):
    nb, rows, seq = pre_t.shape
    col = lambda t: t.astype(F32).reshape(2 * GDN_HEADS, 1)
    return pl.pallas_call(
        _gdn_gates_kernel,
        grid=(nb,),
        in_specs=[pl.BlockSpec((1, rows, seq), lambda b: (b, 0, 0)),
                  pl.BlockSpec((2 * GDN_HEADS, 1), lambda b: (0, 0)),
                  pl.BlockSpec((2 * GDN_HEADS, 1), lambda b: (0, 0))],
        out_specs=pl.BlockSpec((1, rows, seq), lambda b: (b, 0, 0)),
        out_shape=jax.ShapeDtypeStruct((nb, rows, seq), F32),
        compiler_params=_cparams("parallel"),
        name="gdn_gates",
    )(pre_t, col(a_log), col(dt_bias))


def _nilpotent_inverses(mats, nilpotency):
    n = mats[0].shape[0]
    eye = jnp.where(lax.broadcasted_iota(jnp.int32, (n, n), 0) == lax.broadcasted_iota(jnp.int32, (n, n), 1), 1.0, 0.0)
    ps = [eye - a for a in mats]
    a16 = [a.astype(BF16) for a in mats]
    sqs = [_dot(a, a) for a in a16]
    k = 2
    while True:
        sq16 = [s.astype(BF16) for s in sqs]
        ps = [p + _dot(p.astype(BF16), s) for p, s in zip(ps, sq16)]
        k *= 2
        if k >= nilpotency:
            return ps
        sqs = [_dot(s, s) for s in sq16]


def _gdn_local_kernel(q_ref, k_ref, v_ref, gcol_ref, grow_ref, uwqf_ref, uwqb_ref, ktf_ref, ktb_ref, in_ref, eg_ref):
    r = q_ref.shape[1]
    c = GDN_CHUNK
    hd = GDN_HEAD_DIM
    ii = lax.broadcasted_iota(jnp.int32, (r, r), 0)
    jj = lax.broadcasted_iota(jnp.int32, (r, r), 1)
    same = (ii >= c) == (jj >= c)
    incl = (same & (ii >= jj), same & (ii <= jj))
    strict = (same & (ii > jj), same & (ii < jj))
    first_chunk = lax.broadcasted_iota(jnp.int32, (r, 1), 0) < c
    zeros = jnp.zeros((r, r), F32)
    big_a, rhs, keep = [], [], []
    for h in range(GDN_HEADS):
        cols = slice(h * hd, (h + 1) * hd)
        q16, k16 = q_ref[0, :, cols], k_ref[0, :, cols]
        q, k, v = q16.astype(F32), k16.astype(F32), v_ref[0, :, cols].astype(F32)
        kk = _dot_nt(k16, k16)
        qk = _dot_nt(q16, k16)
        a_dir, rhs_dir = [], []
        for d in range(2):
            lane = d * GDN_HEADS + h
            gc_col = gcol_ref[0, :, lane:lane + 1]
            gc_row = grow_ref[0, lane:lane + 1, :]
            beta = gcol_ref[0, :, 2 * GDN_HEADS + lane:2 * GDN_HEADS + lane + 1]
            decay = jnp.exp(jnp.where(incl[d], gc_col - gc_row, -1e30))
            a_dir.append(jnp.where(strict[d], kk * beta * decay, 0.0))
            intra = jnp.where(incl[d], qk * decay, 0.0)
            ends = (gc_col[c - 1:c], gc_col[2 * c - 1:2 * c]) if d == 0 else (gc_col[0:1], gc_col[c:c + 1])
            g_last = jnp.where(first_chunk, ends[0], ends[1])
            eg = jnp.exp(gc_col)
            kb = k * beta
            rhs_dir.append(jnp.concatenate([v * beta, kb * eg], axis=1))
            keep.append((q * eg, (k * jnp.exp(g_last - gc_col)).T, intra[:, :c] + intra[:, c:]))
            for ch in range(2):
                eg_ref[0, ch, lane:lane + 1, :] = jnp.broadcast_to(jnp.exp(ends[ch]), (1, LANES))
        big_a.append(jnp.concatenate([jnp.concatenate([a_dir[0], zeros], axis=1),
                                      jnp.concatenate([zeros, a_dir[1]], axis=1)], axis=0))
        rhs.append(jnp.concatenate(rhs_dir, axis=0))
    t_inv = _nilpotent_inverses(big_a, c)
    for h in range(GDN_HEADS):
        uw = _dot(t_inv[h].astype(BF16), rhs[h].astype(BF16))
        (qd_f, kt_f, in_f), (qd_b, kt_b, in_b) = keep[2 * h], keep[2 * h + 1]
        uwqf_ref[0, h] = jnp.concatenate([uw[:r], qd_f], axis=1).astype(BF16)
        uwqb_ref[0, h] = jnp.concatenate([uw[r:], qd_b], axis=1).astype(BF16)
        ktf_ref[0, h] = kt_f.astype(BF16)
        ktb_ref[0, h] = kt_b.astype(BF16)
        in_ref[0, h] = jnp.concatenate([in_f, in_b], axis=1).astype(BF16)


def gdn_local(qkv, gates_col, gates_row):
    nb, seq, _ = qkv.shape
    r = 2 * GDN_CHUNK
    hd, nh = GDN_HEAD_DIM, GDN_HEADS
    uwq_sd = jax.ShapeDtypeStruct((nb, nh, seq, 3 * hd), BF16)
    kt_sd = jax.ShapeDtypeStruct((nb, nh, hd, seq), BF16)
    uwq_spec = pl.BlockSpec((1, nh, r, 3 * hd), lambda b, i: (b, 0, i, 0))
    kt_spec = pl.BlockSpec((1, nh, hd, r), lambda b, i: (b, 0, 0, i))
    return pl.pallas_call(
        _gdn_local_kernel,
        grid=(nb, seq // r),
        in_specs=[pl.BlockSpec((1, r, GDN_WIDTH), lambda b, i: (b, i, 0)),
                  pl.BlockSpec((1, r, GDN_WIDTH), lambda b, i: (b, i, 1)),
                  pl.BlockSpec((1, r, GDN_WIDTH), lambda b, i: (b, i, 2)),
                  pl.BlockSpec((1, r, 4 * nh), lambda b, i: (b, i, 0)),
                  pl.BlockSpec((1, 4 * nh, r), lambda b, i: (b, 0, i))],
        out_specs=[uwq_spec, uwq_spec, kt_spec, kt_spec,
                   pl.BlockSpec((1, nh, r, 2 * GDN_CHUNK), lambda b, i: (b, 0, i, 0)),
                   pl.BlockSpec((1, 2, SUBLANES, LANES), lambda b, i: (b, i, 0, 0))],
        out_shape=[uwq_sd, uwq_sd, kt_sd, kt_sd,
                   jax.ShapeDtypeStruct((nb, nh, seq, 2 * GDN_CHUNK), BF16),
                   jax.ShapeDtypeStruct((nb, seq // GDN_CHUNK, SUBLANES, LANES), F32)],
        compiler_params=_cparams("parallel", "parallel"),
        name="gdn_local",
    )(qkv, qkv, qkv, gates_col, gates_row)


def _gdn_state_kernel(uwqf, ktf, inf, egf, uwqb, ktb, inb, egb, of_ref, ob_ref, st_ref):
    @pl.when(pl.program_id(1) == 0)
    def _():
        st_ref[...] = jnp.zeros_like(st_ref)

    c = GDN_CHUNK
    hd = GDN_HEAD_DIM
    nc = uwqf.shape[2] // c
    chains = [(d, h) for d in range(2) for h in range(GDN_HEADS)]
    refs = ((uwqf, ktf, inf, egf, of_ref), (uwqb, ktb, inb, egb, ob_ref))
    states = [st_ref[j] for j in range(len(chains))]
    for step in range(nc):
        sws = []
        for j, (d, h) in enumerate(chains):
            ci = step if d == 0 else nc - 1 - step
            rows = slice(ci * c, (ci + 1) * c)
            uwq = refs[d][0]
            wq = jnp.concatenate([uwq[0, h, rows, hd:2 * hd], uwq[0, h, rows, 2 * hd:3 * hd]], axis=0)
            sws.append(_dot(wq, states[j].astype(BF16)))
        for j, (d, h) in enumerate(chains):
            ci = step if d == 0 else nc - 1 - step
            rows = slice(ci * c, (ci + 1) * c)
            uwq, kt, intra, eg, o_ref = refs[d]
            sw = sws[j]
            v16 = (uwq[0, h, rows, 0:hd].astype(F32) - sw[:c]).astype(BF16)
            out = sw[c:] + _dot(intra[0, h, rows, d * c:(d + 1) * c], v16)
            o_ref[0, rows, h * hd:(h + 1) * hd] = out.astype(o_ref.dtype)
            lane = d * GDN_HEADS + h
            states[j] = states[j] * eg[0, ci, lane:lane + 1, :] + _dot(kt[0, h, :, rows], v16)
    for j in range(len(chains)):
        st_ref[j] = states[j]


def gdn_state_scan(uwq_f, uwq_b, kt_f, kt_b, intra, eg):
    nb, nh, seq, _ = uwq_f.shape
    hd = GDN_HEAD_DIM
    nc = GDN_STATE_CHUNKS
    r = nc * GDN_CHUNK
    n = seq // r

    def specs(pos):
        return [pl.BlockSpec((1, nh, r, 3 * hd), lambda b, i: (b, 0, pos(i), 0)),
                pl.BlockSpec((1, nh, hd, r), lambda b, i: (b, 0, 0, pos(i))),
                pl.BlockSpec((1, nh, r, 2 * GDN_CHUNK), lambda b, i: (b, 0, pos(i), 0)),
                pl.BlockSpec((1, nc, SUBLANES, LANES), lambda b, i: (b, pos(i), 0, 0))]

    fwd = lambda i: i
    bwd = lambda i: n - 1 - i
    out_sd = jax.ShapeDtypeStruct((nb, seq, GDN_WIDTH), BF16)
    return pl.pallas_call(
        _gdn_state_kernel,
        grid=(nb, n),
        in_specs=specs(fwd) + specs(bwd),
        out_specs=[pl.BlockSpec((1, r, GDN_WIDTH), lambda b, i: (b, i, 0)),
                   pl.BlockSpec((1, r, GDN_WIDTH), lambda b, i: (b, n - 1 - i, 0))],
        out_shape=[out_sd, out_sd],
        scratch_shapes=[pltpu.VMEM((2 * nh, hd, hd), F32)],
        compiler_params=_cparams("parallel", "arbitrary"),
        name="gdn_state_scan",
    )(uwq_f, kt_f, intra, eg, uwq_b, kt_b, intra, eg)


def _out_proj_kernel(x_ref, mod_ref, ys5_ref, cx_ref, cb_ref, cc_ref, cxp_ref, cxn_ref, ccp_ref, ccn_ref,
                     cw_ref, of_ref, ob_ref, z_ref, gnw_ref, w_ref, o_ref, *, seq):
    tm = x_ref.shape[0]
    t0 = (pl.program_id(0) * tm) % seq
    f32 = lambda ref, rows=slice(None): ref[rows, :].astype(F32)
    last = slice(HALO_ROWS - 1, HALO_ROWS)
    prev_row = jnp.where(t0 == 0, 0.0, f32(ccp_ref, last) * f32(cxp_ref, last))
    next_row = jnp.where(t0 + tm == seq, 0.0, f32(ccn_ref, slice(0, 1)) * f32(cxn_ref, slice(0, 1)))
    y_conv = f32(cb_ref) * _conv3_rows(f32(cc_ref) * f32(cx_ref), prev_row, next_row, cw_ref)
    o = f32(of_ref) + f32(ob_ref)
    z = f32(z_ref)
    heads = []
    for h in range(GDN_HEADS):
        cols = slice(h * GDN_HEAD_DIM, (h + 1) * GDN_HEAD_DIM)
        heads.append(_rms(o[:, cols]) * gnw_ref[...] * _silu(z[:, cols]))
    y_gdn = jnp.concatenate(heads, axis=1)
    c0, c1 = S5_WIDTH, S5_WIDTH + CONV_WIDTH
    mix = (_dot(ys5_ref[...].astype(BF16), w_ref[0:c0, :])
           + _dot(y_conv.astype(BF16), w_ref[c0:c1, :])
           + _dot(y_gdn.astype(BF16), w_ref[c1:, :]))
    o_ref[...] = x_ref[...] + mod_ref[0, 2:3, :] * mix


def out_projection(x2d, mod, y_s5, proj, conv_w, o_f, o_b, gdn_norm_w, w_out, seq):
    m, d = x2d.shape
    tm = ROW_TILE
    per_seq = seq // tm
    cw = CONV_WIDTH
    row = lambda c: pl.BlockSpec((tm, cw), lambda i: (i, c))
    cxp, cxn = _halo_specs(tm, cw, lambda: 1, m)
    ccp, ccn = _halo_specs(tm, cw, lambda: 3, m)
    gw = GDN_WIDTH
    return pl.pallas_call(
        functools.partial(_out_proj_kernel, seq=seq),
        grid=(m // tm,),
        in_specs=[pl.BlockSpec((tm, d), lambda i: (i, 0)),
                  pl.BlockSpec((1, 6, d), lambda i: (i // per_seq, 0, 0)),
                  pl.BlockSpec((tm, cw), lambda i: (i, 0)),
                  row(1), row(2), row(3), cxp, cxn, ccp, ccn,
                  pl.BlockSpec((3, cw), lambda i: (0, 0)),
                  pl.BlockSpec((tm, gw), lambda i: (i, 0)),
                  pl.BlockSpec((tm, gw), lambda i: (i, 0)),
                  pl.BlockSpec((tm, gw), lambda i: (i, (PROJ_MAIN - gw) // gw)),
                  pl.BlockSpec((1, GDN_HEAD_DIM), lambda i: (0, 0)),
                  pl.BlockSpec((d, d), lambda i: (0, 0))],
        out_specs=pl.BlockSpec((tm, d), lambda i: (i, 0)),
        out_shape=jax.ShapeDtypeStruct((m, d), F32),
        compiler_params=_cparams("parallel"),
        name="out_projection",
    )(x2d, mod, y_s5, proj, proj, proj, proj, proj, proj, proj, conv_w, o_f, o_b, proj, gdn_norm_w, w_out)


def _first_argmax(vals, rows, n):
    m = jnp.max(vals, axis=0, keepdims=True)
    return m, jnp.min(jnp.where(vals == m, rows, n), axis=0, keepdims=True)


def _route_kernel(x_ref, mod_ref, nw_ref, wr_ref, br_ref, h_ref, idx_ref, wgt_ref, cnt_ref, carry_ref):
    step = pl.program_id(0)

    @pl.when(step == 0)
    def _():
        carry_ref[...] = jnp.zeros_like(carry_ref)

    tm = x_ref.shape[0]
    h = _rms(x_ref[...]) * nw_ref[...] * (1.0 + mod_ref[0, 4:5, :]) + mod_ref[0, 3:4, :]
    for s in range(SUBLANES):
        h_ref[:, s, :] = h[:, s * LANES:(s + 1) * LANES]
    logits = _dot_nt(wr_ref[...], h, precision=HIGHEST) + br_ref[...]
    lg = logits[N_EXPERTS:N_EXPERTS + MOE_GROUPS, :]
    rows_g = lax.broadcasted_iota(jnp.int32, lg.shape, 0)
    g_max, g_idx = _first_argmax(lg, rows_g, MOE_GROUPS)
    g_w = 1.0 / jnp.sum(jnp.exp(lg - g_max), axis=0, keepdims=True)
    le = jnp.zeros((EXPERTS_PER_GROUP, tm), F32)
    for g in range(MOE_GROUPS):
        le = jnp.where(g_idx == g, logits[g * EXPERTS_PER_GROUP:(g + 1) * EXPERTS_PER_GROUP, :], le)
    rows_e = lax.broadcasted_iota(jnp.int32, le.shape, 0)
    m1, i1 = _first_argmax(le, rows_e, EXPERTS_PER_GROUP)
    m2, i2 = _first_argmax(jnp.where(rows_e == i1, -jnp.inf, le), rows_e, EXPERTS_PER_GROUP)
    r = jnp.exp(m2 - m1)
    w1 = g_w / (1.0 + r)
    e1 = g_idx * EXPERTS_PER_GROUP + i1
    e2 = g_idx * EXPERTS_PER_GROUP + i2
    rows_x = lax.broadcasted_iota(jnp.int32, (N_EXPERTS, tm), 0)
    hit1 = rows_x == e1
    hit2 = rows_x == e2
    onehot = jnp.where(hit1 | hit2, 1.0, 0.0)
    before = lax.broadcasted_iota(jnp.int32, (tm, tm), 0) < lax.broadcasted_iota(jnp.int32, (tm, tm), 1)
    ahead = _dot(onehot.astype(BF16), jnp.where(before, 1.0, 0.0).astype(BF16)) + carry_ref[:, 0:1]
    rank1 = jnp.sum(jnp.where(hit1, ahead, 0.0), axis=0, keepdims=True)
    rank2 = jnp.sum(jnp.where(hit2, ahead, 0.0), axis=0, keepdims=True)
    idx_ref[0:1, :] = e1
    idx_ref[1:2, :] = e2
    idx_ref[2:3, :] = rank1.astype(jnp.int32)
    idx_ref[3:4, :] = rank2.astype(jnp.int32)
    wgt_ref[0:1, :] = w1
    wgt_ref[1:2, :] = w1 * r
    carry_ref[...] = carry_ref[...] + jnp.sum(onehot, axis=1, keepdims=True)
    cnt_ref[...] = carry_ref[...]


def moe_route(x2d, mod, norm_w, w_router_t, b_router, seq):
    m, d = x2d.shape
    tm = ROUTE_TILE
    per_seq = seq // tm
    nr = w_router_t.shape[0]
    return pl.pallas_call(
        _route_kernel,
        grid=(m // tm,),
        in_specs=[pl.BlockSpec((tm, d), lambda i: (i, 0)),
                  pl.BlockSpec((1, 6, d), lambda i: (i // per_seq, 0, 0)),
                  pl.BlockSpec((1, d), lambda i: (0, 0)),
                  pl.BlockSpec((nr, d), lambda i: (0, 0)),
                  pl.BlockSpec((nr, 1), lambda i: (0, 0))],
        out_specs=[pl.BlockSpec((tm,) + SLAB, lambda i: (i, 0, 0)),
                   pl.BlockSpec((4, tm), lambda i: (0, i)),
                   pl.BlockSpec((2, tm), lambda i: (0, i)),
                   pl.BlockSpec((N_EXPERTS, LANES), lambda i: (0, 0))],
        out_shape=[jax.ShapeDtypeStruct((m,) + SLAB, F32),
                   jax.ShapeDtypeStruct((4, m), jnp.int32),
                   jax.ShapeDtypeStruct((2, m), F32),
                   jax.ShapeDtypeStruct((N_EXPERTS, LANES), F32)],
        scratch_shapes=[pltpu.VMEM((N_EXPERTS, LANES), F32)],
        compiler_params=_cparams("arbitrary"),
        name="moe_route",
    )(x2d, mod, norm_w, w_router_t, b_router)


def _row_copy(src_ref, src_row, dst_ref, dst_row, sem):
    return pltpu.make_async_copy(src_ref.at[pl.ds(src_row, 1)], dst_ref.at[pl.ds(dst_row, 1)], sem)


def _scatter_kernel(last_ref, d1_ref, d2_ref, h_ref, xs_ref, zero_ref, sem):
    tm = h_ref.shape[0]

    @pl.when(pl.program_id(0) == 0)
    def _():
        zero_ref[...] = jnp.zeros_like(zero_ref)

        def fill(e, c):
            @pl.when(last_ref[e] >= 0)
            def _():
                pltpu.make_async_copy(zero_ref, xs_ref.at[pl.ds(last_ref[e], MOE_BLOCK)], sem).start()
            return c

        def drain(e, c):
            @pl.when(last_ref[e] >= 0)
            def _():
                pltpu.make_async_copy(zero_ref, xs_ref.at[pl.ds(0, MOE_BLOCK)], sem).wait()
            return c

        lax.fori_loop(0, N_EXPERTS, fill, 0)
        lax.fori_loop(0, N_EXPERTS, drain, 0)

        n_blocks = xs_ref.shape[0] // MOE_BLOCK

        def fill_unused(b, c):
            pltpu.make_async_copy(zero_ref, xs_ref.at[pl.ds(b * MOE_BLOCK, MOE_BLOCK)], sem).start()
            return c

        def drain_unused(b, c):
            pltpu.make_async_copy(zero_ref, xs_ref.at[pl.ds(0, MOE_BLOCK)], sem).wait()
            return c

        lax.fori_loop(last_ref[N_EXPERTS], n_blocks, fill_unused, 0)
        lax.fori_loop(last_ref[N_EXPERTS], n_blocks, drain_unused, 0)

    def start(r, c):
        _row_copy(h_ref, r, xs_ref, d1_ref[0, 0, r], sem).start(priority=0)
        _row_copy(h_ref, r, xs_ref, d2_ref[0, 0, r], sem).start(priority=1)
        return c

    def wait(r, c):
        _row_copy(h_ref, 0, xs_ref, 0, sem).wait()
        _row_copy(h_ref, 0, xs_ref, 0, sem).wait()
        return c

    lax.fori_loop(0, tm, start, 0, unroll=DMA_UNROLL)
    lax.fori_loop(0, tm, wait, 0, unroll=DMA_UNROLL)


def moe_scatter(h_slab, dest1, dest2, last_block_start, n_slots):
    m = h_slab.shape[0]
    tm = ROUTE_TILE
    idx_spec = pl.BlockSpec((1, 1, tm), lambda i, last: (i, 0, 0), memory_space=pltpu.SMEM)
    return pl.pallas_call(
        _scatter_kernel,
        grid_spec=pltpu.PrefetchScalarGridSpec(
            num_scalar_prefetch=1,
            grid=(m // tm,),
            in_specs=[idx_spec, idx_spec, pl.BlockSpec((tm,) + SLAB, lambda i, last: (i, 0, 0))],
            out_specs=pl.BlockSpec(memory_space=pl.ANY),
            scratch_shapes=[pltpu.VMEM((MOE_BLOCK,) + SLAB, F32), pltpu.SemaphoreType.DMA(())]),
        out_shape=jax.ShapeDtypeStruct((n_slots,) + SLAB, F32),
        compiler_params=_cparams("arbitrary"),
        name="moe_scatter",
    )(last_block_start, dest1.reshape(m // tm, 1, tm), dest2.reshape(m // tm, 1, tm), h_slab)


def _expert_kernel(be_ref, nused_ref, xs_ref, wgu_ref, wd_ref, y_ref, wgu16_ref, wd16_ref):
    i = pl.program_id(0)
    changed = jnp.logical_or(i == 0, be_ref[i] != be_ref[jnp.maximum(i - 1, 0)])

    @pl.when(changed)
    def _():
        wgu16_ref[...] = wgu_ref[0].astype(BF16)
        wd16_ref[...] = wd_ref[0].astype(BF16)

    @pl.when(i < nused_ref[0])
    def _():
        x = jnp.concatenate([xs_ref[:, s, :] for s in range(SUBLANES)], axis=1).astype(BF16)
        gu = _dot(x, wgu16_ref[...])
        act = _silu(gu[:, :EXPERT_FF]) * gu[:, EXPERT_FF:]
        y = _dot(act.astype(BF16), wd16_ref[...])
        for s in range(SUBLANES):
            y_ref[:, s, :] = y[:, s * LANES:(s + 1) * LANES]

    @pl.when(i >= nused_ref[0])
    def _():
        y_ref[...] = jnp.zeros_like(y_ref)


def moe_experts(xs, blk_expert, n_used, w_gate_up, w_down):
    n_slots = xs.shape[0]
    n_blocks = n_slots // MOE_BLOCK
    d = D_MODEL
    return pl.pallas_call(
        _expert_kernel,
        grid_spec=pltpu.PrefetchScalarGridSpec(
            num_scalar_prefetch=2,
            grid=(n_blocks,),
            in_specs=[pl.BlockSpec((MOE_BLOCK,) + SLAB, lambda i, be, nu: (jnp.minimum(i, nu[0] - 1), 0, 0)),
                      pl.BlockSpec((1, d, 2 * EXPERT_FF), lambda i, be, nu: (be[i], 0, 0)),
                      pl.BlockSpec((1, EXPERT_FF, d), lambda i, be, nu: (be[i], 0, 0))],
            out_specs=pl.BlockSpec((MOE_BLOCK,) + SLAB, lambda i, be, nu: (i, 0, 0)),
            scratch_shapes=[pltpu.VMEM((d, 2 * EXPERT_FF), BF16), pltpu.VMEM((EXPERT_FF, d), BF16)]),
        out_shape=jax.ShapeDtypeStruct((n_slots,) + SLAB, F32),
        compiler_params=_cparams("arbitrary"),
        name="moe_experts",
    )(blk_expert, n_used, xs, w_gate_up, w_down)


def _combine_kernel(d1_ref, d2_ref, x_ref, mod_ref, wgt_ref, fnw_ref, yb_ref, o_ref, y1_ref, y2_ref, sem, *, final):
    tm = x_ref.shape[0]

    def start(r, c):
        _row_copy(yb_ref, d1_ref[0, 0, r], y1_ref, r, sem).start(priority=0)
        _row_copy(yb_ref, d2_ref[0, 0, r], y2_ref, r, sem).start(priority=1)
        return c

    def wait(r, c):
        _row_copy(yb_ref, 0, y1_ref, 0, sem).wait()
        _row_copy(yb_ref, 0, y2_ref, 0, sem).wait()
        return c

    lax.fori_loop(0, tm, start, 0, unroll=DMA_UNROLL)
    lax.fori_loop(0, tm, wait, 0, unroll=DMA_UNROLL)
    w1 = wgt_ref[:, 0:1]
    w2 = wgt_ref[:, 1:2]
    for s in range(SUBLANES):
        cols = slice(s * LANES, (s + 1) * LANES)
        y = y1_ref[:, s, :] * w1 + y2_ref[:, s, :] * w2
        o_ref[:, cols] = x_ref[:, cols] + mod_ref[0, 5:6, cols] * y
    if final:
        o_ref[...] = _rms(o_ref[...]) * fnw_ref[...]


def moe_combine(x2d, mod, yb, dest1, dest2, weights, final_norm_w, seq, final):
    m, d = x2d.shape
    tm = COMBINE_TILE
    per_seq = seq // tm
    idx_spec = pl.BlockSpec((1, 1, tm), lambda i: (i, 0, 0), memory_space=pltpu.SMEM)
    return pl.pallas_call(
        functools.partial(_combine_kernel, final=final),
        grid=(m // tm,),
        in_specs=[idx_spec, idx_spec,
                  pl.BlockSpec((tm, d), lambda i: (i, 0)),
                  pl.BlockSpec((1, 6, d), lambda i: (i // per_seq, 0, 0)),
                  pl.BlockSpec((tm, 2), lambda i: (i, 0)),
                  pl.BlockSpec((1, d), lambda i: (0, 0)),
                  pl.BlockSpec(memory_space=pl.ANY)],
        out_specs=pl.BlockSpec((tm, d), lambda i: (i, 0)),
        out_shape=jax.ShapeDtypeStruct((m, d), F32),
        scratch_shapes=[pltpu.VMEM((tm,) + SLAB, F32), pltpu.VMEM((tm,) + SLAB, F32),
                        pltpu.SemaphoreType.DMA(())],
        compiler_params=_cparams("arbitrary"),
        name="moe_combine",
    )(dest1.reshape(m // tm, 1, tm), dest2.reshape(m // tm, 1, tm), x2d, mod, weights, final_norm_w, yb)


def hier_moe_layer(x2d, mod, norm_w, w_router_t, b_router, w_gate_up, w_down, final_norm_w, seq, final):
    m = x2d.shape[0]
    h_slab, idx, wgt, counts = moe_route(x2d, mod, norm_w, w_router_t, b_router, seq)
    counts = counts[:, 0].astype(jnp.int32)
    padded = (counts + MOE_BLOCK - 1) // MOE_BLOCK * MOE_BLOCK
    pad_end = jnp.cumsum(padded)
    pad_start = pad_end - padded
    dest1 = pad_start[idx[0]] + idx[2]
    dest2 = pad_start[idx[1]] + idx[3]
    n_blocks = 2 * m // MOE_BLOCK + N_EXPERTS
    blk_start = jnp.arange(n_blocks, dtype=jnp.int32) * MOE_BLOCK
    blk_expert = jnp.minimum(jnp.sum((pad_end[None, :] <= blk_start[:, None]).astype(jnp.int32), axis=1),
                             N_EXPERTS - 1)
    n_used = (pad_end[-1:] // MOE_BLOCK).astype(jnp.int32)
    fill_plan = jnp.concatenate([jnp.where(padded > 0, pad_end - MOE_BLOCK, -1), n_used]).astype(jnp.int32)
    xs = moe_scatter(h_slab, dest1, dest2, fill_plan, n_blocks * MOE_BLOCK)
    yb = moe_experts(xs, blk_expert, n_used, w_gate_up, w_down)
    return moe_combine(x2d, mod, yb, dest1, dest2, wgt.T, final_norm_w, seq, final)


def _layer_params(l, p):
    w_in = p['w_in'][l]
    w_gate = jnp.zeros((D_MODEL, LANES), F32).at[:, :N_GATE_COLS].set(w_in[:, PROJ_MAIN:])
    w_router_t = jnp.zeros((N_EXPERTS + SUBLANES, D_MODEL), F32)
    w_router_t = w_router_t.at[:N_EXPERTS].set(p['moe_w_expert'][l].T)
    w_router_t = w_router_t.at[N_EXPERTS:N_EXPERTS + MOE_GROUPS].set(p['moe_w_group'][l].T)
    b_router = jnp.zeros((N_EXPERTS + SUBLANES, 1), F32)
    b_router = b_router.at[:N_EXPERTS, 0].set(p['moe_b_expert'][l])
    b_router = b_router.at[N_EXPERTS:N_EXPERTS + MOE_GROUPS, 0].set(p['moe_b_group'][l])
    s5 = [s5_discretise(p['s5_lam_re'][l, d], p['s5_lam_im'][l, d], p['s5_log_step'][l, d],
                        p['s5_b_re'][l, d], p['s5_b_im'][l, d], p['s5_c_re'][l, d], p['s5_c_im'][l, d])
          for d in range(2)]
    return dict(
        norm_mix=p['norm_mix'][l][None, :], norm_ffn=p['norm_ffn'][l][None, :],
        w_main=w_in[:, :PROJ_MAIN].astype(BF16), w_gate=w_gate.astype(BF16),
        w_out=p['w_out'][l].astype(BF16), s5=s5,
        s5_d=p['s5_d'][l][None, :], s5_w_glu=p['s5_w_glu'][l].astype(BF16), s5_b_glu=p['s5_b_glu'][l][None, :],
        conv_w=p['conv_w'][l], gdn_conv_w=p['gdn_conv_w'][l], gdn_a_log=p['gdn_a_log'][l],
        gdn_dt_bias=p['gdn_dt_bias'][l], gdn_norm_w=p['gdn_norm_w'][l][None, :],
        w_router_t=w_router_t, b_router=b_router,
        w_gate_up=p['moe_w_gate_up'][l], w_down=p['moe_w_down'][l])


def _trunk(x, mods, layers, final_norm_w):
    nb, seq, d = x.shape
    m = nb * seq
    x2d = x.reshape(m, d)
    for l, lp in enumerate(layers):
        mod = mods[l].reshape(nb, 6, d)
        proj, gate_pre = in_projection(x2d, mod, lp['norm_mix'], lp['w_main'], lp['w_gate'], seq)
        u_t = proj[:, :S5_WIDTH].astype(F32).reshape(nb, seq, S5_WIDTH).transpose(1, 0, 2)
        (a_f, bm_f, cm_f), (a_b, bm_b, cm_b) = lp['s5']
        y_s5 = s5_mixer(u_t, a_f, bm_f, cm_f, a_b, bm_b, cm_b, lp['s5_d'], lp['s5_w_glu'], lp['s5_b_glu'])
        y_s5 = y_s5.astype(BF16).transpose(1, 0, 2).reshape(m, S5_WIDTH)
        qkv = gdn_prepare(proj, lp['gdn_conv_w'], seq).reshape(nb, seq, 3 * GDN_WIDTH)
        pre_t = gate_pre[:, :N_GATE_COLS].reshape(nb, seq, N_GATE_COLS).transpose(0, 2, 1)
        gates = gdn_gates(pre_t, lp['gdn_a_log'], lp['gdn_dt_bias'])
        uwq_f, uwq_b, kt_f, kt_b, intra, eg = gdn_local(qkv, gates.transpose(0, 2, 1), gates)
        o_f, o_b = gdn_state_scan(uwq_f, uwq_b, kt_f, kt_b, intra, eg)
        x2d = out_projection(x2d, mod, y_s5, proj, lp['conv_w'], o_f.reshape(m, GDN_WIDTH),
                             o_b.reshape(m, GDN_WIDTH), lp['gdn_norm_w'], lp['w_out'], seq)
        x2d = hier_moe_layer(x2d, mod, lp['norm_ffn'], lp['w_router_t'], lp['b_router'],
                             lp['w_gate_up'], lp['w_down'], final_norm_w, seq, final=(l == len(layers) - 1))
    return x2d.reshape(nb, seq, d)


def kernel(x_prompt, x_sample, c_prompt, c_sample, w_ada, b_ada, norm_mix, norm_ffn, w_in, w_out, s5_lam_re, s5_lam_im, s5_log_step, s5_b_re, s5_b_im, s5_c_re, s5_c_im, s5_d, s5_w_glu, s5_b_glu, conv_w, gdn_conv_w, gdn_a_log, gdn_dt_bias, gdn_norm_w, moe_w_group, moe_b_group, moe_w_expert, moe_b_expert, moe_w_gate_up, moe_w_down, final_norm):
    p = dict(norm_mix=norm_mix, norm_ffn=norm_ffn, w_in=w_in, w_out=w_out, s5_lam_re=s5_lam_re,
             s5_lam_im=s5_lam_im, s5_log_step=s5_log_step, s5_b_re=s5_b_re, s5_b_im=s5_b_im, s5_c_re=s5_c_re,
             s5_c_im=s5_c_im, s5_d=s5_d, s5_w_glu=s5_w_glu, s5_b_glu=s5_b_glu, conv_w=conv_w,
             gdn_conv_w=gdn_conv_w, gdn_a_log=gdn_a_log, gdn_dt_bias=gdn_dt_bias, gdn_norm_w=gdn_norm_w,
             moe_w_group=moe_w_group, moe_b_group=moe_b_group, moe_w_expert=moe_w_expert,
             moe_b_expert=moe_b_expert, moe_w_gate_up=moe_w_gate_up, moe_w_down=moe_w_down)
    depth = w_ada.shape[0]
    layers = [_layer_params(l, p) for l in range(depth)]
    nbp = c_prompt.shape[0]
    mods = ada_modulation(jnp.concatenate([c_prompt, c_sample], axis=0), w_ada, b_ada)
    fnw = final_norm[None, :]
    y_prompt = _trunk(x_prompt, mods[:, :nbp], layers, fnw)
    y_sample = _trunk(x_sample, mods[:, nbp:], layers, fnw)
    return (y_prompt, y_sample)
```

```python
import functools

import jax
import jax.numpy as jnp
from jax import lax
from jax.experimental import pallas as pl
from jax.experimental.pallas import tpu as pltpu

F32 = jnp.float32
BF16 = jnp.bfloat16
HIGHEST = lax.Precision.HIGHEST

D_MODEL = 1024
DEPTH = 4
S5_WIDTH = 256
S5_GROUP = 16
S5_GROUPS = 16
S5_STATE = 64
S5_LANES = S5_GROUPS * S5_STATE
CONV_WIDTH = 256
GDN_WIDTH = 512
GDN_HEADS = 4
GDN_HEAD_DIM = 128
GDN_CHUNK = 64
PROJ_MAIN = S5_WIDTH + 3 * CONV_WIDTH + 4 * GDN_WIDTH
N_GATE_COLS = 4 * GDN_HEADS
MOE_GROUPS = 4
EXPERTS_PER_GROUP = 8
N_EXPERTS = 32
EXPERT_FF = 512
MOE_BLOCK = 512
EPS = 1e-6

SUBLANES = 8
LANES = 128
SLAB = (SUBLANES, LANES)
VMEM_LIMIT = 56 * 1024 * 1024

ROW_TILE = 512
S5_TIME_TILE = 64
GDN_STATE_CHUNKS = 8
ROUTE_TILE = 512
COMBINE_TILE = 256
ROUTER_ROWS = N_EXPERTS + 16
DMA_UNROLL = 8


def _cparams(*sem):
    return pltpu.CompilerParams(dimension_semantics=sem, vmem_limit_bytes=VMEM_LIMIT)


def _dot(a, b, **kw):
    return jnp.dot(a, b, preferred_element_type=F32, **kw)


def _dot_nt(a, b, **kw):
    return lax.dot_general(a, b, (((1,), (1,)), ((), ())), preferred_element_type=F32, **kw)


def _dot_tn(a, b, **kw):
    return lax.dot_general(a, b, (((0,), (0,)), ((), ())), preferred_element_type=F32, **kw)


def _sigmoid(x):
    return 0.5 * jnp.tanh(0.5 * x) + 0.5


def _silu(x):
    return x * _sigmoid(x)


def _gelu_tanh(x):
    return 0.5 * x * (1.0 + jnp.tanh(0.7978845608028654 * (x + 0.044715 * (x * x * x))))


def _softplus(x):
    return jnp.maximum(x, 0.0) + jnp.log(1.0 + jnp.exp(-jnp.abs(x)))


def _rms(x):
    return x * lax.rsqrt(jnp.mean(x * x, axis=-1, keepdims=True) + EPS)


def _ada_kernel(c_ref, w_ref, b_ref, o_ref):
    o_ref[0] = _dot(_silu(c_ref[...]), w_ref[0], precision=HIGHEST) + b_ref[0]


def ada_modulation(c_all, w_ada, b_ada):
    nb = c_all.shape[0]
    depth, d, n = w_ada.shape
    tn = 1536
    return pl.pallas_call(
        _ada_kernel,
        grid=(depth, n // tn),
        in_specs=[pl.BlockSpec((nb, d), lambda l, j: (0, 0)),
                  pl.BlockSpec((1, d, tn), lambda l, j: (l, 0, j)),
                  pl.BlockSpec((1, 1, tn), lambda l, j: (l, 0, j))],
        out_specs=pl.BlockSpec((1, nb, tn), lambda l, j: (l, 0, j)),
        out_shape=jax.ShapeDtypeStruct((depth, nb, n), F32),
        compiler_params=_cparams("parallel", "parallel"),
        name="ada_modulation",
    )(c_all, w_ada, b_ada.reshape(depth, 1, n))


def _in_proj_kernel(x_ref, mod_ref, nw_ref, w_ref, wg_ref, proj_ref, gate_ref):
    h = _rms(x_ref[...]) * nw_ref[...] * (1.0 + mod_ref[0, 1:2, :]) + mod_ref[0, 0:1, :]
    hb = h.astype(BF16)
    proj_ref[...] = _dot(hb, w_ref[...]).astype(proj_ref.dtype)
    gate_ref[...] = _dot(hb, wg_ref[...])


def in_projection(x2d, mod, norm_w, w_main, w_gate, seq):
    m, d = x2d.shape
    tm = ROW_TILE
    per_seq = seq // tm
    return pl.pallas_call(
        _in_proj_kernel,
        grid=(m // tm,),
        in_specs=[pl.BlockSpec((tm, d), lambda i: (i, 0)),
                  pl.BlockSpec((1, 6, d), lambda i: (i // per_seq, 0, 0)),
                  pl.BlockSpec((1, d), lambda i: (0, 0)),
                  pl.BlockSpec((d, PROJ_MAIN), lambda i: (0, 0)),
                  pl.BlockSpec((d, LANES), lambda i: (0, 0))],
        out_specs=[pl.BlockSpec((tm, PROJ_MAIN), lambda i: (i, 0)),
                   pl.BlockSpec((tm, LANES), lambda i: (i, 0))],
        out_shape=[jax.ShapeDtypeStruct((m, PROJ_MAIN), BF16),
                   jax.ShapeDtypeStruct((m, LANES), F32)],
        compiler_params=_cparams("parallel"),
        name="in_projection",
    )(x2d, mod, norm_w, w_main, w_gate)


def _s5_scan(x_ref, h0_re, h0_im, a_re, a_im, steps, reverse):
    def body(s, carry):
        hr, hi = carry
        t = (steps - 1 - s) if reverse else s
        rows = pl.ds(pl.multiple_of(t * SUBLANES, SUBLANES), SUBLANES)
        xr = x_ref[rows, 0:S5_LANES]
        xi = x_ref[rows, S5_LANES:2 * S5_LANES]
        nr = a_re * hr - a_im * hi + xr
        ni = a_re * hi + a_im * hr + xi
        x_ref[rows, 0:S5_LANES] = nr
        x_ref[rows, S5_LANES:2 * S5_LANES] = ni
        return nr, ni
    return lax.fori_loop(0, steps, body, (h0_re, h0_im), unroll=4)


def _s5_direction(u_ref, a_ref, bmat_ref, cmat_ref, x_ref, st_ref, reverse):
    tc, nb, w = u_ref.shape
    first = pl.program_id(0) == 0

    @pl.when(first)
    def _():
        st_ref[...] = jnp.zeros_like(st_ref)

    u = u_ref[...].reshape(tc * nb, w)
    x_ref[...] = _dot(u.astype(BF16), bmat_ref[...])
    a_re = jnp.broadcast_to(a_ref[0:1, :], (nb, S5_LANES))
    a_im = jnp.broadcast_to(a_ref[1:2, :], (nb, S5_LANES))
    hr, hi = _s5_scan(x_ref, st_ref[0], st_ref[1], a_re, a_im, tc, reverse)
    st_ref[0] = hr
    st_ref[1] = hi
    return u, _dot(x_ref[...].astype(BF16), cmat_ref[...])


def _s5_fwd_kernel(u_ref, a_ref, bmat_ref, cmat_ref, y_ref, x_ref, st_ref):
    _, y = _s5_direction(u_ref, a_ref, bmat_ref, cmat_ref, x_ref, st_ref, reverse=False)
    y_ref[...] = y.reshape(y_ref.shape)


def _s5_bwd_kernel(u_ref, yf_ref, a_ref, bmat_ref, cmat_ref, d_ref, wglu_ref, bglu_ref, o_ref, x_ref, st_ref):
    u, y = _s5_direction(u_ref, a_ref, bmat_ref, cmat_ref, x_ref, st_ref, reverse=True)
    y = y + u * d_ref[...] + yf_ref[...].reshape(u.shape)
    y = _gelu_tanh(y)
    out = y * _sigmoid(_dot(y.astype(BF16), wglu_ref[...]) + bglu_ref[...])
    o_ref[...] = out.reshape(o_ref.shape)


def s5_mixer(u_t, a_f, bmat_f, cmat_f, a_b, bmat_b, cmat_b, d_skip, w_glu, b_glu):
    seq, nb, w = u_t.shape
    assert nb == SUBLANES, "the scan keeps one batch row per sublane"
    tc = S5_TIME_TILE
    n = seq // tc
    rows = tc * nb
    const2 = lambda i: (0, 0)
    par_specs = [pl.BlockSpec((2, S5_LANES), const2),
                 pl.BlockSpec((w, 2 * S5_LANES), const2),
                 pl.BlockSpec((2 * S5_LANES, w), const2)]
    scratch = [pltpu.VMEM((rows, 2 * S5_LANES), F32), pltpu.VMEM((2, nb, S5_LANES), F32)]
    y_f = pl.pallas_call(
        _s5_fwd_kernel,
        grid=(n,),
        in_specs=[pl.BlockSpec((tc, nb, w), lambda i: (i, 0, 0))] + par_specs,
        out_specs=pl.BlockSpec((tc, nb, w), lambda i: (i, 0, 0)),
        out_shape=jax.ShapeDtypeStruct((seq, nb, w), F32),
        scratch_shapes=scratch,
        compiler_params=_cparams("arbitrary"),
        name="s5_forward",
    )(u_t, a_f, bmat_f, cmat_f)
    rev = lambda i: (n - 1 - i, 0, 0)
    return pl.pallas_call(
        _s5_bwd_kernel,
        grid=(n,),
        in_specs=[pl.BlockSpec((tc, nb, w), rev), pl.BlockSpec((tc, nb, w), rev)] + par_specs
                 + [pl.BlockSpec((1, w), const2), pl.BlockSpec((w, w), const2), pl.BlockSpec((1, w), const2)],
        out_specs=pl.BlockSpec((tc, nb, w), rev),
        out_shape=jax.ShapeDtypeStruct((seq, nb, w), F32),
        scratch_shapes=scratch,
        compiler_params=_cparams("arbitrary"),
        name="s5_backward_glu",
    )(u_t, y_f, a_b, bmat_b, cmat_b, d_skip, w_glu, b_glu)


def s5_discretise(lam_re, lam_im, log_step, b_re, b_im, c_re, c_im):
    g, p, h = S5_GROUPS, S5_STATE, S5_GROUP
    lr, li = lam_re.astype(F32), lam_im.astype(F32)
    dt = jnp.exp(log_step.astype(F32))[:, None]
    mag = jnp.exp(lr * dt)
    abr, abi = mag * jnp.cos(li * dt), mag * jnp.sin(li * dt)
    den = lr * lr + li * li
    cr = ((abr - 1.0) * lr + abi * li) / den
    ci = (abi * lr - (abr - 1.0) * li) / den
    bbr = cr[..., None] * b_re - ci[..., None] * b_im
    bbi = cr[..., None] * b_im + ci[..., None] * b_re
    eye = jnp.eye(g, dtype=F32)
    to_b = lambda t: jnp.einsum('gph,gk->ghkp', t, eye).reshape(g * h, g * p)
    bmat = jnp.concatenate([to_b(bbr), to_b(bbi)], axis=1)
    to_c = lambda t: jnp.einsum('ghp,gk->gpkh', t, eye).reshape(g * p, g * h)
    cmat = jnp.concatenate([to_c(c_re.astype(F32)), -to_c(c_im.astype(F32))], axis=0)
    a = jnp.stack([abr.reshape(g * p), abi.reshape(g * p)])
    return a, bmat.astype(BF16), cmat.astype(BF16)


def _conv3_rows(x, prev_row, next_row, w_ref):
    tm = x.shape[0]
    rows = lax.broadcasted_iota(jnp.int32, x.shape, 0)
    xp = jnp.where(rows == 0, prev_row, pltpu.roll(x, 1, 0))
    xn = jnp.where(rows == tm - 1, next_row, pltpu.roll(x, tm - 1, 0))
    return xp * w_ref[0:1, :] + x * w_ref[1:2, :] + xn * w_ref[2:3, :]


HALO_ROWS = 16


def _halo_specs(tm, width, col_block, n_rows):
    per = tm // HALO_ROWS
    last = n_rows // HALO_ROWS - 1
    prev = pl.BlockSpec((HALO_ROWS, width), lambda i, *_: (jnp.maximum(i * per - 1, 0), col_block(*_)))
    nxt = pl.BlockSpec((HALO_ROWS, width), lambda i, *_: (jnp.minimum((i + 1) * per, last), col_block(*_)))
    return prev, nxt


def _gdn_prep_kernel(x_ref, xp_ref, xn_ref, w_ref, o_ref, *, seq):
    tm = x_ref.shape[0]
    j = pl.program_id(1)
    t0 = (pl.program_id(0) * tm) % seq
    prev_row = jnp.where(t0 == 0, 0.0, xp_ref[HALO_ROWS - 1:HALO_ROWS, :].astype(F32))
    next_row = jnp.where(t0 + tm == seq, 0.0, xn_ref[0:1, :].astype(F32))
    y = _silu(_conv3_rows(x_ref[...].astype(F32), prev_row, next_row, w_ref))

    @pl.when(j == 2)
    def _():
        o_ref[...] = y.astype(o_ref.dtype)

    @pl.when(j < 2)
    def _():
        scale = jnp.where(j == 0, GDN_HEAD_DIM ** -0.5, 1.0)
        for h in range(GDN_HEADS):
            cols = slice(h * GDN_HEAD_DIM, (h + 1) * GDN_HEAD_DIM)
            yh = y[:, cols]
            yh = yh * (lax.rsqrt(jnp.sum(yh * yh, axis=-1, keepdims=True) + EPS) * scale)
            o_ref[:, cols] = yh.astype(o_ref.dtype)


def gdn_prepare(proj, conv_w, seq):
    m = proj.shape[0]
    tm = ROW_TILE
    first = (S5_WIDTH + 3 * CONV_WIDTH) // GDN_WIDTH
    prev, nxt = _halo_specs(tm, GDN_WIDTH, lambda j: first + j, m)
    return pl.pallas_call(
        functools.partial(_gdn_prep_kernel, seq=seq),
        grid=(m // tm, 3),
        in_specs=[pl.BlockSpec((tm, GDN_WIDTH), lambda i, j: (i, first + j)), prev, nxt,
                  pl.BlockSpec((3, GDN_WIDTH), lambda i, j: (0, j))],
        out_specs=pl.BlockSpec((tm, GDN_WIDTH), lambda i, j: (i, j)),
        out_shape=jax.ShapeDtypeStruct((m, 3 * GDN_WIDTH), BF16),
        compiler_params=_cparams("parallel", "parallel"),
        name="gdn_prepare",
    )(proj, proj, proj, conv_w)


def _gdn_gates_kernel(pre_ref, alog_ref, dtb_ref, o_ref):
    seq = pre_ref.shape[2]
    nh2 = 2 * GDN_HEADS
    g = -jnp.exp(alog_ref[...]) * _softplus(pre_ref[0, 0:nh2, :] + dtb_ref[...])
    o_ref[0, nh2:2 * nh2, :] = _sigmoid(pre_ref[0, nh2:2 * nh2, :])
    s = lax.broadcasted_iota(jnp.int32, (LANES, LANES), 0)
    t = lax.broadcasted_iota(jnp.int32, (LANES, LANES), 1)
    same = (s // GDN_CHUNK) == (t // GDN_CHUNK)
    m_f = jnp.where(same & (s <= t), 1.0, 0.0)
    m_b = jnp.where(same & (s >= t), 1.0, 0.0)
    for jb in range(seq // LANES):
        cols = slice(jb * LANES, (jb + 1) * LANES)
        gb = g[:, cols]
        o_ref[0, 0:GDN_HEADS, cols] = _dot(gb, m_f, precision=HIGHEST)[0:GDN_HEADS]
        o_ref[0, GDN_HEADS:nh2, cols] = _dot(gb, m_b, precision=HIGHEST)[GDN_HEADS:nh2]


def gdn_gates(pre_t, a_log, dt_bias):
    nb, rows, seq = pre_t.shape
    col = lambda t: t.astype(F32).reshape(2 * GDN_HEADS, 1)
    return pl.pallas_call(
        _gdn_gates_kernel,
        grid=(nb,),
        in_specs=[pl.BlockSpec((1, rows, seq), lambda b: (b, 0, 0)),
                  pl.BlockSpec((2 * GDN_HEADS, 1), lambda b: (0, 0)),
                  pl.BlockSpec((2 * GDN_HEADS, 1), lambda b: (0, 0))],
        out_specs=pl.BlockSpec((1, rows, seq), lambda b: (b, 0, 0)),
        out_shape=jax.ShapeDtypeStruct((nb, rows, seq), F32),
        compiler_params=_cparams("parallel"),
        name="gdn_gates",
    )(pre_t, col(a_log), col(dt_bias))


def _nilpotent_inverses(mats, nilpotency):
    n = mats[0].shape[0]
    eye = jnp.where(lax.broadcasted_iota(jnp.int32, (n, n), 0) == lax.broadcasted_iota(jnp.int32, (n, n), 1), 1.0, 0.0)
    ps = [eye - a for a in mats]
    a16 = [a.astype(BF16) for a in mats]
    sqs = [_dot(a, a) for a in a16]
    k = 2
    while True:
        sq16 = [s.astype(BF16) for s in sqs]
        ps = [p + _dot(p.astype(BF16), s) for p, s in zip(ps, sq16)]
        k *= 2
        if k >= nilpotency:
            return ps
        sqs = [_dot(s, s) for s in sq16]


def _gdn_local_kernel(q_ref, k_ref, v_ref, gcol_ref, grow_ref, uwqf_ref, uwqb_ref, ktf_ref, ktb_ref, in_ref, eg_ref):
    r = q_ref.shape[1]
    c = GDN_CHUNK
    hd = GDN_HEAD_DIM
    ii = lax.broadcasted_iota(jnp.int32, (r, r), 0)
    jj = lax.broadcasted_iota(jnp.int32, (r, r), 1)
    same = (ii >= c) == (jj >= c)
    incl = (same & (ii >= jj), same & (ii <= jj))
    strict = (same & (ii > jj), same & (ii < jj))
    first_chunk = lax.broadcasted_iota(jnp.int32, (r, 1), 0) < c
    zeros = jnp.zeros((r, r), F32)
    big_a, rhs, keep = [], [], []
    for h in range(GDN_HEADS):
        cols = slice(h * hd, (h + 1) * hd)
        q16, k16 = q_ref[0, :, cols], k_ref[0, :, cols]
        q, k, v = q16.astype(F32), k16.astype(F32), v_ref[0, :, cols].astype(F32)
        kk = _dot_nt(k16, k16)
        qk = _dot_nt(q16, k16)
        a_dir, rhs_dir = [], []
        for d in range(2):
            lane = d * GDN_HEADS + h
            gc_col = gcol_ref[0, :, lane:lane + 1]
            gc_row = grow_ref[0, lane:lane + 1, :]
            beta = gcol_ref[0, :, 2 * GDN_HEADS + lane:2 * GDN_HEADS + lane + 1]
            decay = jnp.exp(jnp.where(incl[d], gc_col - gc_row, -1e30))
            a_dir.append(jnp.where(strict[d], kk * beta * decay, 0.0))
            intra = jnp.where(incl[d], qk * decay, 0.0)
            ends = (gc_col[c - 1:c], gc_col[2 * c - 1:2 * c]) if d == 0 else (gc_col[0:1], gc_col[c:c + 1])
            g_last = jnp.where(first_chunk, ends[0], ends[1])
            eg = jnp.exp(gc_col)
            kb = k * beta
            rhs_dir.append(jnp.concatenate([v * beta, kb * eg], axis=1))
            keep.append((q * eg, (k * jnp.exp(g_last - gc_col)).T, intra[:, :c] + intra[:, c:]))
            for ch in range(2):
                eg_ref[0, ch, lane:lane + 1, :] = jnp.broadcast_to(jnp.exp(ends[ch]), (1, LANES))
        big_a.append(jnp.concatenate([jnp.concatenate([a_dir[0], zeros], axis=1),
                                      jnp.concatenate([zeros, a_dir[1]], axis=1)], axis=0))
        rhs.append(jnp.concatenate(rhs_dir, axis=0))
    t_inv = _nilpotent_inverses(big_a, c)
    for h in range(GDN_HEADS):
        uw = _dot(t_inv[h].astype(BF16), rhs[h].astype(BF16))
        (qd_f, kt_f, in_f), (qd_b, kt_b, in_b) = keep[2 * h], keep[2 * h + 1]
        uwqf_ref[0, h] = jnp.concatenate([uw[:r], qd_f], axis=1).astype(BF16)
        uwqb_ref[0, h] = jnp.concatenate([uw[r:], qd_b], axis=1).astype(BF16)
        ktf_ref[0, h] = kt_f.astype(BF16)
        ktb_ref[0, h] = kt_b.astype(BF16)
        in_ref[0, h] = jnp.concatenate([in_f, in_b], axis=1).astype(BF16)


def gdn_local(qkv, gates_col, gates_row):
    nb, seq, _ = qkv.shape
    r = 2 * GDN_CHUNK
    hd, nh = GDN_HEAD_DIM, GDN_HEADS
    uwq_sd = jax.ShapeDtypeStruct((nb, nh, seq, 3 * hd), BF16)
    kt_sd = jax.ShapeDtypeStruct((nb, nh, hd, seq), BF16)
    uwq_spec = pl.BlockSpec((1, nh, r, 3 * hd), lambda b, i: (b, 0, i, 0))
    kt_spec = pl.BlockSpec((1, nh, hd, r), lambda b, i: (b, 0, 0, i))
    return pl.pallas_call(
        _gdn_local_kernel,
        grid=(nb, seq // r),
        in_specs=[pl.BlockSpec((1, r, GDN_WIDTH), lambda b, i: (b, i, 0)),
                  pl.BlockSpec((1, r, GDN_WIDTH), lambda b, i: (b, i, 1)),
                  pl.BlockSpec((1, r, GDN_WIDTH), lambda b, i: (b, i, 2)),
                  pl.BlockSpec((1, r, 4 * nh), lambda b, i: (b, i, 0)),
                  pl.BlockSpec((1, 4 * nh, r), lambda b, i: (b, 0, i))],
        out_specs=[uwq_spec, uwq_spec, kt_spec, kt_spec,
                   pl.BlockSpec((1, nh, r, 2 * GDN_CHUNK), lambda b, i: (b, 0, i, 0)),
                   pl.BlockSpec((1, 2, SUBLANES, LANES), lambda b, i: (b, i, 0, 0))],
        out_shape=[uwq_sd, uwq_sd, kt_sd, kt_sd,
                   jax.ShapeDtypeStruct((nb, nh, seq, 2 * GDN_CHUNK), BF16),
                   jax.ShapeDtypeStruct((nb, seq // GDN_CHUNK, SUBLANES, LANES), F32)],
        compiler_params=_cparams("parallel", "parallel"),
        name="gdn_local",
    )(qkv, qkv, qkv, gates_col, gates_row)


def _gdn_state_kernel(uwqf, ktf, inf, egf, uwqb, ktb, inb, egb, of_ref, ob_ref, st_ref):
    @pl.when(pl.program_id(1) == 0)
    def _():
        st_ref[...] = jnp.zeros_like(st_ref)

    c = GDN_CHUNK
    hd = GDN_HEAD_DIM
    nc = uwqf.shape[2] // c
    chains = [(d, h) for d in range(2) for h in range(GDN_HEADS)]
    refs = ((uwqf, ktf, inf, egf, of_ref), (uwqb, ktb, inb, egb, ob_ref))
    states = [st_ref[j] for j in range(len(chains))]
    for step in range(nc):
        sws = []
        for j, (d, h) in enumerate(chains):
            ci = step if d == 0 else nc - 1 - step
            rows = slice(ci * c, (ci + 1) * c)
            uwq = refs[d][0]
            wq = jnp.concatenate([uwq[0, h, rows, hd:2 * hd], uwq[0, h, rows, 2 * hd:3 * hd]], axis=0)
            sws.append(_dot(wq, states[j].astype(BF16)))
        for j, (d, h) in enumerate(chains):
            ci = step if d == 0 else nc - 1 - step
            rows = slice(ci * c, (ci + 1) * c)
            uwq, kt, intra, eg, o_ref = refs[d]
            sw = sws[j]
            v16 = (uwq[0, h, rows, 0:hd].astype(F32) - sw[:c]).astype(BF16)
            out = sw[c:] + _dot(intra[0, h, rows, d * c:(d + 1) * c], v16)
            o_ref[0, rows, h * hd:(h + 1) * hd] = out.astype(o_ref.dtype)
            lane = d * GDN_HEADS + h
            states[j] = states[j] * eg[0, ci, lane:lane + 1, :] + _dot(kt[0, h, :, rows], v16)
    for j in range(len(chains)):
        st_ref[j] = states[j]


def gdn_state_scan(uwq_f, uwq_b, kt_f, kt_b, intra, eg):
    nb, nh, seq, _ = uwq_f.shape
    hd = GDN_HEAD_DIM
    nc = GDN_STATE_CHUNKS
    r = nc * GDN_CHUNK
    n = seq // r

    def specs(pos):
        return [pl.BlockSpec((1, nh, r, 3 * hd), lambda b, i: (b, 0, pos(i), 0)),
                pl.BlockSpec((1, nh, hd, r), lambda b, i: (b, 0, 0, pos(i))),
                pl.BlockSpec((1, nh, r, 2 * GDN_CHUNK), lambda b, i: (b, 0, pos(i), 0)),
                pl.BlockSpec((1, nc, SUBLANES, LANES), lambda b, i: (b, pos(i), 0, 0))]

    fwd = lambda i: i
    bwd = lambda i: n - 1 - i
    out_sd = jax.ShapeDtypeStruct((nb, seq, GDN_WIDTH), BF16)
    return pl.pallas_call(
        _gdn_state_kernel,
        grid=(nb, n),
        in_specs=specs(fwd) + specs(bwd),
        out_specs=[pl.BlockSpec((1, r, GDN_WIDTH), lambda b, i: (b, i, 0)),
                   pl.BlockSpec((1, r, GDN_WIDTH), lambda b, i: (b, n - 1 - i, 0))],
        out_shape=[out_sd, out_sd],
        scratch_shapes=[pltpu.VMEM((2 * nh, hd, hd), F32)],
        compiler_params=_cparams("parallel", "arbitrary"),
        name="gdn_state_scan",
    )(uwq_f, kt_f, intra, eg, uwq_b, kt_b, intra, eg)


def _out_proj_kernel(x_ref, mod_ref, ys5_ref, cx_ref, cb_ref, cc_ref, cxp_ref, cxn_ref, ccp_ref, ccn_ref,
                     cw_ref, of_ref, ob_ref, z_ref, gnw_ref, w_ref, o_ref, *, seq):
    tm = x_ref.shape[0]
    t0 = (pl.program_id(0) * tm) % seq
    f32 = lambda ref, rows=slice(None): ref[rows, :].astype(F32)
    last = slice(HALO_ROWS - 1, HALO_ROWS)
    prev_row = jnp.where(t0 == 0, 0.0, f32(ccp_ref, last) * f32(cxp_ref, last))
    next_row = jnp.where(t0 + tm == seq, 0.0, f32(ccn_ref, slice(0, 1)) * f32(cxn_ref, slice(0, 1)))
    y_conv = f32(cb_ref) * _conv3_rows(f32(cc_ref) * f32(cx_ref), prev_row, next_row, cw_ref)
    o = f32(of_ref) + f32(ob_ref)
    z = f32(z_ref)
    heads = []
    for h in range(GDN_HEADS):
        cols = slice(h * GDN_HEAD_DIM, (h + 1) * GDN_HEAD_DIM)
        heads.append(_rms(o[:, cols]) * gnw_ref[...] * _silu(z[:, cols]))
    y_gdn = jnp.concatenate(heads, axis=1)
    c0, c1 = S5_WIDTH, S5_WIDTH + CONV_WIDTH
    mix = (_dot(ys5_ref[...].astype(BF16), w_ref[0:c0, :])
           + _dot(y_conv.astype(BF16), w_ref[c0:c1, :])
           + _dot(y_gdn.astype(BF16), w_ref[c1:, :]))
    o_ref[...] = x_ref[...] + mod_ref[0, 2:3, :] * mix


def out_projection(x2d, mod, y_s5, proj, conv_w, o_f, o_b, gdn_norm_w, w_out, seq):
    m, d = x2d.shape
    tm = ROW_TILE
    per_seq = seq // tm
    cw = CONV_WIDTH
    row = lambda c: pl.BlockSpec((tm, cw), lambda i: (i, c))
    cxp, cxn = _halo_specs(tm, cw, lambda: 1, m)
    ccp, ccn = _halo_specs(tm, cw, lambda: 3, m)
    gw = GDN_WIDTH
    return pl.pallas_call(
        functools.partial(_out_proj_kernel, seq=seq),
        grid=(m // tm,),
        in_specs=[pl.BlockSpec((tm, d), lambda i: (i, 0)),
                  pl.BlockSpec((1, 6, d), lambda i: (i // per_seq, 0, 0)),
                  pl.BlockSpec((tm, cw), lambda i: (i, 0)),
                  row(1), row(2), row(3), cxp, cxn, ccp, ccn,
                  pl.BlockSpec((3, cw), lambda i: (0, 0)),
                  pl.BlockSpec((tm, gw), lambda i: (i, 0)),
                  pl.BlockSpec((tm, gw), lambda i: (i, 0)),
                  pl.BlockSpec((tm, gw), lambda i: (i, (PROJ_MAIN - gw) // gw)),
                  pl.BlockSpec((1, GDN_HEAD_DIM), lambda i: (0, 0)),
                  pl.BlockSpec((d, d), lambda i: (0, 0))],
        out_specs=pl.BlockSpec((tm, d), lambda i: (i, 0)),
        out_shape=jax.ShapeDtypeStruct((m, d), F32),
        compiler_params=_cparams("parallel"),
        name="out_projection",
    )(x2d, mod, y_s5, proj, proj, proj, proj, proj, proj, proj, conv_w, o_f, o_b, proj, gdn_norm_w, w_out)


def _first_argmax(vals, rows, n):
    m = jnp.max(vals, axis=0, keepdims=True)
    return m, jnp.min(jnp.where(vals == m, rows, n), axis=0, keepdims=True)


def _route_kernel(x_ref, mod_ref, nw_ref, wr_ref, br_ref, h_ref, idx_ref, wgt_ref, cnt_ref, carry_ref, before_ref):
    step = pl.program_id(0)
    tm = x_ref.shape[0]

    @pl.when(step == 0)
    def _():
        carry_ref[...] = jnp.zeros_like(carry_ref)
        earlier = lax.broadcasted_iota(jnp.int32, (tm, tm), 0) < lax.broadcasted_iota(jnp.int32, (tm, tm), 1)
        before_ref[...] = jnp.where(earlier, 1.0, 0.0).astype(BF16)

    h = _rms(x_ref[...]) * nw_ref[...] * (1.0 + mod_ref[0, 4:5, :]) + mod_ref[0, 3:4, :]
    h_ref[...] = h.reshape(h_ref.shape)
    nr = br_ref.shape[0]
    h_hi = h.astype(BF16)
    h_lo = (h - h_hi.astype(F32)).astype(BF16)
    head = _dot_nt(wr_ref[...], h_hi)
    logits = head[:nr] + head[nr:] + _dot_nt(wr_ref[0:nr, :], h_lo) + br_ref[...]
    lg = logits[N_EXPERTS:N_EXPERTS + MOE_GROUPS, :]
    rows_g = lax.broadcasted_iota(jnp.int32, lg.shape, 0)
    g_max, g_idx = _first_argmax(lg, rows_g, MOE_GROUPS)
    g_w = 1.0 / jnp.sum(jnp.exp(lg - g_max), axis=0, keepdims=True)
    le = jnp.zeros((EXPERTS_PER_GROUP, tm), F32)
    for g in range(MOE_GROUPS):
        le = jnp.where(g_idx == g, logits[g * EXPERTS_PER_GROUP:(g + 1) * EXPERTS_PER_GROUP, :], le)
    rows_e = lax.broadcasted_iota(jnp.int32, le.shape, 0)
    m1, i1 = _first_argmax(le, rows_e, EXPERTS_PER_GROUP)
    m2, i2 = _first_argmax(jnp.where(rows_e == i1, -jnp.inf, le), rows_e, EXPERTS_PER_GROUP)
    r = jnp.exp(m2 - m1)
    w1 = g_w / (1.0 + r)
    e1 = g_idx * EXPERTS_PER_GROUP + i1
    e2 = g_idx * EXPERTS_PER_GROUP + i2
    rows_x = lax.broadcasted_iota(jnp.int32, (N_EXPERTS, tm), 0)
    hit1 = rows_x == e1
    hit2 = rows_x == e2
    onehot = jnp.where(hit1 | hit2, 1.0, 0.0)
    ahead = _dot(onehot.astype(BF16), before_ref[...]) + carry_ref[:, 0:1]
    rank1 = jnp.sum(jnp.where(hit1, ahead, 0.0), axis=0, keepdims=True)
    rank2 = jnp.sum(jnp.where(hit2, ahead, 0.0), axis=0, keepdims=True)
    idx_ref[0:1, :] = e1
    idx_ref[1:2, :] = e2
    idx_ref[2:3, :] = rank1.astype(jnp.int32)
    idx_ref[3:4, :] = rank2.astype(jnp.int32)
    wgt_ref[0:1, :] = w1
    wgt_ref[1:2, :] = w1 * r
    carry_ref[...] = carry_ref[...] + jnp.sum(onehot, axis=1, keepdims=True)
    cnt_ref[...] = carry_ref[...]


def moe_route(x2d, mod, norm_w, w_router_t, b_router, seq):
    m, d = x2d.shape
    tm = ROUTE_TILE
    per_seq = seq // tm
    nr = w_router_t.shape[0]
    return pl.pallas_call(
        _route_kernel,
        grid=(m // tm,),
        in_specs=[pl.BlockSpec((tm, d), lambda i: (i, 0)),
                  pl.BlockSpec((1, 6, d), lambda i: (i // per_seq, 0, 0)),
                  pl.BlockSpec((1, d), lambda i: (0, 0)),
                  pl.BlockSpec((nr, d), lambda i: (0, 0)),
                  pl.BlockSpec((nr // 2, 1), lambda i: (0, 0))],
        out_specs=[pl.BlockSpec((tm,) + SLAB, lambda i: (i, 0, 0)),
                   pl.BlockSpec((4, tm), lambda i: (0, i)),
                   pl.BlockSpec((2, tm), lambda i: (0, i)),
                   pl.BlockSpec((N_EXPERTS, LANES), lambda i: (0, 0))],
        out_shape=[jax.ShapeDtypeStruct((m,) + SLAB, F32),
                   jax.ShapeDtypeStruct((4, m), jnp.int32),
                   jax.ShapeDtypeStruct((2, m), F32),
                   jax.ShapeDtypeStruct((N_EXPERTS, LANES), F32)],
        scratch_shapes=[pltpu.VMEM((N_EXPERTS, LANES), F32), pltpu.VMEM((tm, tm), BF16)],
        compiler_params=_cparams("arbitrary"),
        name="moe_route",
    )(x2d, mod, norm_w, w_router_t, b_router)


def _row_copy(src_ref, src_row, dst_ref, dst_row, sem):
    return pltpu.make_async_copy(src_ref.at[pl.ds(src_row, 1)], dst_ref.at[pl.ds(dst_row, 1)], sem)


def _scatter_kernel(last_ref, d1_ref, d2_ref, h_ref, xs_ref, zero_ref, sem):
    tm = h_ref.shape[0]

    @pl.when(pl.program_id(0) == 0)
    def _():
        zero_ref[...] = jnp.zeros_like(zero_ref)

        def fill(e, c):
            @pl.when(last_ref[e] >= 0)
            def _():
                pltpu.make_async_copy(zero_ref, xs_ref.at[pl.ds(last_ref[e], MOE_BLOCK)], sem).start()
            return c

        def drain(e, c):
            @pl.when(last_ref[e] >= 0)
            def _():
                pltpu.make_async_copy(zero_ref, xs_ref.at[pl.ds(0, MOE_BLOCK)], sem).wait()
            return c

        lax.fori_loop(0, N_EXPERTS, fill, 0)
        lax.fori_loop(0, N_EXPERTS, drain, 0)

        n_blocks = xs_ref.shape[0] // MOE_BLOCK

        def fill_unused(b, c):
            pltpu.make_async_copy(zero_ref, xs_ref.at[pl.ds(b * MOE_BLOCK, MOE_BLOCK)], sem).start()
            return c

        def drain_unused(b, c):
            pltpu.make_async_copy(zero_ref, xs_ref.at[pl.ds(0, MOE_BLOCK)], sem).wait()
            return c

        lax.fori_loop(last_ref[N_EXPERTS], n_blocks, fill_unused, 0)
        lax.fori_loop(last_ref[N_EXPERTS], n_blocks, drain_unused, 0)

    def start(r, c):
        _row_copy(h_ref, r, xs_ref, d1_ref[0, 0, r], sem).start(priority=0)
        _row_copy(h_ref, r, xs_ref, d2_ref[0, 0, r], sem).start(priority=1)
        return c

    def wait(r, c):
        _row_copy(h_ref, 0, xs_ref, 0, sem).wait()
        _row_copy(h_ref, 0, xs_ref, 0, sem).wait()
        return c

    lax.fori_loop(0, tm, start, 0, unroll=DMA_UNROLL)
    lax.fori_loop(0, tm, wait, 0, unroll=DMA_UNROLL)


def moe_scatter(h_slab, dest1, dest2, last_block_start, n_slots):
    m = h_slab.shape[0]
    tm = ROUTE_TILE
    idx_spec = pl.BlockSpec((1, 1, tm), lambda i, last: (i, 0, 0), memory_space=pltpu.SMEM)
    return pl.pallas_call(
        _scatter_kernel,
        grid_spec=pltpu.PrefetchScalarGridSpec(
            num_scalar_prefetch=1,
            grid=(m // tm,),
            in_specs=[idx_spec, idx_spec, pl.BlockSpec((tm,) + SLAB, lambda i, last: (i, 0, 0))],
            out_specs=pl.BlockSpec(memory_space=pl.ANY),
            scratch_shapes=[pltpu.VMEM((MOE_BLOCK,) + SLAB, F32), pltpu.SemaphoreType.DMA(())]),
        out_shape=jax.ShapeDtypeStruct((n_slots,) + SLAB, F32),
        compiler_params=_cparams("arbitrary"),
        name="moe_scatter",
    )(last_block_start, dest1.reshape(m // tm, 1, tm), dest2.reshape(m // tm, 1, tm), h_slab)


def _expert_kernel(be_ref, nused_ref, xs_ref, wgu_ref, wd_ref, y_ref, wgu16_ref, wd16_ref):
    i = pl.program_id(0)
    changed = jnp.logical_or(i == 0, be_ref[i] != be_ref[jnp.maximum(i - 1, 0)])

    @pl.when(changed)
    def _():
        wgu16_ref[...] = wgu_ref[0].astype(BF16)
        wd16_ref[...] = wd_ref[0].astype(BF16)

    @pl.when(i < nused_ref[0])
    def _():
        x = xs_ref[...].reshape(MOE_BLOCK, D_MODEL).astype(BF16)
        gu = _dot(x, wgu16_ref[...])
        act = _silu(gu[:, :EXPERT_FF]) * gu[:, EXPERT_FF:]
        y = _dot(act.astype(BF16), wd16_ref[...])
        y_ref[...] = y.reshape(y_ref.shape)

    @pl.when(i >= nused_ref[0])
    def _():
        y_ref[...] = jnp.zeros_like(y_ref)


def moe_experts(xs, blk_expert, n_used, w_gate_up, w_down):
    n_slots = xs.shape[0]
    n_blocks = n_slots // MOE_BLOCK
    d = D_MODEL
    return pl.pallas_call(
        _expert_kernel,
        grid_spec=pltpu.PrefetchScalarGridSpec(
            num_scalar_prefetch=2,
            grid=(n_blocks,),
            in_specs=[pl.BlockSpec((MOE_BLOCK,) + SLAB, lambda i, be, nu: (jnp.minimum(i, nu[0] - 1), 0, 0)),
                      pl.BlockSpec((1, d, 2 * EXPERT_FF), lambda i, be, nu: (be[i], 0, 0)),
                      pl.BlockSpec((1, EXPERT_FF, d), lambda i, be, nu: (be[i], 0, 0))],
            out_specs=pl.BlockSpec((MOE_BLOCK,) + SLAB, lambda i, be, nu: (i, 0, 0)),
            scratch_shapes=[pltpu.VMEM((d, 2 * EXPERT_FF), BF16), pltpu.VMEM((EXPERT_FF, d), BF16)]),
        out_shape=jax.ShapeDtypeStruct((n_slots,) + SLAB, F32),
        compiler_params=_cparams("arbitrary"),
        name="moe_experts",
    )(blk_expert, n_used, xs, w_gate_up, w_down)


def _combine_kernel(d1_ref, d2_ref, x_ref, mod_ref, wgt_ref, fnw_ref, yb_ref, o_ref, y1_ref, y2_ref, sem, *, final):
    tm = x_ref.shape[0]

    def start(r, c):
        _row_copy(yb_ref, d1_ref[0, 0, r], y1_ref, r, sem).start(priority=0)
        _row_copy(yb_ref, d2_ref[0, 0, r], y2_ref, r, sem).start(priority=1)
        return c

    def wait(r, c):
        _row_copy(yb_ref, 0, y1_ref, 0, sem).wait()
        _row_copy(yb_ref, 0, y2_ref, 0, sem).wait()
        return c

    lax.fori_loop(0, tm, start, 0, unroll=DMA_UNROLL)
    lax.fori_loop(0, tm, wait, 0, unroll=DMA_UNROLL)
    w1 = wgt_ref[:, 0:1]
    w2 = wgt_ref[:, 1:2]
    y = y1_ref[...].reshape(x_ref.shape) * w1 + y2_ref[...].reshape(x_ref.shape) * w2
    o_ref[...] = x_ref[...] + mod_ref[0, 5:6, :] * y
    if final:
        o_ref[...] = _rms(o_ref[...]) * fnw_ref[...]


def moe_combine(x2d, mod, yb, dest1, dest2, weights, final_norm_w, seq, final):
    m, d = x2d.shape
    tm = COMBINE_TILE
    per_seq = seq // tm
    idx_spec = pl.BlockSpec((1, 1, tm), lambda i: (i, 0, 0), memory_space=pltpu.SMEM)
    return pl.pallas_call(
        functools.partial(_combine_kernel, final=final),
        grid=(m // tm,),
        in_specs=[idx_spec, idx_spec,
                  pl.BlockSpec((tm, d), lambda i: (i, 0)),
                  pl.BlockSpec((1, 6, d), lambda i: (i // per_seq, 0, 0)),
                  pl.BlockSpec((tm, 2), lambda i: (i, 0)),
                  pl.BlockSpec((1, d), lambda i: (0, 0)),
                  pl.BlockSpec(memory_space=pl.ANY)],
        out_specs=pl.BlockSpec((tm, d), lambda i: (i, 0)),
        out_shape=jax.ShapeDtypeStruct((m, d), F32),
        scratch_shapes=[pltpu.VMEM((tm,) + SLAB, F32), pltpu.VMEM((tm,) + SLAB, F32),
                        pltpu.SemaphoreType.DMA(())],
        compiler_params=_cparams("arbitrary"),
        name="moe_combine",
    )(dest1.reshape(m // tm, 1, tm), dest2.reshape(m // tm, 1, tm), x2d, mod, weights, final_norm_w, yb)


def hier_moe_layer(x2d, mod, norm_w, w_router_t, b_router, w_gate_up, w_down, final_norm_w, seq, final):
    m = x2d.shape[0]
    h_slab, idx, wgt, counts = moe_route(x2d, mod, norm_w, w_router_t, b_router, seq)
    counts = counts[:, 0].astype(jnp.int32)
    padded = (counts + MOE_BLOCK - 1) // MOE_BLOCK * MOE_BLOCK
    pad_end = jnp.cumsum(padded)
    pad_start = pad_end - padded
    dest1 = pad_start[idx[0]] + idx[2]
    dest2 = pad_start[idx[1]] + idx[3]
    n_blocks = 2 * m // MOE_BLOCK + N_EXPERTS
    blk_start = jnp.arange(n_blocks, dtype=jnp.int32) * MOE_BLOCK
    blk_expert = jnp.minimum(jnp.sum((pad_end[None, :] <= blk_start[:, None]).astype(jnp.int32), axis=1),
                             N_EXPERTS - 1)
    n_used = (pad_end[-1:] // MOE_BLOCK).astype(jnp.int32)
    fill_plan = jnp.concatenate([jnp.where(padded > 0, pad_end - MOE_BLOCK, -1), n_used]).astype(jnp.int32)
    xs = moe_scatter(h_slab, dest1, dest2, fill_plan, n_blocks * MOE_BLOCK)
    yb = moe_experts(xs, blk_expert, n_used, w_gate_up, w_down)
    return moe_combine(x2d, mod, yb, dest1, dest2, wgt.T, final_norm_w, seq, final)


def _layer_params(l, p):
    w_in = p['w_in'][l]
    w_gate = jnp.zeros((D_MODEL, LANES), F32).at[:, :N_GATE_COLS].set(w_in[:, PROJ_MAIN:])
    w_router_t = jnp.zeros((ROUTER_ROWS, D_MODEL), F32)
    w_router_t = w_router_t.at[:N_EXPERTS].set(p['moe_w_expert'][l].T)
    w_router_t = w_router_t.at[N_EXPERTS:N_EXPERTS + MOE_GROUPS].set(p['moe_w_group'][l].T)
    w_router_hi = w_router_t.astype(BF16)
    w_router_t = jnp.concatenate([w_router_hi, (w_router_t - w_router_hi.astype(F32)).astype(BF16)], axis=0)
    b_router = jnp.zeros((ROUTER_ROWS, 1), F32)
    b_router = b_router.at[:N_EXPERTS, 0].set(p['moe_b_expert'][l])
    b_router = b_router.at[N_EXPERTS:N_EXPERTS + MOE_GROUPS, 0].set(p['moe_b_group'][l])
    s5 = [s5_discretise(p['s5_lam_re'][l, d], p['s5_lam_im'][l, d], p['s5_log_step'][l, d],
                        p['s5_b_re'][l, d], p['s5_b_im'][l, d], p['s5_c_re'][l, d], p['s5_c_im'][l, d])
          for d in range(2)]
    return dict(
        norm_mix=p['norm_mix'][l][None, :], norm_ffn=p['norm_ffn'][l][None, :],
        w_main=w_in[:, :PROJ_MAIN].astype(BF16), w_gate=w_gate.astype(BF16),
        w_out=p['w_out'][l].astype(BF16), s5=s5,
        s5_d=p['s5_d'][l][None, :], s5_w_glu=p['s5_w_glu'][l].astype(BF16), s5_b_glu=p['s5_b_glu'][l][None, :],
        conv_w=p['conv_w'][l], gdn_conv_w=p['gdn_conv_w'][l], gdn_a_log=p['gdn_a_log'][l],
        gdn_dt_bias=p['gdn_dt_bias'][l], gdn_norm_w=p['gdn_norm_w'][l][None, :],
        w_router_t=w_router_t, b_router=b_router,
        w_gate_up=p['moe_w_gate_up'][l], w_down=p['moe_w_down'][l])


def _trunk(x, mods, layers, final_norm_w):
    nb, seq, d = x.shape
    m = nb * seq
    x2d = x.reshape(m, d)
    for l, lp in enumerate(layers):
        mod = mods[l].reshape(nb, 6, d)
        proj, gate_pre = in_projection(x2d, mod, lp['norm_mix'], lp['w_main'], lp['w_gate'], seq)
        u_t = proj[:, :S5_WIDTH].astype(F32).reshape(nb, seq, S5_WIDTH).transpose(1, 0, 2)
        (a_f, bm_f, cm_f), (a_b, bm_b, cm_b) = lp['s5']
        y_s5 = s5_mixer(u_t, a_f, bm_f, cm_f, a_b, bm_b, cm_b, lp['s5_d'], lp['s5_w_glu'], lp['s5_b_glu'])
        y_s5 = y_s5.astype(BF16).transpose(1, 0, 2).reshape(m, S5_WIDTH)
        qkv = gdn_prepare(proj, lp['gdn_conv_w'], seq).reshape(nb, seq, 3 * GDN_WIDTH)
        pre_t = gate_pre[:, :N_GATE_COLS].reshape(nb, seq, N_GATE_COLS).transpose(0, 2, 1)
        gates = gdn_gates(pre_t, lp['gdn_a_log'], lp['gdn_dt_bias'])
        uwq_f, uwq_b, kt_f, kt_b, intra, eg = gdn_local(qkv, gates.transpose(0, 2, 1), gates)
        o_f, o_b = gdn_state_scan(uwq_f, uwq_b, kt_f, kt_b, intra, eg)
        x2d = out_projection(x2d, mod, y_s5, proj, lp['conv_w'], o_f.reshape(m, GDN_WIDTH),
                             o_b.reshape(m, GDN_WIDTH), lp['gdn_norm_w'], lp['w_out'], seq)
        x2d = hier_moe_layer(x2d, mod, lp['norm_ffn'], lp['w_router_t'], lp['b_router'],
                             lp['w_gate_up'], lp['w_down'], final_norm_w, seq, final=(l == len(layers) - 1))
    return x2d.reshape(nb, seq, d)


def kernel(x_prompt, x_sample, c_prompt, c_sample, w_ada, b_ada, norm_mix, norm_ffn, w_in, w_out, s5_lam_re, s5_lam_im, s5_log_step, s5_b_re, s5_b_im, s5_c_re, s5_c_im, s5_d, s5_w_glu, s5_b_glu, conv_w, gdn_conv_w, gdn_a_log, gdn_dt_bias, gdn_norm_w, moe_w_group, moe_b_group, moe_w_expert, moe_b_expert, moe_w_gate_up, moe_w_down, final_norm):
    p = dict(norm_mix=norm_mix, norm_ffn=norm_ffn, w_in=w_in, w_out=w_out, s5_lam_re=s5_lam_re,
             s5_lam_im=s5_lam_im, s5_log_step=s5_log_step, s5_b_re=s5_b_re, s5_b_im=s5_b_im, s5_c_re=s5_c_re,
             s5_c_im=s5_c_im, s5_d=s5_d, s5_w_glu=s5_w_glu, s5_b_glu=s5_b_glu, conv_w=conv_w,
             gdn_conv_w=gdn_conv_w, gdn_a_log=gdn_a_log, gdn_dt_bias=gdn_dt_bias, gdn_norm_w=gdn_norm_w,
             moe_w_group=moe_w_group, moe_b_group=moe_b_group, moe_w_expert=moe_w_expert,
             moe_b_expert=moe_b_expert, moe_w_gate_up=moe_w_gate_up, moe_w_down=moe_w_down)
    depth = w_ada.shape[0]
    layers = [_layer_params(l, p) for l in range(depth)]
    nbp = c_prompt.shape[0]
    mods = ada_modulation(jnp.concatenate([c_prompt, c_sample], axis=0), w_ada, b_ada)
    fnw = final_norm[None, :]
    y_prompt = _trunk(x_prompt, mods[:, :nbp], layers, fnw)
    y_sample = _trunk(x_sample, mods[:, nbp:], layers, fnw)
    return (y_prompt, y_sample)
```

```python
import functools

import jax
import jax.numpy as jnp
from jax import lax
from jax.experimental import pallas as pl
from jax.experimental.pallas import tpu as pltpu

F32 = jnp.float32
BF16 = jnp.bfloat16
HIGHEST = lax.Precision.HIGHEST

D_MODEL = 1024
DEPTH = 4
S5_WIDTH = 256
S5_GROUP = 16
S5_GROUPS = 16
S5_STATE = 64
S5_LANES = S5_GROUPS * S5_STATE
CONV_WIDTH = 256
GDN_WIDTH = 512
GDN_HEADS = 4
GDN_HEAD_DIM = 128
GDN_CHUNK = 64
PROJ_MAIN = S5_WIDTH + 3 * CONV_WIDTH + 4 * GDN_WIDTH
N_GATE_COLS = 4 * GDN_HEADS
MOE_GROUPS = 4
EXPERTS_PER_GROUP = 8
N_EXPERTS = 32
EXPERT_FF = 512
MOE_BLOCK = 512
EPS = 1e-6

SUBLANES = 8
LANES = 128
SLAB = (SUBLANES, LANES)
VMEM_LIMIT = 56 * 1024 * 1024

ROW_TILE = 512
S5_TIME_TILE = 64
GDN_STATE_CHUNKS = 8
ROUTE_TILE = 512
ROUTER_ROWS = N_EXPERTS + 16
DMA_UNROLL = 8


def _cparams(*sem):
    return pltpu.CompilerParams(dimension_semantics=sem, vmem_limit_bytes=VMEM_LIMIT)


def _dot(a, b, **kw):
    return jnp.dot(a, b, preferred_element_type=F32, **kw)


def _dot_nt(a, b, **kw):
    return lax.dot_general(a, b, (((1,), (1,)), ((), ())), preferred_element_type=F32, **kw)


def _dot_tn(a, b, **kw):
    return lax.dot_general(a, b, (((0,), (0,)), ((), ())), preferred_element_type=F32, **kw)


def _sigmoid(x):
    return 0.5 * jnp.tanh(0.5 * x) + 0.5


def _silu(x):
    return x * _sigmoid(x)


def _gelu_tanh(x):
    return 0.5 * x * (1.0 + jnp.tanh(0.7978845608028654 * (x + 0.044715 * (x * x * x))))


def _softplus(x):
    return jnp.maximum(x, 0.0) + jnp.log(1.0 + jnp.exp(-jnp.abs(x)))


def _rms(x):
    return x * lax.rsqrt(jnp.mean(x * x, axis=-1, keepdims=True) + EPS)


def _ada_kernel(c_ref, w_ref, b_ref, o_ref):
    o_ref[0] = _dot(_silu(c_ref[...]), w_ref[0], precision=HIGHEST) + b_ref[0]


def ada_modulation(c_all, w_ada, b_ada):
    nb = c_all.shape[0]
    depth, d, n = w_ada.shape
    tn = 1536
    return pl.pallas_call(
        _ada_kernel,
        grid=(depth, n // tn),
        in_specs=[pl.BlockSpec((nb, d), lambda l, j: (0, 0)),
                  pl.BlockSpec((1, d, tn), lambda l, j: (l, 0, j)),
                  pl.BlockSpec((1, 1, tn), lambda l, j: (l, 0, j))],
        out_specs=pl.BlockSpec((1, nb, tn), lambda l, j: (l, 0, j)),
        out_shape=jax.ShapeDtypeStruct((depth, nb, n), F32),
        compiler_params=_cparams("parallel", "parallel"),
        name="ada_modulation",
    )(c_all, w_ada, b_ada.reshape(depth, 1, n))


def _in_proj_kernel(x_ref, mod_ref, nw_ref, w_ref, wg_ref, proj_ref, u_ref, gate_ref):
    h = _rms(x_ref[...]) * nw_ref[...] * (1.0 + mod_ref[0, 1:2, :]) + mod_ref[0, 0:1, :]
    hb = h.astype(BF16)
    proj = _dot(hb, w_ref[...]).astype(proj_ref.dtype)
    proj_ref[...] = proj
    u_ref[...] = proj[:, :S5_WIDTH]
    gate_ref[0] = _dot_nt(wg_ref[...], hb)


def in_projection(x2d, mod, norm_w, w_main, w_gate, seq):
    m, d = x2d.shape
    tm = ROW_TILE
    per_seq = seq // tm
    nb = m // seq
    return pl.pallas_call(
        _in_proj_kernel,
        grid=(m // tm,),
        in_specs=[pl.BlockSpec((tm, d), lambda i: (i, 0)),
                  pl.BlockSpec((1, 6, d), lambda i: (i // per_seq, 0, 0)),
                  pl.BlockSpec((1, d), lambda i: (0, 0)),
                  pl.BlockSpec((d, PROJ_MAIN), lambda i: (0, 0)),
                  pl.BlockSpec((N_GATE_COLS, d), lambda i: (0, 0))],
        out_specs=[pl.BlockSpec((tm, PROJ_MAIN), lambda i: (i, 0)),
                   pl.BlockSpec((tm, S5_WIDTH), lambda i: (i % per_seq, i // per_seq)),
                   pl.BlockSpec((1, N_GATE_COLS, tm), lambda i: (i // per_seq, 0, i % per_seq))],
        out_shape=[jax.ShapeDtypeStruct((m, PROJ_MAIN), BF16),
                   jax.ShapeDtypeStruct((seq, nb * S5_WIDTH), BF16),
                   jax.ShapeDtypeStruct((nb, N_GATE_COLS, seq), F32)],
        compiler_params=_cparams("parallel"),
        name="in_projection",
    )(x2d, mod, norm_w, w_main, w_gate)


def _s5_scan(x_ref, h0_re, h0_im, a_re, a_im, steps, reverse):
    def body(s, carry):
        hr, hi = carry
        t = (steps - 1 - s) if reverse else s
        rows = pl.ds(pl.multiple_of(t * SUBLANES, SUBLANES), SUBLANES)
        xr = x_ref[rows, 0:S5_LANES]
        xi = x_ref[rows, S5_LANES:2 * S5_LANES]
        nr = a_re * hr - a_im * hi + xr
        ni = a_re * hi + a_im * hr + xi
        x_ref[rows, 0:S5_LANES] = nr
        x_ref[rows, S5_LANES:2 * S5_LANES] = ni
        return nr, ni
    return lax.fori_loop(0, steps, body, (h0_re, h0_im), unroll=4)


def _s5_direction(u_ref, a_ref, bmat_ref, cmat_ref, x_ref, st_ref, reverse):
    tc = u_ref.shape[0]
    w = S5_WIDTH
    nb = u_ref.shape[1] // w
    first = pl.program_id(0) == 0

    @pl.when(first)
    def _():
        st_ref[...] = jnp.zeros_like(st_ref)

    u = u_ref[...].astype(F32).reshape(tc * nb, w)
    x_ref[...] = _dot(u.astype(BF16), bmat_ref[...])
    a_re = jnp.broadcast_to(a_ref[0:1, :], (nb, S5_LANES))
    a_im = jnp.broadcast_to(a_ref[1:2, :], (nb, S5_LANES))
    hr, hi = _s5_scan(x_ref, st_ref[0], st_ref[1], a_re, a_im, tc, reverse)
    st_ref[0] = hr
    st_ref[1] = hi
    return u, _dot(x_ref[...].astype(BF16), cmat_ref[...])


def _s5_fwd_kernel(u_ref, a_ref, bmat_ref, cmat_ref, y_ref, x_ref, st_ref):
    _, y = _s5_direction(u_ref, a_ref, bmat_ref, cmat_ref, x_ref, st_ref, reverse=False)
    y_ref[...] = y.reshape(y_ref.shape)


def _s5_bwd_kernel(u_ref, yf_ref, a_ref, bmat_ref, cmat_ref, d_ref, wglu_ref, bglu_ref, o_ref, x_ref, st_ref):
    u, y = _s5_direction(u_ref, a_ref, bmat_ref, cmat_ref, x_ref, st_ref, reverse=True)
    y = y + u * d_ref[...] + yf_ref[...].reshape(u.shape)
    y = _gelu_tanh(y)
    out = y * _sigmoid(_dot(y.astype(BF16), wglu_ref[...]) + bglu_ref[...])
    o_ref[...] = out.reshape(o_ref.shape).astype(o_ref.dtype)


def s5_mixer(u_t, a_f, bmat_f, cmat_f, a_b, bmat_b, cmat_b, d_skip, w_glu, b_glu):
    seq = u_t.shape[0]
    w = S5_WIDTH
    nb = u_t.shape[1] // w
    assert nb == SUBLANES, "the scan keeps one batch row per sublane"
    tc = S5_TIME_TILE
    n = seq // tc
    rows = tc * nb
    const2 = lambda i: (0, 0)
    par_specs = [pl.BlockSpec((2, S5_LANES), const2),
                 pl.BlockSpec((w, 2 * S5_LANES), const2),
                 pl.BlockSpec((2 * S5_LANES, w), const2)]
    scratch = [pltpu.VMEM((rows, 2 * S5_LANES), F32), pltpu.VMEM((2, nb, S5_LANES), F32)]
    y_f = pl.pallas_call(
        _s5_fwd_kernel,
        grid=(n,),
        in_specs=[pl.BlockSpec((tc, nb * w), lambda i: (i, 0))] + par_specs,
        out_specs=pl.BlockSpec((tc, nb, w), lambda i: (i, 0, 0)),
        out_shape=jax.ShapeDtypeStruct((seq, nb, w), F32),
        scratch_shapes=scratch,
        compiler_params=_cparams("arbitrary"),
        name="s5_forward",
    )(u_t, a_f, bmat_f, cmat_f)
    rev = lambda i: (n - 1 - i, 0, 0)
    rev2 = lambda i: (n - 1 - i, 0)
    return pl.pallas_call(
        _s5_bwd_kernel,
        grid=(n,),
        in_specs=[pl.BlockSpec((tc, nb * w), rev2), pl.BlockSpec((tc, nb, w), rev)] + par_specs
                 + [pl.BlockSpec((1, w), const2), pl.BlockSpec((w, w), const2), pl.BlockSpec((1, w), const2)],
        out_specs=pl.BlockSpec((tc, nb * w), rev2),
        out_shape=jax.ShapeDtypeStruct((seq, nb * w), BF16),
        scratch_shapes=scratch,
        compiler_params=_cparams("arbitrary"),
        name="s5_backward_glu",
    )(u_t, y_f, a_b, bmat_b, cmat_b, d_skip, w_glu, b_glu)


def s5_discretise(lam_re, lam_im, log_step, b_re, b_im, c_re, c_im):
    g, p, h = S5_GROUPS, S5_STATE, S5_GROUP
    lr, li = lam_re.astype(F32), lam_im.astype(F32)
    dt = jnp.exp(log_step.astype(F32))[:, None]
    mag = jnp.exp(lr * dt)
    abr, abi = mag * jnp.cos(li * dt), mag * jnp.sin(li * dt)
    den = lr * lr + li * li
    cr = ((abr - 1.0) * lr + abi * li) / den
    ci = (abi * lr - (abr - 1.0) * li) / den
    bbr = cr[..., None] * b_re - ci[..., None] * b_im
    bbi = cr[..., None] * b_im + ci[..., None] * b_re
    eye = jnp.eye(g, dtype=F32)
    to_b = lambda t: jnp.einsum('gph,gk->ghkp', t, eye).reshape(g * h, g * p)
    bmat = jnp.concatenate([to_b(bbr), to_b(bbi)], axis=1)
    to_c = lambda t: jnp.einsum('ghp,gk->gpkh', t, eye).reshape(g * p, g * h)
    cmat = jnp.concatenate([to_c(c_re.astype(F32)), -to_c(c_im.astype(F32))], axis=0)
    a = jnp.stack([abr.reshape(g * p), abi.reshape(g * p)])
    return a, bmat.astype(BF16), cmat.astype(BF16)


def _conv3_rows(x, prev_row, next_row, w_ref):
    tm = x.shape[0]
    rows = lax.broadcasted_iota(jnp.int32, x.shape, 0)
    xp = jnp.where(rows == 0, prev_row, pltpu.roll(x, 1, 0))
    xn = jnp.where(rows == tm - 1, next_row, pltpu.roll(x, tm - 1, 0))
    return xp * w_ref[0:1, :] + x * w_ref[1:2, :] + xn * w_ref[2:3, :]


HALO_ROWS = 16


def _halo_specs(tm, width, col_block, n_rows):
    per = tm // HALO_ROWS
    last = n_rows // HALO_ROWS - 1
    prev = pl.BlockSpec((HALO_ROWS, width), lambda i, *_: (jnp.maximum(i * per - 1, 0), col_block(*_)))
    nxt = pl.BlockSpec((HALO_ROWS, width), lambda i, *_: (jnp.minimum((i + 1) * per, last), col_block(*_)))
    return prev, nxt


def _gdn_prep_kernel(x_ref, xp_ref, xn_ref, w_ref, o_ref, *, seq):
    tm = x_ref.shape[0]
    j = pl.program_id(1)
    t0 = (pl.program_id(0) * tm) % seq
    prev_row = jnp.where(t0 == 0, 0.0, xp_ref[HALO_ROWS - 1:HALO_ROWS, :].astype(F32))
    next_row = jnp.where(t0 + tm == seq, 0.0, xn_ref[0:1, :].astype(F32))
    y = _silu(_conv3_rows(x_ref[...].astype(F32), prev_row, next_row, w_ref))

    @pl.when(j == 2)
    def _():
        o_ref[...] = y.astype(o_ref.dtype)

    @pl.when(j < 2)
    def _():
        scale = jnp.where(j == 0, GDN_HEAD_DIM ** -0.5, 1.0)
        for h in range(GDN_HEADS):
            cols = slice(h * GDN_HEAD_DIM, (h + 1) * GDN_HEAD_DIM)
            yh = y[:, cols]
            yh = yh * (lax.rsqrt(jnp.sum(yh * yh, axis=-1, keepdims=True) + EPS) * scale)
            o_ref[:, cols] = yh.astype(o_ref.dtype)


def gdn_prepare(proj, conv_w, seq):
    m = proj.shape[0]
    tm = ROW_TILE
    first = (S5_WIDTH + 3 * CONV_WIDTH) // GDN_WIDTH
    prev, nxt = _halo_specs(tm, GDN_WIDTH, lambda j: first + j, m)
    return pl.pallas_call(
        functools.partial(_gdn_prep_kernel, seq=seq),
        grid=(m // tm, 3),
        in_specs=[pl.BlockSpec((tm, GDN_WIDTH), lambda i, j: (i, first + j)), prev, nxt,
                  pl.BlockSpec((3, GDN_WIDTH), lambda i, j: (0, j))],
        out_specs=pl.BlockSpec((tm, GDN_WIDTH), lambda i, j: (i, j)),
        out_shape=jax.ShapeDtypeStruct((m, 3 * GDN_WIDTH), BF16),
        compiler_params=_cparams("parallel", "parallel"),
        name="gdn_prepare",
    )(proj, proj, proj, conv_w)


def _gdn_gates_kernel(pre_ref, alog_ref, dtb_ref, o_ref):
    seq = pre_ref.shape[2]
    nh2 = 2 * GDN_HEADS
    g = -jnp.exp(alog_ref[...]) * _softplus(pre_ref[0, 0:nh2, :] + dtb_ref[...])
    o_ref[0, nh2:2 * nh2, :] = _sigmoid(pre_ref[0, nh2:2 * nh2, :])
    s = lax.broadcasted_iota(jnp.int32, (LANES, LANES), 0)
    t = lax.broadcasted_iota(jnp.int32, (LANES, LANES), 1)
    same = (s // GDN_CHUNK) == (t // GDN_CHUNK)
    m_f = jnp.where(same & (s <= t), 1.0, 0.0)
    m_b = jnp.where(same & (s >= t), 1.0, 0.0)
    for jb in range(seq // LANES):
        cols = slice(jb * LANES, (jb + 1) * LANES)
        gb = g[:, cols]
        o_ref[0, 0:GDN_HEADS, cols] = _dot(gb, m_f, precision=HIGHEST)[0:GDN_HEADS]
        o_ref[0, GDN_HEADS:nh2, cols] = _dot(gb, m_b, precision=HIGHEST)[GDN_HEADS:nh2]


def gdn_gates(pre_t, a_log, dt_bias):
    nb, rows, seq = pre_t.shape
    col = lambda t: t.astype(F32).reshape(2 * GDN_HEADS, 1)
    return pl.pallas_call(
        _gdn_gates_kernel,
        grid=(nb,),
        in_specs=[pl.BlockSpec((1, rows, seq), lambda b: (b, 0, 0)),
                  pl.BlockSpec((2 * GDN_HEADS, 1), lambda b: (0, 0)),
                  pl.BlockSpec((2 * GDN_HEADS, 1), lambda b: (0, 0))],
        out_specs=pl.BlockSpec((1, rows, seq), lambda b: (b, 0, 0)),
        out_shape=jax.ShapeDtypeStruct((nb, rows, seq), F32),
        compiler_params=_cparams("parallel"),
        name="gdn_gates",
    )(pre_t, col(a_log), col(dt_bias))


def _nilpotent_inverses(mats, nilpotency):
    n = mats[0].shape[0]
    eye = jnp.where(lax.broadcasted_iota(jnp.int32, (n, n), 0) == lax.broadcasted_iota(jnp.int32, (n, n), 1), 1.0, 0.0)
    ps = [eye - a for a in mats]
    a16 = [a.astype(BF16) for a in mats]
    sqs = [_dot(a, a) for a in a16]
    k = 2
    while True:
        sq16 = [s.astype(BF16) for s in sqs]
        ps = [p + _dot(p.astype(BF16), s) for p, s in zip(ps, sq16)]
        k *= 2
        if k >= nilpotency:
            return ps
        sqs = [_dot(s, s) for s in sq16]


def _gdn_local_kernel(q_ref, k_ref, v_ref, gcol_ref, grow_ref, uwqf_ref, uwqb_ref, ktf_ref, ktb_ref, in_ref, eg_ref):
    r = q_ref.shape[1]
    c = GDN_CHUNK
    hd = GDN_HEAD_DIM
    ii = lax.broadcasted_iota(jnp.int32, (r, r), 0)
    jj = lax.broadcasted_iota(jnp.int32, (r, r), 1)
    same = (ii >= c) == (jj >= c)
    incl = (same & (ii >= jj), same & (ii <= jj))
    strict = (same & (ii > jj), same & (ii < jj))
    first_chunk = lax.broadcasted_iota(jnp.int32, (r, 1), 0) < c
    zeros = jnp.zeros((r, r), F32)
    big_a, rhs, keep = [], [], []
    for h in range(GDN_HEADS):
        cols = slice(h * hd, (h + 1) * hd)
        q16, k16 = q_ref[0, :, cols], k_ref[0, :, cols]
        q, k, v = q16.astype(F32), k16.astype(F32), v_ref[0, :, cols].astype(F32)
        kk = _dot_nt(k16, k16)
        qk = _dot_nt(q16, k16)
        a_dir, rhs_dir = [], []
        for d in range(2):
            lane = d * GDN_HEADS + h
            gc_col = gcol_ref[0, :, lane:lane + 1]
            gc_row = grow_ref[0, lane:lane + 1, :]
            beta = gcol_ref[0, :, 2 * GDN_HEADS + lane:2 * GDN_HEADS + lane + 1]
            decay = jnp.exp(jnp.where(incl[d], gc_col - gc_row, -1e30))
            a_dir.append(jnp.where(strict[d], kk * beta * decay, 0.0))
            intra = jnp.where(incl[d], qk * decay, 0.0)
            ends = (gc_col[c - 1:c], gc_col[2 * c - 1:2 * c]) if d == 0 else (gc_col[0:1], gc_col[c:c + 1])
            g_last = jnp.where(first_chunk, ends[0], ends[1])
            eg = jnp.exp(gc_col)
            kb = k * beta
            rhs_dir.append(jnp.concatenate([v * beta, kb * eg], axis=1))
            keep.append((q * eg, (k * jnp.exp(g_last - gc_col)).T, intra[:, :c] + intra[:, c:]))
            for ch in range(2):
                eg_ref[0, ch, lane:lane + 1, :] = jnp.broadcast_to(jnp.exp(ends[ch]), (1, LANES))
        big_a.append(jnp.concatenate([jnp.concatenate([a_dir[0], zeros], axis=1),
                                      jnp.concatenate([zeros, a_dir[1]], axis=1)], axis=0))
        rhs.append(jnp.concatenate(rhs_dir, axis=0))
    t_inv = _nilpotent_inverses(big_a, c)
    for h in range(GDN_HEADS):
        uw = _dot(t_inv[h].astype(BF16), rhs[h].astype(BF16))
        (qd_f, kt_f, in_f), (qd_b, kt_b, in_b) = keep[2 * h], keep[2 * h + 1]
        uwqf_ref[0, h] = jnp.concatenate([uw[:r], qd_f], axis=1).astype(BF16)
        uwqb_ref[0, h] = jnp.concatenate([uw[r:], qd_b], axis=1).astype(BF16)
        ktf_ref[0, h] = kt_f.astype(BF16)
        ktb_ref[0, h] = kt_b.astype(BF16)
        in_ref[0, h] = jnp.concatenate([in_f, in_b], axis=1).astype(BF16)


def gdn_local(qkv, gates_col, gates_row):
    nb, seq, _ = qkv.shape
    r = 2 * GDN_CHUNK
    hd, nh = GDN_HEAD_DIM, GDN_HEADS
    uwq_sd = jax.ShapeDtypeStruct((nb, nh, seq, 3 * hd), BF16)
    kt_sd = jax.ShapeDtypeStruct((nb, nh, hd, seq), BF16)
    uwq_spec = pl.BlockSpec((1, nh, r, 3 * hd), lambda b, i: (b, 0, i, 0))
    kt_spec = pl.BlockSpec((1, nh, hd, r), lambda b, i: (b, 0, 0, i))
    return pl.pallas_call(
        _gdn_local_kernel,
        grid=(nb, seq // r),
        in_specs=[pl.BlockSpec((1, r, GDN_WIDTH), lambda b, i: (b, i, 0)),
                  pl.BlockSpec((1, r, GDN_WIDTH), lambda b, i: (b, i, 1)),
                  pl.BlockSpec((1, r, GDN_WIDTH), lambda b, i: (b, i, 2)),
                  pl.BlockSpec((1, r, 4 * nh), lambda b, i: (b, i, 0)),
                  pl.BlockSpec((1, 4 * nh, r), lambda b, i: (b, 0, i))],
        out_specs=[uwq_spec, uwq_spec, kt_spec, kt_spec,
                   pl.BlockSpec((1, nh, r, 2 * GDN_CHUNK), lambda b, i: (b, 0, i, 0)),
                   pl.BlockSpec((1, 2, SUBLANES, LANES), lambda b, i: (b, i, 0, 0))],
        out_shape=[uwq_sd, uwq_sd, kt_sd, kt_sd,
                   jax.ShapeDtypeStruct((nb, nh, seq, 2 * GDN_CHUNK), BF16),
                   jax.ShapeDtypeStruct((nb, seq // GDN_CHUNK, SUBLANES, LANES), F32)],
        compiler_params=_cparams("parallel", "parallel"),
        name="gdn_local",
    )(qkv, qkv, qkv, gates_col, gates_row)


def _gdn_state_kernel(uwqf, ktf, inf, egf, uwqb, ktb, inb, egb, of_ref, ob_ref, st_ref):
    @pl.when(pl.program_id(1) == 0)
    def _():
        st_ref[...] = jnp.zeros_like(st_ref)

    c = GDN_CHUNK
    hd = GDN_HEAD_DIM
    nc = uwqf.shape[2] // c
    chains = [(d, h) for d in range(2) for h in range(GDN_HEADS)]
    refs = ((uwqf, ktf, inf, egf, of_ref), (uwqb, ktb, inb, egb, ob_ref))
    states = [st_ref[j] for j in range(len(chains))]
    for step in range(nc):
        sws = []
        for j, (d, h) in enumerate(chains):
            ci = step if d == 0 else nc - 1 - step
            rows = slice(ci * c, (ci + 1) * c)
            uwq = refs[d][0]
            wq = jnp.concatenate([uwq[0, h, rows, hd:2 * hd], uwq[0, h, rows, 2 * hd:3 * hd]], axis=0)
            sws.append(_dot(wq, states[j].astype(BF16)))
        for j, (d, h) in enumerate(chains):
            ci = step if d == 0 else nc - 1 - step
            rows = slice(ci * c, (ci + 1) * c)
            uwq, kt, intra, eg, o_ref = refs[d]
            sw = sws[j]
            v16 = (uwq[0, h, rows, 0:hd].astype(F32) - sw[:c]).astype(BF16)
            out = sw[c:] + _dot(intra[0, h, rows, d * c:(d + 1) * c], v16)
            o_ref[0, rows, h * hd:(h + 1) * hd] = out.astype(o_ref.dtype)
            lane = d * GDN_HEADS + h
            states[j] = states[j] * eg[0, ci, lane:lane + 1, :] + _dot(kt[0, h, :, rows], v16)
    for j in range(len(chains)):
        st_ref[j] = states[j]


def gdn_state_scan(uwq_f, uwq_b, kt_f, kt_b, intra, eg):
    nb, nh, seq, _ = uwq_f.shape
    hd = GDN_HEAD_DIM
    nc = GDN_STATE_CHUNKS
    r = nc * GDN_CHUNK
    n = seq // r

    def specs(pos):
        return [pl.BlockSpec((1, nh, r, 3 * hd), lambda b, i: (b, 0, pos(i), 0)),
                pl.BlockSpec((1, nh, hd, r), lambda b, i: (b, 0, 0, pos(i))),
                pl.BlockSpec((1, nh, r, 2 * GDN_CHUNK), lambda b, i: (b, 0, pos(i), 0)),
                pl.BlockSpec((1, nc, SUBLANES, LANES), lambda b, i: (b, pos(i), 0, 0))]

    fwd = lambda i: i
    bwd = lambda i: n - 1 - i
    out_sd = jax.ShapeDtypeStruct((nb, seq, GDN_WIDTH), BF16)
    return pl.pallas_call(
        _gdn_state_kernel,
        grid=(nb, n),
        in_specs=specs(fwd) + specs(bwd),
        out_specs=[pl.BlockSpec((1, r, GDN_WIDTH), lambda b, i: (b, i, 0)),
                   pl.BlockSpec((1, r, GDN_WIDTH), lambda b, i: (b, n - 1 - i, 0))],
        out_shape=[out_sd, out_sd],
        scratch_shapes=[pltpu.VMEM((2 * nh, hd, hd), F32)],
        compiler_params=_cparams("parallel", "arbitrary"),
        name="gdn_state_scan",
    )(uwq_f, kt_f, intra, eg, uwq_b, kt_b, intra, eg)


def _out_proj_kernel(x_ref, mod_ref, ys5_ref, cx_ref, cb_ref, cc_ref, cxp_ref, cxn_ref, ccp_ref, ccn_ref,
                     cw_ref, of_ref, ob_ref, z_ref, gnw_ref, w_ref, o_ref, *, seq):
    tm = x_ref.shape[0]
    t0 = (pl.program_id(0) * tm) % seq
    f32 = lambda ref, rows=slice(None): ref[rows, :].astype(F32)
    last = slice(HALO_ROWS - 1, HALO_ROWS)
    prev_row = jnp.where(t0 == 0, 0.0, f32(ccp_ref, last) * f32(cxp_ref, last))
    next_row = jnp.where(t0 + tm == seq, 0.0, f32(ccn_ref, slice(0, 1)) * f32(cxn_ref, slice(0, 1)))
    y_conv = f32(cb_ref) * _conv3_rows(f32(cc_ref) * f32(cx_ref), prev_row, next_row, cw_ref)
    o = f32(of_ref) + f32(ob_ref)
    z = f32(z_ref)
    heads = []
    for h in range(GDN_HEADS):
        cols = slice(h * GDN_HEAD_DIM, (h + 1) * GDN_HEAD_DIM)
        heads.append(_rms(o[:, cols]) * gnw_ref[...] * _silu(z[:, cols]))
    y_gdn = jnp.concatenate(heads, axis=1)
    c0, c1 = S5_WIDTH, S5_WIDTH + CONV_WIDTH
    mix = (_dot(ys5_ref[...].astype(BF16), w_ref[0:c0, :])
           + _dot(y_conv.astype(BF16), w_ref[c0:c1, :])
           + _dot(y_gdn.astype(BF16), w_ref[c1:, :]))
    o_ref[...] = x_ref[...] + mod_ref[0, 2:3, :] * mix


def out_projection(x2d, mod, y_s5, proj, conv_w, o_f, o_b, gdn_norm_w, w_out, seq):
    m, d = x2d.shape
    tm = ROW_TILE
    per_seq = seq // tm
    cw = CONV_WIDTH
    row = lambda c: pl.BlockSpec((tm, cw), lambda i: (i, c))
    cxp, cxn = _halo_specs(tm, cw, lambda: 1, m)
    ccp, ccn = _halo_specs(tm, cw, lambda: 3, m)
    gw = GDN_WIDTH
    return pl.pallas_call(
        functools.partial(_out_proj_kernel, seq=seq),
        grid=(m // tm,),
        in_specs=[pl.BlockSpec((tm, d), lambda i: (i, 0)),
                  pl.BlockSpec((1, 6, d), lambda i: (i // per_seq, 0, 0)),
                  pl.BlockSpec((tm, S5_WIDTH), lambda i: (i % per_seq, i // per_seq)),
                  row(1), row(2), row(3), cxp, cxn, ccp, ccn,
                  pl.BlockSpec((3, cw), lambda i: (0, 0)),
                  pl.BlockSpec((tm, gw), lambda i: (i, 0)),
                  pl.BlockSpec((tm, gw), lambda i: (i, 0)),
                  pl.BlockSpec((tm, gw), lambda i: (i, (PROJ_MAIN - gw) // gw)),
                  pl.BlockSpec((1, GDN_HEAD_DIM), lambda i: (0, 0)),
                  pl.BlockSpec((d, d), lambda i: (0, 0))],
        out_specs=pl.BlockSpec((tm, d), lambda i: (i, 0)),
        out_shape=jax.ShapeDtypeStruct((m, d), F32),
        compiler_params=_cparams("parallel"),
        name="out_projection",
    )(x2d, mod, y_s5, proj, proj, proj, proj, proj, proj, proj, conv_w, o_f, o_b, proj, gdn_norm_w, w_out)


def _first_argmax(vals, rows, n):
    m = jnp.max(vals, axis=0, keepdims=True)
    return m, jnp.min(jnp.where(vals == m, rows, n), axis=0, keepdims=True)


def _route_kernel(x_ref, mod_ref, nw_ref, wr_ref, br_ref, h_ref, idx_ref, wgt_ref, cnt_ref, carry_ref, before_ref):
    step = pl.program_id(0)
    tm = x_ref.shape[0]

    @pl.when(step == 0)
    def _():
        carry_ref[...] = jnp.zeros_like(carry_ref)
        earlier = lax.broadcasted_iota(jnp.int32, (tm, tm), 0) < lax.broadcasted_iota(jnp.int32, (tm, tm), 1)
        before_ref[...] = jnp.where(earlier, 1.0, 0.0).astype(BF16)

    h = _rms(x_ref[...]) * nw_ref[...] * (1.0 + mod_ref[0, 4:5, :]) + mod_ref[0, 3:4, :]
    h_ref[...] = h.reshape(h_ref.shape)
    nr = br_ref.shape[0]
    h_hi = h.astype(BF16)
    h_lo = (h - h_hi.astype(F32)).astype(BF16)
    head = _dot_nt(wr_ref[...], h_hi)
    logits = head[:nr] + head[nr:] + _dot_nt(wr_ref[0:nr, :], h_lo) + br_ref[...]
    lg = logits[N_EXPERTS:N_EXPERTS + MOE_GROUPS, :]
    rows_g = lax.broadcasted_iota(jnp.int32, lg.shape, 0)
    g_max, g_idx = _first_argmax(lg, rows_g, MOE_GROUPS)
    g_w = 1.0 / jnp.sum(jnp.exp(lg - g_max), axis=0, keepdims=True)
    le = jnp.zeros((EXPERTS_PER_GROUP, tm), F32)
    for g in range(MOE_GROUPS):
        le = jnp.where(g_idx == g, logits[g * EXPERTS_PER_GROUP:(g + 1) * EXPERTS_PER_GROUP, :], le)
    rows_e = lax.broadcasted_iota(jnp.int32, le.shape, 0)
    m1, i1 = _first_argmax(le, rows_e, EXPERTS_PER_GROUP)
    m2, i2 = _first_argmax(jnp.where(rows_e == i1, -jnp.inf, le), rows_e, EXPERTS_PER_GROUP)
    r = jnp.exp(m2 - m1)
    w1 = g_w / (1.0 + r)
    e1 = g_idx * EXPERTS_PER_GROUP + i1
    e2 = g_idx * EXPERTS_PER_GROUP + i2
    rows_x = lax.broadcasted_iota(jnp.int32, (N_EXPERTS, tm), 0)
    hit1 = rows_x == e1
    hit2 = rows_x == e2
    onehot = jnp.where(hit1 | hit2, 1.0, 0.0)
    ahead = _dot(onehot.astype(BF16), before_ref[...]) + carry_ref[:, 0:1]
    rank1 = jnp.sum(jnp.where(hit1, ahead, 0.0), axis=0, keepdims=True)
    rank2 = jnp.sum(jnp.where(hit2, ahead, 0.0), axis=0, keepdims=True)
    idx_ref[0, 0:1, :] = e1
    idx_ref[0, 1:2, :] = e2
    idx_ref[0, 2:3, :] = rank1.astype(jnp.int32)
    idx_ref[0, 3:4, :] = rank2.astype(jnp.int32)
    wgt_ref[0:1, :] = w1
    wgt_ref[1:2, :] = w1 * r
    carry_ref[...] = carry_ref[...] + jnp.sum(onehot, axis=1, keepdims=True)
    cnt_ref[...] = carry_ref[...]


def moe_route(x2d, mod, norm_w, w_router_t, b_router, seq):
    m, d = x2d.shape
    tm = ROUTE_TILE
    per_seq = seq // tm
    nr = w_router_t.shape[0]
    return pl.pallas_call(
        _route_kernel,
        grid=(m // tm,),
        in_specs=[pl.BlockSpec((tm, d), lambda i: (i, 0)),
                  pl.BlockSpec((1, 6, d), lambda i: (i // per_seq, 0, 0)),
                  pl.BlockSpec((1, d), lambda i: (0, 0)),
                  pl.BlockSpec((nr, d), lambda i: (0, 0)),
                  pl.BlockSpec((nr // 2, 1), lambda i: (0, 0))],
        out_specs=[pl.BlockSpec((tm,) + SLAB, lambda i: (i, 0, 0)),
                   pl.BlockSpec((1, 4, tm), lambda i: (i, 0, 0)),
                   pl.BlockSpec((2, tm), lambda i: (0, i)),
                   pl.BlockSpec((N_EXPERTS, LANES), lambda i: (0, 0))],
        out_shape=[jax.ShapeDtypeStruct((m,) + SLAB, F32),
                   jax.ShapeDtypeStruct((m // tm, 4, tm), jnp.int32),
                   jax.ShapeDtypeStruct((2, m), F32),
                   jax.ShapeDtypeStruct((N_EXPERTS, LANES), F32)],
        scratch_shapes=[pltpu.VMEM((N_EXPERTS, LANES), F32), pltpu.VMEM((tm, tm), BF16)],
        compiler_params=_cparams("arbitrary"),
        name="moe_route",
    )(x2d, mod, norm_w, w_router_t, b_router)


def _row_copy(src_ref, src_row, dst_ref, dst_row, sem):
    return pltpu.make_async_copy(src_ref.at[pl.ds(src_row, 1)], dst_ref.at[pl.ds(dst_row, 1)], sem)


def _scatter_kernel(last_ref, start_ref, idx_ref, h_ref, xs_ref, zero_ref, sem):
    tm = h_ref.shape[0]

    @pl.when(pl.program_id(0) == 0)
    def _():
        zero_ref[...] = jnp.zeros_like(zero_ref)

        def fill(e, c):
            @pl.when(last_ref[e] >= 0)
            def _():
                pltpu.make_async_copy(zero_ref, xs_ref.at[pl.ds(last_ref[e], MOE_BLOCK)], sem).start()
            return c

        def drain(e, c):
            @pl.when(last_ref[e] >= 0)
            def _():
                pltpu.make_async_copy(zero_ref, xs_ref.at[pl.ds(0, MOE_BLOCK)], sem).wait()
            return c

        lax.fori_loop(0, N_EXPERTS, fill, 0)
        lax.fori_loop(0, N_EXPERTS, drain, 0)

        n_blocks = xs_ref.shape[0] // MOE_BLOCK

        def fill_unused(b, c):
            pltpu.make_async_copy(zero_ref, xs_ref.at[pl.ds(b * MOE_BLOCK, MOE_BLOCK)], sem).start()
            return c

        def drain_unused(b, c):
            pltpu.make_async_copy(zero_ref, xs_ref.at[pl.ds(0, MOE_BLOCK)], sem).wait()
            return c

        lax.fori_loop(last_ref[N_EXPERTS], n_blocks, fill_unused, 0)
        lax.fori_loop(last_ref[N_EXPERTS], n_blocks, drain_unused, 0)

    def start(r, c):
        _row_copy(h_ref, r, xs_ref, _slot(start_ref, idx_ref, 0, r), sem).start(priority=0)
        _row_copy(h_ref, r, xs_ref, _slot(start_ref, idx_ref, 1, r), sem).start(priority=1)
        return c

    def wait(r, c):
        _row_copy(h_ref, 0, xs_ref, 0, sem).wait()
        _row_copy(h_ref, 0, xs_ref, 0, sem).wait()
        return c

    lax.fori_loop(0, tm, start, 0, unroll=DMA_UNROLL)
    lax.fori_loop(0, tm, wait, 0, unroll=DMA_UNROLL)


def _slot(start_ref, idx_ref, k, r):
    return start_ref[idx_ref[0, k, r]] + idx_ref[0, 2 + k, r]


def moe_scatter(h_slab, idx, pad_start, last_block_start, n_slots):
    m = h_slab.shape[0]
    tm = ROUTE_TILE
    return pl.pallas_call(
        _scatter_kernel,
        grid_spec=pltpu.PrefetchScalarGridSpec(
            num_scalar_prefetch=2,
            grid=(m // tm,),
            in_specs=[pl.BlockSpec((1, 4, tm), lambda i, last, ps: (i, 0, 0), memory_space=pltpu.SMEM),
                      pl.BlockSpec((tm,) + SLAB, lambda i, last, ps: (i, 0, 0))],
            out_specs=pl.BlockSpec(memory_space=pl.ANY),
            scratch_shapes=[pltpu.VMEM((MOE_BLOCK,) + SLAB, F32), pltpu.SemaphoreType.DMA(())]),
        out_shape=jax.ShapeDtypeStruct((n_slots,) + SLAB, F32),
        compiler_params=_cparams("arbitrary"),
        name="moe_scatter",
    )(last_block_start, pad_start, idx, h_slab)


def _expert_kernel(be_ref, nused_ref, xs_ref, wgu_ref, wd_ref, y_ref, wgu16_ref, wd16_ref):
    i = pl.program_id(0)
    changed = jnp.logical_or(i == 0, be_ref[i] != be_ref[jnp.maximum(i - 1, 0)])

    @pl.when(changed)
    def _():
        wgu16_ref[...] = wgu_ref[0].astype(BF16)
        wd16_ref[...] = wd_ref[0].astype(BF16)

    @pl.when(i < nused_ref[0])
    def _():
        x = xs_ref[...].reshape(MOE_BLOCK, D_MODEL).astype(BF16)
        gu = _dot(x, wgu16_ref[...])
        act = _silu(gu[:, :EXPERT_FF]) * gu[:, EXPERT_FF:]
        y = _dot(act.astype(BF16), wd16_ref[...])
        y_ref[...] = y.reshape(y_ref.shape)

    @pl.when(i >= nused_ref[0])
    def _():
        y_ref[...] = jnp.zeros_like(y_ref)


def moe_experts(xs, blk_expert, n_used, w_gate_up, w_down):
    n_slots = xs.shape[0]
    n_blocks = n_slots // MOE_BLOCK
    d = D_MODEL
    return pl.pallas_call(
        _expert_kernel,
        grid_spec=pltpu.PrefetchScalarGridSpec(
            num_scalar_prefetch=2,
            grid=(n_blocks,),
            in_specs=[pl.BlockSpec((MOE_BLOCK,) + SLAB, lambda i, be, nu: (jnp.minimum(i, nu[0] - 1), 0, 0)),
                      pl.BlockSpec((1, d, 2 * EXPERT_FF), lambda i, be, nu: (be[i], 0, 0)),
                      pl.BlockSpec((1, EXPERT_FF, d), lambda i, be, nu: (be[i], 0, 0))],
            out_specs=pl.BlockSpec((MOE_BLOCK,) + SLAB, lambda i, be, nu: (i, 0, 0)),
            scratch_shapes=[pltpu.VMEM((d, 2 * EXPERT_FF), BF16), pltpu.VMEM((EXPERT_FF, d), BF16)]),
        out_shape=jax.ShapeDtypeStruct((n_slots,) + SLAB, F32),
        compiler_params=_cparams("arbitrary"),
        name="moe_experts",
    )(blk_expert, n_used, xs, w_gate_up, w_down)


def _combine_kernel(start_ref, idx_ref, x_ref, mod_ref, wgt_ref, fnw_ref, yb_ref, o_ref, y1_ref, y2_ref, sem, *,
                    final):
    tm = x_ref.shape[0]

    def start(r, c):
        _row_copy(yb_ref, _slot(start_ref, idx_ref, 0, r), y1_ref, r, sem).start(priority=0)
        _row_copy(yb_ref, _slot(start_ref, idx_ref, 1, r), y2_ref, r, sem).start(priority=1)
        return c

    def wait(r, c):
        _row_copy(yb_ref, 0, y1_ref, 0, sem).wait()
        _row_copy(yb_ref, 0, y2_ref, 0, sem).wait()
        return c

    lax.fori_loop(0, tm, start, 0, unroll=DMA_UNROLL)
    lax.fori_loop(0, tm, wait, 0, unroll=DMA_UNROLL)
    w1 = wgt_ref[:, 0:1]
    w2 = wgt_ref[:, 1:2]
    y = y1_ref[...].reshape(x_ref.shape) * w1 + y2_ref[...].reshape(x_ref.shape) * w2
    o_ref[...] = x_ref[...] + mod_ref[0, 5:6, :] * y
    if final:
        o_ref[...] = _rms(o_ref[...]) * fnw_ref[...]


def moe_combine(x2d, mod, yb, idx, pad_start, weights, final_norm_w, seq, final):
    m, d = x2d.shape
    tm = ROUTE_TILE
    per_seq = seq // tm
    return pl.pallas_call(
        functools.partial(_combine_kernel, final=final),
        grid_spec=pltpu.PrefetchScalarGridSpec(
            num_scalar_prefetch=1,
            grid=(m // tm,),
            in_specs=[pl.BlockSpec((1, 4, tm), lambda i, ps: (i, 0, 0), memory_space=pltpu.SMEM),
                      pl.BlockSpec((tm, d), lambda i, ps: (i, 0)),
                      pl.BlockSpec((1, 6, d), lambda i, ps: (i // per_seq, 0, 0)),
                      pl.BlockSpec((tm, 2), lambda i, ps: (i, 0)),
                      pl.BlockSpec((1, d), lambda i, ps: (0, 0)),
                      pl.BlockSpec(memory_space=pl.ANY)],
            out_specs=pl.BlockSpec((tm, d), lambda i, ps: (i, 0)),
            scratch_shapes=[pltpu.VMEM((tm,) + SLAB, F32), pltpu.VMEM((tm,) + SLAB, F32),
                            pltpu.SemaphoreType.DMA(())]),
        out_shape=jax.ShapeDtypeStruct((m, d), F32),
        compiler_params=_cparams("arbitrary"),
        name="moe_combine",
    )(pad_start, idx, x2d, mod, weights, final_norm_w, yb)


def hier_moe_layer(x2d, mod, norm_w, w_router_t, b_router, w_gate_up, w_down, final_norm_w, seq, final):
    m = x2d.shape[0]
    h_slab, idx, wgt, counts = moe_route(x2d, mod, norm_w, w_router_t, b_router, seq)
    counts = counts[:, 0].astype(jnp.int32)
    padded = (counts + MOE_BLOCK - 1) // MOE_BLOCK * MOE_BLOCK
    pad_end = jnp.cumsum(padded)
    pad_start = (pad_end - padded).astype(jnp.int32)
    n_blocks = 2 * m // MOE_BLOCK + N_EXPERTS
    blk_start = jnp.arange(n_blocks, dtype=jnp.int32) * MOE_BLOCK
    blk_expert = jnp.minimum(jnp.sum((pad_end[None, :] <= blk_start[:, None]).astype(jnp.int32), axis=1),
                             N_EXPERTS - 1)
    n_used = (pad_end[-1:] // MOE_BLOCK).astype(jnp.int32)
    fill_plan = jnp.concatenate([jnp.where(padded > 0, pad_end - MOE_BLOCK, -1), n_used]).astype(jnp.int32)
    xs = moe_scatter(h_slab, idx, pad_start, fill_plan, n_blocks * MOE_BLOCK)
    yb = moe_experts(xs, blk_expert, n_used, w_gate_up, w_down)
    return moe_combine(x2d, mod, yb, idx, pad_start, wgt.T, final_norm_w, seq, final)


def _layer_params(l, p):
    w_in = p['w_in'][l]
    w_gate = w_in[:, PROJ_MAIN:].T
    w_router_t = jnp.zeros((ROUTER_ROWS, D_MODEL), F32)
    w_router_t = w_router_t.at[:N_EXPERTS].set(p['moe_w_expert'][l].T)
    w_router_t = w_router_t.at[N_EXPERTS:N_EXPERTS + MOE_GROUPS].set(p['moe_w_group'][l].T)
    w_router_hi = w_router_t.astype(BF16)
    w_router_t = jnp.concatenate([w_router_hi, (w_router_t - w_router_hi.astype(F32)).astype(BF16)], axis=0)
    b_router = jnp.zeros((ROUTER_ROWS, 1), F32)
    b_router = b_router.at[:N_EXPERTS, 0].set(p['moe_b_expert'][l])
    b_router = b_router.at[N_EXPERTS:N_EXPERTS + MOE_GROUPS, 0].set(p['moe_b_group'][l])
    s5 = [s5_discretise(p['s5_lam_re'][l, d], p['s5_lam_im'][l, d], p['s5_log_step'][l, d],
                        p['s5_b_re'][l, d], p['s5_b_im'][l, d], p['s5_c_re'][l, d], p['s5_c_im'][l, d])
          for d in range(2)]
    return dict(
        norm_mix=p['norm_mix'][l][None, :], norm_ffn=p['norm_ffn'][l][None, :],
        w_main=w_in[:, :PROJ_MAIN].astype(BF16), w_gate=w_gate.astype(BF16),
        w_out=p['w_out'][l].astype(BF16), s5=s5,
        s5_d=p['s5_d'][l][None, :], s5_w_glu=p['s5_w_glu'][l].astype(BF16), s5_b_glu=p['s5_b_glu'][l][None, :],
        conv_w=p['conv_w'][l], gdn_conv_w=p['gdn_conv_w'][l], gdn_a_log=p['gdn_a_log'][l],
        gdn_dt_bias=p['gdn_dt_bias'][l], gdn_norm_w=p['gdn_norm_w'][l][None, :],
        w_router_t=w_router_t, b_router=b_router,
        w_gate_up=p['moe_w_gate_up'][l], w_down=p['moe_w_down'][l])


def _trunk(x, mods, layers, final_norm_w):
    nb, seq, d = x.shape
    m = nb * seq
    x2d = x.reshape(m, d)
    for l, lp in enumerate(layers):
        mod = mods[l].reshape(nb, 6, d)
        proj, u_t, pre_t = in_projection(x2d, mod, lp['norm_mix'], lp['w_main'], lp['w_gate'], seq)
        (a_f, bm_f, cm_f), (a_b, bm_b, cm_b) = lp['s5']
        y_s5 = s5_mixer(u_t, a_f, bm_f, cm_f, a_b, bm_b, cm_b, lp['s5_d'], lp['s5_w_glu'], lp['s5_b_glu'])
        qkv = gdn_prepare(proj, lp['gdn_conv_w'], seq).reshape(nb, seq, 3 * GDN_WIDTH)
        gates = gdn_gates(pre_t, lp['gdn_a_log'], lp['gdn_dt_bias'])
        uwq_f, uwq_b, kt_f, kt_b, intra, eg = gdn_local(qkv, gates.transpose(0, 2, 1), gates)
        o_f, o_b = gdn_state_scan(uwq_f, uwq_b, kt_f, kt_b, intra, eg)
        x2d = out_projection(x2d, mod, y_s5, proj, lp['conv_w'], o_f.reshape(m, GDN_WIDTH),
                             o_b.reshape(m, GDN_WIDTH), lp['gdn_norm_w'], lp['w_out'], seq)
        x2d = hier_moe_layer(x2d, mod, lp['norm_ffn'], lp['w_router_t'], lp['b_router'],
                             lp['w_gate_up'], lp['w_down'], final_norm_w, seq, final=(l == len(layers) - 1))
    return x2d.reshape(nb, seq, d)


def kernel(x_prompt, x_sample, c_prompt, c_sample, w_ada, b_ada, norm_mix, norm_ffn, w_in, w_out, s5_lam_re, s5_lam_im, s5_log_step, s5_b_re, s5_b_im, s5_c_re, s5_c_im, s5_d, s5_w_glu, s5_b_glu, conv_w, gdn_conv_w, gdn_a_log, gdn_dt_bias, gdn_norm_w, moe_w_group, moe_b_group, moe_w_expert, moe_b_expert, moe_w_gate_up, moe_w_down, final_norm):
    p = dict(norm_mix=norm_mix, norm_ffn=norm_ffn, w_in=w_in, w_out=w_out, s5_lam_re=s5_lam_re,
             s5_lam_im=s5_lam_im, s5_log_step=s5_log_step, s5_b_re=s5_b_re, s5_b_im=s5_b_im, s5_c_re=s5_c_re,
             s5_c_im=s5_c_im, s5_d=s5_d, s5_w_glu=s5_w_glu, s5_b_glu=s5_b_glu, conv_w=conv_w,
             gdn_conv_w=gdn_conv_w, gdn_a_log=gdn_a_log, gdn_dt_bias=gdn_dt_bias, gdn_norm_w=gdn_norm_w,
             moe_w_group=moe_w_group, moe_b_group=moe_b_group, moe_w_expert=moe_w_expert,
             moe_b_expert=moe_b_expert, moe_w_gate_up=moe_w_gate_up, moe_w_down=moe_w_down)
    depth = w_ada.shape[0]
    layers = [_layer_params(l, p) for l in range(depth)]
    nbp = c_prompt.shape[0]
    mods = ada_modulation(jnp.concatenate([c_prompt, c_sample], axis=0), w_ada, b_ada)
    fnw = final_norm[None, :]
    y_prompt = _trunk(x_prompt, mods[:, :nbp], layers, fnw)
    y_sample = _trunk(x_sample, mods[:, nbp:], layers, fnw)
    return (y_prompt, y_sample)
```

```python
import functools

import jax
import jax.numpy as jnp
from jax import lax
from jax.experimental import pallas as pl
from jax.experimental.pallas import tpu as pltpu

F32 = jnp.float32
BF16 = jnp.bfloat16
HIGHEST = lax.Precision.HIGHEST

D_MODEL = 1024
DEPTH = 4
S5_WIDTH = 256
S5_GROUP = 16
S5_GROUPS = 16
S5_STATE = 64
S5_LANES = S5_GROUPS * S5_STATE
CONV_WIDTH = 256
GDN_WIDTH = 512
GDN_HEADS = 4
GDN_HEAD_DIM = 128
GDN_CHUNK = 64
PROJ_MAIN = S5_WIDTH + 3 * CONV_WIDTH + 4 * GDN_WIDTH
N_GATE_COLS = 4 * GDN_HEADS
MOE_GROUPS = 4
EXPERTS_PER_GROUP = 8
N_EXPERTS = 32
EXPERT_FF = 512
MOE_BLOCK = 512
EPS = 1e-6

SUBLANES = 8
LANES = 128
SLAB = (SUBLANES, LANES)
VMEM_LIMIT = 56 * 1024 * 1024

ROW_TILE = 512
S5_TIME_TILE = 64
GDN_STATE_CHUNKS = 8
ROUTE_TILE = 512
ROUTER_ROWS = N_EXPERTS + 16
DMA_UNROLL = 8


def _cparams(*sem):
    return pltpu.CompilerParams(dimension_semantics=sem, vmem_limit_bytes=VMEM_LIMIT)


def _dot(a, b, **kw):
    return jnp.dot(a, b, preferred_element_type=F32, **kw)


def _dot_nt(a, b, **kw):
    return lax.dot_general(a, b, (((1,), (1,)), ((), ())), preferred_element_type=F32, **kw)


def _dot_tn(a, b, **kw):
    return lax.dot_general(a, b, (((0,), (0,)), ((), ())), preferred_element_type=F32, **kw)


def _sigmoid(x):
    return 0.5 * jnp.tanh(0.5 * x) + 0.5


def _silu(x):
    return x * _sigmoid(x)


def _gelu_tanh(x):
    return 0.5 * x * (1.0 + jnp.tanh(0.7978845608028654 * (x + 0.044715 * (x * x * x))))


def _softplus(x):
    return jnp.maximum(x, 0.0) + jnp.log(1.0 + jnp.exp(-jnp.abs(x)))


def _rms(x):
    return x * lax.rsqrt(jnp.mean(x * x, axis=-1, keepdims=True) + EPS)


def _ada_kernel(c_ref, w_ref, b_ref, o_ref):
    o_ref[0] = _dot(_silu(c_ref[...]), w_ref[0], precision=HIGHEST) + b_ref[0]


def ada_modulation(c_all, w_ada, b_ada):
    nb = c_all.shape[0]
    depth, d, n = w_ada.shape
    tn = 1536
    return pl.pallas_call(
        _ada_kernel,
        grid=(depth, n // tn),
        in_specs=[pl.BlockSpec((nb, d), lambda l, j: (0, 0)),
                  pl.BlockSpec((1, d, tn), lambda l, j: (l, 0, j)),
                  pl.BlockSpec((1, 1, tn), lambda l, j: (l, 0, j))],
        out_specs=pl.BlockSpec((1, nb, tn), lambda l, j: (l, 0, j)),
        out_shape=jax.ShapeDtypeStruct((depth, nb, n), F32),
        compiler_params=_cparams("parallel", "parallel"),
        name="ada_modulation",
    )(c_all, w_ada, b_ada.reshape(depth, 1, n))


def _in_proj_kernel(x_ref, mod_ref, nw_ref, w_ref, wg_ref, proj_ref, u_ref, gate_ref):
    h = _rms(x_ref[...]) * nw_ref[...] * (1.0 + mod_ref[0, 1:2, :]) + mod_ref[0, 0:1, :]
    hb = h.astype(BF16)
    proj = _dot(hb, w_ref[...]).astype(proj_ref.dtype)
    proj_ref[...] = proj
    u_ref[...] = proj[:, :S5_WIDTH]
    gate_ref[0] = _dot_nt(wg_ref[...], hb)


def in_projection(x2d, mod, norm_w, w_main, w_gate, seq):
    m, d = x2d.shape
    tm = ROW_TILE
    per_seq = seq // tm
    nb = m // seq
    return pl.pallas_call(
        _in_proj_kernel,
        grid=(m // tm,),
        in_specs=[pl.BlockSpec((tm, d), lambda i: (i, 0)),
                  pl.BlockSpec((1, 6, d), lambda i: (i // per_seq, 0, 0)),
                  pl.BlockSpec((1, d), lambda i: (0, 0)),
                  pl.BlockSpec((d, PROJ_MAIN), lambda i: (0, 0)),
                  pl.BlockSpec((N_GATE_COLS, d), lambda i: (0, 0))],
        out_specs=[pl.BlockSpec((tm, PROJ_MAIN), lambda i: (i, 0)),
                   pl.BlockSpec((tm, S5_WIDTH), lambda i: (i % per_seq, i // per_seq)),
                   pl.BlockSpec((1, N_GATE_COLS, tm), lambda i: (i // per_seq, 0, i % per_seq))],
        out_shape=[jax.ShapeDtypeStruct((m, PROJ_MAIN), BF16),
                   jax.ShapeDtypeStruct((seq, nb * S5_WIDTH), BF16),
                   jax.ShapeDtypeStruct((nb, N_GATE_COLS, seq), F32)],
        compiler_params=_cparams("parallel"),
        name="in_projection",
    )(x2d, mod, norm_w, w_main, w_gate)


def _s5_scan(x_ref, h0_re, h0_im, a_re, a_im, steps, reverse):
    def body(s, carry):
        hr, hi = carry
        t = (steps - 1 - s) if reverse else s
        rows = pl.ds(pl.multiple_of(t * SUBLANES, SUBLANES), SUBLANES)
        xr = x_ref[rows, 0:S5_LANES]
        xi = x_ref[rows, S5_LANES:2 * S5_LANES]
        nr = a_re * hr - a_im * hi + xr
        ni = a_re * hi + a_im * hr + xi
        x_ref[rows, 0:S5_LANES] = nr
        x_ref[rows, S5_LANES:2 * S5_LANES] = ni
        return nr, ni
    return lax.fori_loop(0, steps, body, (h0_re, h0_im), unroll=4)


def _s5_direction(u_ref, a_ref, bmat_ref, cmat_ref, x_ref, st_ref, reverse):
    tc = u_ref.shape[0]
    w = S5_WIDTH
    nb = u_ref.shape[1] // w
    first = pl.program_id(0) == 0

    @pl.when(first)
    def _():
        st_ref[...] = jnp.zeros_like(st_ref)

    u = u_ref[...].astype(F32).reshape(tc * nb, w)
    x_ref[...] = _dot(u.astype(BF16), bmat_ref[...])
    a_re = jnp.broadcast_to(a_ref[0:1, :], (nb, S5_LANES))
    a_im = jnp.broadcast_to(a_ref[1:2, :], (nb, S5_LANES))
    hr, hi = _s5_scan(x_ref, st_ref[0], st_ref[1], a_re, a_im, tc, reverse)
    st_ref[0] = hr
    st_ref[1] = hi
    return u, _dot(x_ref[...].astype(BF16), cmat_ref[...])


def _s5_fwd_kernel(u_ref, a_ref, bmat_ref, cmat_ref, y_ref, x_ref, st_ref):
    _, y = _s5_direction(u_ref, a_ref, bmat_ref, cmat_ref, x_ref, st_ref, reverse=False)
    y_ref[...] = y.reshape(y_ref.shape)


def _s5_bwd_kernel(u_ref, yf_ref, a_ref, bmat_ref, cmat_ref, d_ref, wglu_ref, bglu_ref, o_ref, x_ref, st_ref):
    u, y = _s5_direction(u_ref, a_ref, bmat_ref, cmat_ref, x_ref, st_ref, reverse=True)
    y = y + u * d_ref[...] + yf_ref[...].reshape(u.shape)
    y = _gelu_tanh(y)
    out = y * _sigmoid(_dot(y.astype(BF16), wglu_ref[...]) + bglu_ref[...])
    o_ref[...] = out.reshape(o_ref.shape).astype(o_ref.dtype)


def s5_mixer(u_t, a_f, bmat_f, cmat_f, a_b, bmat_b, cmat_b, d_skip, w_glu, b_glu):
    seq = u_t.shape[0]
    w = S5_WIDTH
    nb = u_t.shape[1] // w
    assert nb == SUBLANES, "the scan keeps one batch row per sublane"
    tc = S5_TIME_TILE
    n = seq // tc
    rows = tc * nb
    const2 = lambda i: (0, 0)
    par_specs = [pl.BlockSpec((2, S5_LANES), const2),
                 pl.BlockSpec((w, 2 * S5_LANES), const2),
                 pl.BlockSpec((2 * S5_LANES, w), const2)]
    scratch = [pltpu.VMEM((rows, 2 * S5_LANES), F32), pltpu.VMEM((2, nb, S5_LANES), F32)]
    y_f = pl.pallas_call(
        _s5_fwd_kernel,
        grid=(n,),
        in_specs=[pl.BlockSpec((tc, nb * w), lambda i: (i, 0))] + par_specs,
        out_specs=pl.BlockSpec((tc, nb, w), lambda i: (i, 0, 0)),
        out_shape=jax.ShapeDtypeStruct((seq, nb, w), F32),
        scratch_shapes=scratch,
        compiler_params=_cparams("arbitrary"),
        name="s5_forward",
    )(u_t, a_f, bmat_f, cmat_f)
    rev = lambda i: (n - 1 - i, 0, 0)
    rev2 = lambda i: (n - 1 - i, 0)
    return pl.pallas_call(
        _s5_bwd_kernel,
        grid=(n,),
        in_specs=[pl.BlockSpec((tc, nb * w), rev2), pl.BlockSpec((tc, nb, w), rev)] + par_specs
                 + [pl.BlockSpec((1, w), const2), pl.BlockSpec((w, w), const2), pl.BlockSpec((1, w), const2)],
        out_specs=pl.BlockSpec((tc, nb * w), rev2),
        out_shape=jax.ShapeDtypeStruct((seq, nb * w), BF16),
        scratch_shapes=scratch,
        compiler_params=_cparams("arbitrary"),
        name="s5_backward_glu",
    )(u_t, y_f, a_b, bmat_b, cmat_b, d_skip, w_glu, b_glu)


def s5_discretise(lam_re, lam_im, log_step, b_re, b_im, c_re, c_im):
    g, p, h = S5_GROUPS, S5_STATE, S5_GROUP
    lr, li = lam_re.astype(F32), lam_im.astype(F32)
    dt = jnp.exp(log_step.astype(F32))[:, None]
    mag = jnp.exp(lr * dt)
    abr, abi = mag * jnp.cos(li * dt), mag * jnp.sin(li * dt)
    den = lr * lr + li * li
    cr = ((abr - 1.0) * lr + abi * li) / den
    ci = (abi * lr - (abr - 1.0) * li) / den
    bbr = cr[..., None] * b_re - ci[..., None] * b_im
    bbi = cr[..., None] * b_im + ci[..., None] * b_re
    eye = jnp.eye(g, dtype=F32)
    to_b = lambda t: jnp.einsum('gph,gk->ghkp', t, eye).reshape(g * h, g * p)
    bmat = jnp.concatenate([to_b(bbr), to_b(bbi)], axis=1)
    to_c = lambda t: jnp.einsum('ghp,gk->gpkh', t, eye).reshape(g * p, g * h)
    cmat = jnp.concatenate([to_c(c_re.astype(F32)), -to_c(c_im.astype(F32))], axis=0)
    a = jnp.stack([abr.reshape(g * p), abi.reshape(g * p)])
    return a, bmat.astype(BF16), cmat.astype(BF16)


def _conv3_rows(x, prev_row, next_row, w_ref):
    tm = x.shape[0]
    rows = lax.broadcasted_iota(jnp.int32, x.shape, 0)
    xp = jnp.where(rows == 0, prev_row, pltpu.roll(x, 1, 0))
    xn = jnp.where(rows == tm - 1, next_row, pltpu.roll(x, tm - 1, 0))
    return xp * w_ref[0:1, :] + x * w_ref[1:2, :] + xn * w_ref[2:3, :]


HALO_ROWS = 16
GDN_PREP_ROWS = 64


def _halo_specs(tm, width, col_block, n_rows):
    per = tm // HALO_ROWS
    last = n_rows // HALO_ROWS - 1
    prev = pl.BlockSpec((HALO_ROWS, width), lambda i, *_: (jnp.maximum(i * per - 1, 0), col_block(*_)))
    nxt = pl.BlockSpec((HALO_ROWS, width), lambda i, *_: (jnp.minimum((i + 1) * per, last), col_block(*_)))
    return prev, nxt


def _gdn_prep_kernel(x_ref, xp_ref, xn_ref, w_ref, o_ref, *, seq):
    tm = x_ref.shape[0]
    j = pl.program_id(1)
    t0 = (pl.program_id(0) * tm) % seq
    first_tile = t0 == 0
    last_tile = t0 + tm == seq
    normalise = j < 2
    scale = jnp.where(j == 0, GDN_HEAD_DIM ** -0.5, 1.0)
    rc = GDN_PREP_ROWS
    row = lambda ref, r, cols: ref[r:r + 1, cols].astype(F32)
    for h in range(GDN_HEADS):
        cols = slice(h * GDN_HEAD_DIM, (h + 1) * GDN_HEAD_DIM)
        w = w_ref[:, cols]
        for r0 in range(0, tm, rc):
            prev_row = (jnp.where(first_tile, 0.0, row(xp_ref, HALO_ROWS - 1, cols)) if r0 == 0
                        else row(x_ref, r0 - 1, cols))
            next_row = (jnp.where(last_tile, 0.0, row(xn_ref, 0, cols)) if r0 + rc == tm
                        else row(x_ref, r0 + rc, cols))
            y = _silu(_conv3_rows(x_ref[r0:r0 + rc, cols].astype(F32), prev_row, next_row, w))
            yn = y * (lax.rsqrt(jnp.sum(y * y, axis=-1, keepdims=True) + EPS) * scale)
            o_ref[r0:r0 + rc, cols] = jnp.where(normalise, yn, y).astype(o_ref.dtype)


def gdn_prepare(proj, conv_w, seq):
    m = proj.shape[0]
    tm = ROW_TILE
    first = (S5_WIDTH + 3 * CONV_WIDTH) // GDN_WIDTH
    prev, nxt = _halo_specs(tm, GDN_WIDTH, lambda j: first + j, m)
    return pl.pallas_call(
        functools.partial(_gdn_prep_kernel, seq=seq),
        grid=(m // tm, 3),
        in_specs=[pl.BlockSpec((tm, GDN_WIDTH), lambda i, j: (i, first + j)), prev, nxt,
                  pl.BlockSpec((3, GDN_WIDTH), lambda i, j: (0, j))],
        out_specs=pl.BlockSpec((tm, GDN_WIDTH), lambda i, j: (i, j)),
        out_shape=jax.ShapeDtypeStruct((m, 3 * GDN_WIDTH), BF16),
        compiler_params=_cparams("parallel", "parallel"),
        name="gdn_prepare",
    )(proj, proj, proj, conv_w)


def _gdn_gates_kernel(pre_ref, alog_ref, dtb_ref, o_ref):
    seq = pre_ref.shape[2]
    nh2 = 2 * GDN_HEADS
    g = -jnp.exp(alog_ref[...]) * _softplus(pre_ref[0, 0:nh2, :] + dtb_ref[...])
    o_ref[0, nh2:2 * nh2, :] = _sigmoid(pre_ref[0, nh2:2 * nh2, :])
    s = lax.broadcasted_iota(jnp.int32, (LANES, LANES), 0)
    t = lax.broadcasted_iota(jnp.int32, (LANES, LANES), 1)
    same = (s // GDN_CHUNK) == (t // GDN_CHUNK)
    m_f = jnp.where(same & (s <= t), 1.0, 0.0)
    m_b = jnp.where(same & (s >= t), 1.0, 0.0)
    for jb in range(seq // LANES):
        cols = slice(jb * LANES, (jb + 1) * LANES)
        gb = g[:, cols]
        o_ref[0, 0:GDN_HEADS, cols] = _dot(gb, m_f, precision=HIGHEST)[0:GDN_HEADS]
        o_ref[0, GDN_HEADS:nh2, cols] = _dot(gb, m_b, precision=HIGHEST)[GDN_HEADS:nh2]


def gdn_gates(pre_t, a_log, dt_bias):
    nb, rows, seq = pre_t.shape
    col = lambda t: t.astype(F32).reshape(2 * GDN_HEADS, 1)
    return pl.pallas_call(
        _gdn_gates_kernel,
        grid=(nb,),
        in_specs=[pl.BlockSpec((1, rows, seq), lambda b: (b, 0, 0)),
                  pl.BlockSpec((2 * GDN_HEADS, 1), lambda b: (0, 0)),
                  pl.BlockSpec((2 * GDN_HEADS, 1), lambda b: (0, 0))],
        out_specs=pl.BlockSpec((1, rows, seq), lambda b: (b, 0, 0)),
        out_shape=jax.ShapeDtypeStruct((nb, rows, seq), F32),
        compiler_params=_cparams("parallel"),
        name="gdn_gates",
    )(pre_t, col(a_log), col(dt_bias))


def _nilpotent_inverses(mats, nilpotency):
    n = mats[0].shape[0]
    eye = jnp.where(lax.broadcasted_iota(jnp.int32, (n, n), 0) == lax.broadcasted_iota(jnp.int32, (n, n), 1), 1.0, 0.0)
    ps = [eye - a for a in mats]
    a16 = [a.astype(BF16) for a in mats]
    sqs = [_dot(a, a) for a in a16]
    k = 2
    while True:
        sq16 = [s.astype(BF16) for s in sqs]
        ps = [p + _dot(p.astype(BF16), s) for p, s in zip(ps, sq16)]
        k *= 2
        if k >= nilpotency:
            return ps
        sqs = [_dot(s, s) for s in sq16]


def _gdn_local_kernel(q_ref, k_ref, v_ref, gcol_ref, grow_ref, uwqf_ref, uwqb_ref, ktf_ref, ktb_ref, in_ref, eg_ref):
    r = q_ref.shape[1]
    c = GDN_CHUNK
    hd = GDN_HEAD_DIM
    ii = lax.broadcasted_iota(jnp.int32, (r, r), 0)
    jj = lax.broadcasted_iota(jnp.int32, (r, r), 1)
    same = (ii >= c) == (jj >= c)
    incl = (same & (ii >= jj), same & (ii <= jj))
    strict = (same & (ii > jj), same & (ii < jj))
    first_chunk = lax.broadcasted_iota(jnp.int32, (r, 1), 0) < c
    zeros = jnp.zeros((r, r), F32)
    big_a, rhs, keep = [], [], []
    for h in range(GDN_HEADS):
        cols = slice(h * hd, (h + 1) * hd)
        q16, k16 = q_ref[0, :, cols], k_ref[0, :, cols]
        q, k, v = q16.astype(F32), k16.astype(F32), v_ref[0, :, cols].astype(F32)
        kk = _dot_nt(k16, k16)
        qk = _dot_nt(q16, k16)
        a_dir, rhs_dir = [], []
        for d in range(2):
            lane = d * GDN_HEADS + h
            gc_col = gcol_ref[0, :, lane:lane + 1]
            gc_row = grow_ref[0, lane:lane + 1, :]
            beta = gcol_ref[0, :, 2 * GDN_HEADS + lane:2 * GDN_HEADS + lane + 1]
            decay = jnp.exp(jnp.where(incl[d], gc_col - gc_row, -1e30))
            a_dir.append(jnp.where(strict[d], kk * beta * decay, 0.0))
            intra = jnp.where(incl[d], qk * decay, 0.0)
            ends = (gc_col[c - 1:c], gc_col[2 * c - 1:2 * c]) if d == 0 else (gc_col[0:1], gc_col[c:c + 1])
            g_last = jnp.where(first_chunk, ends[0], ends[1])
            eg = jnp.exp(gc_col)
            kb = k * beta
            rhs_dir.append(jnp.concatenate([v * beta, kb * eg], axis=1))
            keep.append((q * eg, (k * jnp.exp(g_last - gc_col)).T, intra[:, :c] + intra[:, c:]))
            for ch in range(2):
                eg_ref[0, ch, lane:lane + 1, :] = jnp.broadcast_to(jnp.exp(ends[ch]), (1, LANES))
        big_a.append(jnp.concatenate([jnp.concatenate([a_dir[0], zeros], axis=1),
                                      jnp.concatenate([zeros, a_dir[1]], axis=1)], axis=0))
        rhs.append(jnp.concatenate(rhs_dir, axis=0))
    t_inv = _nilpotent_inverses(big_a, c)
    for h in range(GDN_HEADS):
        uw = _dot(t_inv[h].astype(BF16), rhs[h].astype(BF16))
        (qd_f, kt_f, in_f), (qd_b, kt_b, in_b) = keep[2 * h], keep[2 * h + 1]
        uwqf_ref[0, h] = jnp.concatenate([uw[:r], qd_f], axis=1).astype(BF16)
        uwqb_ref[0, h] = jnp.concatenate([uw[r:], qd_b], axis=1).astype(BF16)
        ktf_ref[0, h] = kt_f.astype(BF16)
        ktb_ref[0, h] = kt_b.astype(BF16)
        in_ref[0, h] = jnp.concatenate([in_f, in_b], axis=1).astype(BF16)


def gdn_local(qkv, gates_col, gates_row):
    nb, seq, _ = qkv.shape
    r = 2 * GDN_CHUNK
    hd, nh = GDN_HEAD_DIM, GDN_HEADS
    uwq_sd = jax.ShapeDtypeStruct((nb, nh, seq, 3 * hd), BF16)
    kt_sd = jax.ShapeDtypeStruct((nb, nh, hd, seq), BF16)
    uwq_spec = pl.BlockSpec((1, nh, r, 3 * hd), lambda b, i: (b, 0, i, 0))
    kt_spec = pl.BlockSpec((1, nh, hd, r), lambda b, i: (b, 0, 0, i))
    return pl.pallas_call(
        _gdn_local_kernel,
        grid=(nb, seq // r),
        in_specs=[pl.BlockSpec((1, r, GDN_WIDTH), lambda b, i: (b, i, 0)),
                  pl.BlockSpec((1, r, GDN_WIDTH), lambda b, i: (b, i, 1)),
                  pl.BlockSpec((1, r, GDN_WIDTH), lambda b, i: (b, i, 2)),
                  pl.BlockSpec((1, r, 4 * nh), lambda b, i: (b, i, 0)),
                  pl.BlockSpec((1, 4 * nh, r), lambda b, i: (b, 0, i))],
        out_specs=[uwq_spec, uwq_spec, kt_spec, kt_spec,
                   pl.BlockSpec((1, nh, r, 2 * GDN_CHUNK), lambda b, i: (b, 0, i, 0)),
                   pl.BlockSpec((1, 2, SUBLANES, LANES), lambda b, i: (b, i, 0, 0))],
        out_shape=[uwq_sd, uwq_sd, kt_sd, kt_sd,
                   jax.ShapeDtypeStruct((nb, nh, seq, 2 * GDN_CHUNK), BF16),
                   jax.ShapeDtypeStruct((nb, seq // GDN_CHUNK, SUBLANES, LANES), F32)],
        compiler_params=_cparams("parallel", "parallel"),
        name="gdn_local",
    )(qkv, qkv, qkv, gates_col, gates_row)


def _gdn_state_kernel(uwqf, ktf, inf, egf, uwqb, ktb, inb, egb, of_ref, ob_ref, st_ref):
    @pl.when(pl.program_id(1) == 0)
    def _():
        st_ref[...] = jnp.zeros_like(st_ref)

    c = GDN_CHUNK
    hd = GDN_HEAD_DIM
    nc = uwqf.shape[2] // c
    chains = [(d, h) for d in range(2) for h in range(GDN_HEADS)]
    refs = ((uwqf, ktf, inf, egf, of_ref), (uwqb, ktb, inb, egb, ob_ref))
    states = [st_ref[j] for j in range(len(chains))]
    for step in range(nc):
        sws = []
        for j, (d, h) in enumerate(chains):
            ci = step if d == 0 else nc - 1 - step
            rows = slice(ci * c, (ci + 1) * c)
            uwq = refs[d][0]
            wq = jnp.concatenate([uwq[0, h, rows, hd:2 * hd], uwq[0, h, rows, 2 * hd:3 * hd]], axis=0)
            sws.append(_dot(wq, states[j].astype(BF16)))
        for j, (d, h) in enumerate(chains):
            ci = step if d == 0 else nc - 1 - step
            rows = slice(ci * c, (ci + 1) * c)
            uwq, kt, intra, eg, o_ref = refs[d]
            sw = sws[j]
            v16 = (uwq[0, h, rows, 0:hd].astype(F32) - sw[:c]).astype(BF16)
            out = sw[c:] + _dot(intra[0, h, rows, d * c:(d + 1) * c], v16)
            o_ref[0, rows, h * hd:(h + 1) * hd] = out.astype(o_ref.dtype)
            lane = d * GDN_HEADS + h
            states[j] = states[j] * eg[0, ci, lane:lane + 1, :] + _dot(kt[0, h, :, rows], v16)
    for j in range(len(chains)):
        st_ref[j] = states[j]


def gdn_state_scan(uwq_f, uwq_b, kt_f, kt_b, intra, eg):
    nb, nh, seq, _ = uwq_f.shape
    hd = GDN_HEAD_DIM
    nc = GDN_STATE_CHUNKS
    r = nc * GDN_CHUNK
    n = seq // r

    def specs(pos):
        return [pl.BlockSpec((1, nh, r, 3 * hd), lambda b, i: (b, 0, pos(i), 0)),
                pl.BlockSpec((1, nh, hd, r), lambda b, i: (b, 0, 0, pos(i))),
                pl.BlockSpec((1, nh, r, 2 * GDN_CHUNK), lambda b, i: (b, 0, pos(i), 0)),
                pl.BlockSpec((1, nc, SUBLANES, LANES), lambda b, i: (b, pos(i), 0, 0))]

    fwd = lambda i: i
    bwd = lambda i: n - 1 - i
    out_sd = jax.ShapeDtypeStruct((nb, seq, GDN_WIDTH), BF16)
    return pl.pallas_call(
        _gdn_state_kernel,
        grid=(nb, n),
        in_specs=specs(fwd) + specs(bwd),
        out_specs=[pl.BlockSpec((1, r, GDN_WIDTH), lambda b, i: (b, i, 0)),
                   pl.BlockSpec((1, r, GDN_WIDTH), lambda b, i: (b, n - 1 - i, 0))],
        out_shape=[out_sd, out_sd],
        scratch_shapes=[pltpu.VMEM((2 * nh, hd, hd), F32)],
        compiler_params=_cparams("parallel", "arbitrary"),
        name="gdn_state_scan",
    )(uwq_f, kt_f, intra, eg, uwq_b, kt_b, intra, eg)


def _out_proj_kernel(x_ref, mod_ref, ys5_ref, cx_ref, cb_ref, cc_ref, cxp_ref, cxn_ref, ccp_ref, ccn_ref,
                     cw_ref, of_ref, ob_ref, z_ref, gnw_ref, w_ref, o_ref, *, seq):
    tm = x_ref.shape[0]
    t0 = (pl.program_id(0) * tm) % seq
    f32 = lambda ref, rows=slice(None): ref[rows, :].astype(F32)
    last = slice(HALO_ROWS - 1, HALO_ROWS)
    prev_row = jnp.where(t0 == 0, 0.0, f32(ccp_ref, last) * f32(cxp_ref, last))
    next_row = jnp.where(t0 + tm == seq, 0.0, f32(ccn_ref, slice(0, 1)) * f32(cxn_ref, slice(0, 1)))
    y_conv = f32(cb_ref) * _conv3_rows(f32(cc_ref) * f32(cx_ref), prev_row, next_row, cw_ref)
    o = f32(of_ref) + f32(ob_ref)
    z = f32(z_ref)
    heads = []
    for h in range(GDN_HEADS):
        cols = slice(h * GDN_HEAD_DIM, (h + 1) * GDN_HEAD_DIM)
        heads.append(_rms(o[:, cols]) * gnw_ref[...] * _silu(z[:, cols]))
    y_gdn = jnp.concatenate(heads, axis=1)
    c0, c1 = S5_WIDTH, S5_WIDTH + CONV_WIDTH
    mix = (_dot(ys5_ref[...].astype(BF16), w_ref[0:c0, :])
           + _dot(y_conv.astype(BF16), w_ref[c0:c1, :])
           + _dot(y_gdn.astype(BF16), w_ref[c1:, :]))
    o_ref[...] = x_ref[...] + mod_ref[0, 2:3, :] * mix


def out_projection(x2d, mod, y_s5, proj, conv_w, o_f, o_b, gdn_norm_w, w_out, seq):
    m, d = x2d.shape
    tm = ROW_TILE
    per_seq = seq // tm
    cw = CONV_WIDTH
    row = lambda c: pl.BlockSpec((tm, cw), lambda i: (i, c))
    cxp, cxn = _halo_specs(tm, cw, lambda: 1, m)
    ccp, ccn = _halo_specs(tm, cw, lambda: 3, m)
    gw = GDN_WIDTH
    return pl.pallas_call(
        functools.partial(_out_proj_kernel, seq=seq),
        grid=(m // tm,),
        in_specs=[pl.BlockSpec((tm, d), lambda i: (i, 0)),
                  pl.BlockSpec((1, 6, d), lambda i: (i // per_seq, 0, 0)),
                  pl.BlockSpec((tm, S5_WIDTH), lambda i: (i % per_seq, i // per_seq)),
                  row(1), row(2), row(3), cxp, cxn, ccp, ccn,
                  pl.BlockSpec((3, cw), lambda i: (0, 0)),
                  pl.BlockSpec((tm, gw), lambda i: (i, 0)),
                  pl.BlockSpec((tm, gw), lambda i: (i, 0)),
                  pl.BlockSpec((tm, gw), lambda i: (i, (PROJ_MAIN - gw) // gw)),
                  pl.BlockSpec((1, GDN_HEAD_DIM), lambda i: (0, 0)),
                  pl.BlockSpec((d, d), lambda i: (0, 0))],
        out_specs=pl.BlockSpec((tm, d), lambda i: (i, 0)),
        out_shape=jax.ShapeDtypeStruct((m, d), F32),
        compiler_params=_cparams("parallel"),
        name="out_projection",
    )(x2d, mod, y_s5, proj, proj, proj, proj, proj, proj, proj, conv_w, o_f, o_b, proj, gdn_norm_w, w_out)


def _first_argmax(vals, rows, n):
    m = jnp.max(vals, axis=0, keepdims=True)
    return m, jnp.min(jnp.where(vals == m, rows, n), axis=0, keepdims=True)


def _route_kernel(x_ref, mod_ref, nw_ref, wr_ref, br_ref, h_ref, idx_ref, wgt_ref, cnt_ref, carry_ref, before_ref):
    step = pl.program_id(0)
    tm = x_ref.shape[0]

    @pl.when(step == 0)
    def _():
        carry_ref[...] = jnp.zeros_like(carry_ref)
        earlier = lax.broadcasted_iota(jnp.int32, (tm, tm), 0) < lax.broadcasted_iota(jnp.int32, (tm, tm), 1)
        before_ref[...] = jnp.where(earlier, 1.0, 0.0).astype(BF16)

    h = _rms(x_ref[...]) * nw_ref[...] * (1.0 + mod_ref[0, 4:5, :]) + mod_ref[0, 3:4, :]
    h_ref[...] = h.reshape(h_ref.shape)
    nr = br_ref.shape[0]
    h_hi = h.astype(BF16)
    h_lo = (h - h_hi.astype(F32)).astype(BF16)
    head = _dot_nt(wr_ref[...], h_hi)
    logits = head[:nr] + head[nr:] + _dot_nt(wr_ref[0:nr, :], h_lo) + br_ref[...]
    lg = logits[N_EXPERTS:N_EXPERTS + MOE_GROUPS, :]
    rows_g = lax.broadcasted_iota(jnp.int32, lg.shape, 0)
    g_max, g_idx = _first_argmax(lg, rows_g, MOE_GROUPS)
    g_w = 1.0 / jnp.sum(jnp.exp(lg - g_max), axis=0, keepdims=True)
    le = jnp.zeros((EXPERTS_PER_GROUP, tm), F32)
    for g in range(MOE_GROUPS):
        le = jnp.where(g_idx == g, logits[g * EXPERTS_PER_GROUP:(g + 1) * EXPERTS_PER_GROUP, :], le)
    rows_e = lax.broadcasted_iota(jnp.int32, le.shape, 0)
    m1, i1 = _first_argmax(le, rows_e, EXPERTS_PER_GROUP)
    m2, i2 = _first_argmax(jnp.where(rows_e == i1, -jnp.inf, le), rows_e, EXPERTS_PER_GROUP)
    r = jnp.exp(m2 - m1)
    w1 = g_w / (1.0 + r)
    e1 = g_idx * EXPERTS_PER_GROUP + i1
    e2 = g_idx * EXPERTS_PER_GROUP + i2
    rows_x = lax.broadcasted_iota(jnp.int32, (N_EXPERTS, tm), 0)
    hit1 = rows_x == e1
    hit2 = rows_x == e2
    onehot = jnp.where(hit1 | hit2, 1.0, 0.0)
    ahead = _dot(onehot.astype(BF16), before_ref[...]) + carry_ref[:, 0:1]
    rank1 = jnp.sum(jnp.where(hit1, ahead, 0.0), axis=0, keepdims=True)
    rank2 = jnp.sum(jnp.where(hit2, ahead, 0.0), axis=0, keepdims=True)
    idx_ref[0, 0:1, :] = e1
    idx_ref[0, 1:2, :] = e2
    idx_ref[0, 2:3, :] = rank1.astype(jnp.int32)
    idx_ref[0, 3:4, :] = rank2.astype(jnp.int32)
    wgt_ref[0:1, :] = w1
    wgt_ref[1:2, :] = w1 * r
    carry_ref[...] = carry_ref[...] + jnp.sum(onehot, axis=1, keepdims=True)
    cnt_ref[...] = carry_ref[...]


def moe_route(x2d, mod, norm_w, w_router_t, b_router, seq):
    m, d = x2d.shape
    tm = ROUTE_TILE
    per_seq = seq // tm
    nr = w_router_t.shape[0]
    return pl.pallas_call(
        _route_kernel,
        grid=(m // tm,),
        in_specs=[pl.BlockSpec((tm, d), lambda i: (i, 0)),
                  pl.BlockSpec((1, 6, d), lambda i: (i // per_seq, 0, 0)),
                  pl.BlockSpec((1, d), lambda i: (0, 0)),
                  pl.BlockSpec((nr, d), lambda i: (0, 0)),
                  pl.BlockSpec((nr // 2, 1), lambda i: (0, 0))],
        out_specs=[pl.BlockSpec((tm,) + SLAB, lambda i: (i, 0, 0)),
                   pl.BlockSpec((1, 4, tm), lambda i: (i, 0, 0)),
                   pl.BlockSpec((2, tm), lambda i: (0, i)),
                   pl.BlockSpec((N_EXPERTS, LANES), lambda i: (0, 0))],
        out_shape=[jax.ShapeDtypeStruct((m,) + SLAB, F32),
                   jax.ShapeDtypeStruct((m // tm, 4, tm), jnp.int32),
                   jax.ShapeDtypeStruct((2, m), F32),
                   jax.ShapeDtypeStruct((N_EXPERTS, LANES), F32)],
        scratch_shapes=[pltpu.VMEM((N_EXPERTS, LANES), F32), pltpu.VMEM((tm, tm), BF16)],
        compiler_params=_cparams("arbitrary"),
        name="moe_route",
    )(x2d, mod, norm_w, w_router_t, b_router)


def _row_copy(src_ref, src_row, dst_ref, dst_row, sem):
    return pltpu.make_async_copy(src_ref.at[pl.ds(src_row, 1)], dst_ref.at[pl.ds(dst_row, 1)], sem)


def _scatter_kernel(last_ref, slot_ref, h_ref, xs_ref, zero_ref, sem):
    tm = h_ref.shape[0]

    @pl.when(pl.program_id(0) == 0)
    def _():
        zero_ref[...] = jnp.zeros_like(zero_ref)

        def fill(e, c):
            @pl.when(last_ref[e] >= 0)
            def _():
                pltpu.make_async_copy(zero_ref, xs_ref.at[pl.ds(last_ref[e], MOE_BLOCK)], sem).start()
            return c

        def drain(e, c):
            @pl.when(last_ref[e] >= 0)
            def _():
                pltpu.make_async_copy(zero_ref, xs_ref.at[pl.ds(0, MOE_BLOCK)], sem).wait()
            return c

        lax.fori_loop(0, N_EXPERTS, fill, 0)
        lax.fori_loop(0, N_EXPERTS, drain, 0)

        n_blocks = xs_ref.shape[0] // MOE_BLOCK

        def fill_unused(b, c):
            pltpu.make_async_copy(zero_ref, xs_ref.at[pl.ds(b * MOE_BLOCK, MOE_BLOCK)], sem).start()
            return c

        def drain_unused(b, c):
            pltpu.make_async_copy(zero_ref, xs_ref.at[pl.ds(0, MOE_BLOCK)], sem).wait()
            return c

        lax.fori_loop(last_ref[N_EXPERTS], n_blocks, fill_unused, 0)
        lax.fori_loop(last_ref[N_EXPERTS], n_blocks, drain_unused, 0)

    def start(r, c):
        _row_copy(h_ref, r, xs_ref, slot_ref[0, 0, r], sem).start(priority=0)
        _row_copy(h_ref, r, xs_ref, slot_ref[0, 1, r], sem).start(priority=1)
        return c

    def wait(r, c):
        _row_copy(h_ref, 0, xs_ref, 0, sem).wait()
        _row_copy(h_ref, 0, xs_ref, 0, sem).wait()
        return c

    lax.fori_loop(0, tm, start, 0, unroll=DMA_UNROLL)
    lax.fori_loop(0, tm, wait, 0, unroll=DMA_UNROLL)


def _slot(start_ref, idx_ref, k, r):
    return start_ref[idx_ref[0, k, r]] + idx_ref[0, 2 + k, r]


def moe_scatter(h_slab, slots, last_block_start, n_slots):
    m = h_slab.shape[0]
    tm = ROUTE_TILE
    return pl.pallas_call(
        _scatter_kernel,
        grid_spec=pltpu.PrefetchScalarGridSpec(
            num_scalar_prefetch=1,
            grid=(m // tm,),
            in_specs=[pl.BlockSpec((1, 2, tm), lambda i, last: (i, 0, 0), memory_space=pltpu.SMEM),
                      pl.BlockSpec((tm,) + SLAB, lambda i, last: (i, 0, 0))],
            out_specs=pl.BlockSpec(memory_space=pl.ANY),
            scratch_shapes=[pltpu.VMEM((MOE_BLOCK,) + SLAB, F32), pltpu.SemaphoreType.DMA(())]),
        out_shape=jax.ShapeDtypeStruct((n_slots,) + SLAB, F32),
        compiler_params=_cparams("arbitrary"),
        name="moe_scatter",
    )(last_block_start, slots, h_slab)


def _expert_kernel(be_ref, nused_ref, xs_ref, wgu_ref, wd_ref, y_ref, wgu16_ref, wd16_ref):
    i = pl.program_id(0)
    changed = jnp.logical_or(i == 0, be_ref[i] != be_ref[jnp.maximum(i - 1, 0)])

    @pl.when(changed)
    def _():
        wgu16_ref[...] = wgu_ref[0, 0].astype(BF16)
        wd16_ref[...] = wd_ref[0, 0].astype(BF16)

    @pl.when(i < nused_ref[0])
    def _():
        x = xs_ref[...].reshape(MOE_BLOCK, D_MODEL).astype(BF16)
        gu = _dot(x, wgu16_ref[...])
        act = _silu(gu[:, :EXPERT_FF]) * gu[:, EXPERT_FF:]
        y = _dot(act.astype(BF16), wd16_ref[...])
        y_ref[...] = y.reshape(y_ref.shape)

    @pl.when(i >= nused_ref[0])
    def _():
        y_ref[...] = jnp.zeros_like(y_ref)


def moe_experts(xs, blk_expert, n_used, w_gate_up, w_down, layer):
    n_slots = xs.shape[0]
    n_blocks = n_slots // MOE_BLOCK
    d = D_MODEL
    return pl.pallas_call(
        _expert_kernel,
        grid_spec=pltpu.PrefetchScalarGridSpec(
            num_scalar_prefetch=2,
            grid=(n_blocks,),
            in_specs=[pl.BlockSpec((MOE_BLOCK,) + SLAB, lambda i, be, nu: (jnp.minimum(i, nu[0] - 1), 0, 0)),
                      pl.BlockSpec((1, 1, d, 2 * EXPERT_FF), lambda i, be, nu: (layer, be[i], 0, 0)),
                      pl.BlockSpec((1, 1, EXPERT_FF, d), lambda i, be, nu: (layer, be[i], 0, 0))],
            out_specs=pl.BlockSpec((MOE_BLOCK,) + SLAB, lambda i, be, nu: (i, 0, 0)),
            scratch_shapes=[pltpu.VMEM((d, 2 * EXPERT_FF), BF16), pltpu.VMEM((EXPERT_FF, d), BF16)]),
        out_shape=jax.ShapeDtypeStruct((n_slots,) + SLAB, F32),
        compiler_params=_cparams("arbitrary"),
        name="moe_experts",
    )(blk_expert, n_used, xs, w_gate_up, w_down)


def _combine_kernel(start_ref, idx_ref, x_ref, mod_ref, wgt_ref, fnw_ref, yb_ref, o_ref, y1_ref, y2_ref, sem, *,
                    final):
    tm = x_ref.shape[0]

    def start(r, c):
        _row_copy(yb_ref, _slot(start_ref, idx_ref, 0, r), y1_ref, r, sem).start(priority=0)
        _row_copy(yb_ref, _slot(start_ref, idx_ref, 1, r), y2_ref, r, sem).start(priority=1)
        return c

    def wait(r, c):
        _row_copy(yb_ref, 0, y1_ref, 0, sem).wait()
        _row_copy(yb_ref, 0, y2_ref, 0, sem).wait()
        return c

    lax.fori_loop(0, tm, start, 0, unroll=DMA_UNROLL)
    lax.fori_loop(0, tm, wait, 0, unroll=DMA_UNROLL)
    w1 = wgt_ref[:, 0:1]
    w2 = wgt_ref[:, 1:2]
    y = y1_ref[...].reshape(x_ref.shape) * w1 + y2_ref[...].reshape(x_ref.shape) * w2
    o_ref[...] = x_ref[...] + mod_ref[0, 5:6, :] * y
    if final:
        o_ref[...] = _rms(o_ref[...]) * fnw_ref[...]


def moe_combine(x2d, mod, yb, idx, pad_start, weights, final_norm_w, seq, final):
    m, d = x2d.shape
    tm = ROUTE_TILE
    per_seq = seq // tm
    return pl.pallas_call(
        functools.partial(_combine_kernel, final=final),
        grid_spec=pltpu.PrefetchScalarGridSpec(
            num_scalar_prefetch=1,
            grid=(m // tm,),
            in_specs=[pl.BlockSpec((1, 4, tm), lambda i, ps: (i, 0, 0), memory_space=pltpu.SMEM),
                      pl.BlockSpec((tm, d), lambda i, ps: (i, 0)),
                      pl.BlockSpec((1, 6, d), lambda i, ps: (i // per_seq, 0, 0)),
                      pl.BlockSpec((tm, 2), lambda i, ps: (i, 0)),
                      pl.BlockSpec((1, d), lambda i, ps: (0, 0)),
                      pl.BlockSpec(memory_space=pl.ANY)],
            out_specs=pl.BlockSpec((tm, d), lambda i, ps: (i, 0)),
            scratch_shapes=[pltpu.VMEM((tm,) + SLAB, F32), pltpu.VMEM((tm,) + SLAB, F32),
                            pltpu.SemaphoreType.DMA(())]),
        out_shape=jax.ShapeDtypeStruct((m, d), F32),
        compiler_params=_cparams("arbitrary"),
        name="moe_combine",
    )(pad_start, idx, x2d, mod, weights, final_norm_w, yb)


def hier_moe_layer(x2d, mod, norm_w, w_router_t, b_router, w_gate_up, w_down, layer, final_norm_w, seq, final):
    m = x2d.shape[0]
    h_slab, idx, wgt, counts = moe_route(x2d, mod, norm_w, w_router_t, b_router, seq)
    counts = counts[:, 0].astype(jnp.int32)
    padded = (counts + MOE_BLOCK - 1) // MOE_BLOCK * MOE_BLOCK
    pad_end = jnp.cumsum(padded)
    pad_start = (pad_end - padded).astype(jnp.int32)
    n_blocks = 2 * m // MOE_BLOCK + N_EXPERTS
    blk_start = jnp.arange(n_blocks, dtype=jnp.int32) * MOE_BLOCK
    blk_expert = jnp.minimum(jnp.sum((pad_end[None, :] <= blk_start[:, None]).astype(jnp.int32), axis=1),
                             N_EXPERTS - 1)
    n_used = (pad_end[-1:] // MOE_BLOCK).astype(jnp.int32)
    fill_plan = jnp.concatenate([jnp.where(padded > 0, pad_end - MOE_BLOCK, -1), n_used]).astype(jnp.int32)
    slots = pad_start[idx[:, 0:2, :]] + idx[:, 2:4, :]
    xs = moe_scatter(h_slab, slots, fill_plan, n_blocks * MOE_BLOCK)
    yb = moe_experts(xs, blk_expert, n_used, w_gate_up, w_down, layer)
    return moe_combine(x2d, mod, yb, idx, pad_start, wgt.T, final_norm_w, seq, final)


def _layer_params(l, p):
    w_in = p['w_in'][l]
    w_gate = w_in[:, PROJ_MAIN:].T
    w_router_t = jnp.zeros((ROUTER_ROWS, D_MODEL), F32)
    w_router_t = w_router_t.at[:N_EXPERTS].set(p['moe_w_expert'][l].T)
    w_router_t = w_router_t.at[N_EXPERTS:N_EXPERTS + MOE_GROUPS].set(p['moe_w_group'][l].T)
    w_router_hi = w_router_t.astype(BF16)
    w_router_t = jnp.concatenate([w_router_hi, (w_router_t - w_router_hi.astype(F32)).astype(BF16)], axis=0)
    b_router = jnp.zeros((ROUTER_ROWS, 1), F32)
    b_router = b_router.at[:N_EXPERTS, 0].set(p['moe_b_expert'][l])
    b_router = b_router.at[N_EXPERTS:N_EXPERTS + MOE_GROUPS, 0].set(p['moe_b_group'][l])
    s5 = [s5_discretise(p['s5_lam_re'][l, d], p['s5_lam_im'][l, d], p['s5_log_step'][l, d],
                        p['s5_b_re'][l, d], p['s5_b_im'][l, d], p['s5_c_re'][l, d], p['s5_c_im'][l, d])
          for d in range(2)]
    return dict(
        norm_mix=p['norm_mix'][l][None, :], norm_ffn=p['norm_ffn'][l][None, :],
        w_main=w_in[:, :PROJ_MAIN].astype(BF16), w_gate=w_gate.astype(BF16),
        w_out=p['w_out'][l].astype(BF16), s5=s5,
        s5_d=p['s5_d'][l][None, :], s5_w_glu=p['s5_w_glu'][l].astype(BF16), s5_b_glu=p['s5_b_glu'][l][None, :],
        conv_w=p['conv_w'][l], gdn_conv_w=p['gdn_conv_w'][l], gdn_a_log=p['gdn_a_log'][l],
        gdn_dt_bias=p['gdn_dt_bias'][l], gdn_norm_w=p['gdn_norm_w'][l][None, :],
        w_router_t=w_router_t, b_router=b_router,
        w_gate_up=p['moe_w_gate_up'], w_down=p['moe_w_down'])


def _trunk(x, mods, layers, final_norm_w):
    nb, seq, d = x.shape
    m = nb * seq
    x2d = x.reshape(m, d)
    for l, lp in enumerate(layers):
        mod = mods[l].reshape(nb, 6, d)
        proj, u_t, pre_t = in_projection(x2d, mod, lp['norm_mix'], lp['w_main'], lp['w_gate'], seq)
        (a_f, bm_f, cm_f), (a_b, bm_b, cm_b) = lp['s5']
        y_s5 = s5_mixer(u_t, a_f, bm_f, cm_f, a_b, bm_b, cm_b, lp['s5_d'], lp['s5_w_glu'], lp['s5_b_glu'])
        qkv = gdn_prepare(proj, lp['gdn_conv_w'], seq).reshape(nb, seq, 3 * GDN_WIDTH)
        gates = gdn_gates(pre_t, lp['gdn_a_log'], lp['gdn_dt_bias'])
        uwq_f, uwq_b, kt_f, kt_b, intra, eg = gdn_local(qkv, gates.transpose(0, 2, 1), gates)
        o_f, o_b = gdn_state_scan(uwq_f, uwq_b, kt_f, kt_b, intra, eg)
        x2d = out_projection(x2d, mod, y_s5, proj, lp['conv_w'], o_f.reshape(m, GDN_WIDTH),
                             o_b.reshape(m, GDN_WIDTH), lp['gdn_norm_w'], lp['w_out'], seq)
        x2d = hier_moe_layer(x2d, mod, lp['norm_ffn'], lp['w_router_t'], lp['b_router'],
                             lp['w_gate_up'], lp['w_down'], l, final_norm_w, seq, final=(l == len(layers) - 1))
    return x2d.reshape(nb, seq, d)


def kernel(x_prompt, x_sample, c_prompt, c_sample, w_ada, b_ada, norm_mix, norm_ffn, w_in, w_out, s5_lam_re, s5_lam_im, s5_log_step, s5_b_re, s5_b_im, s5_c_re, s5_c_im, s5_d, s5_w_glu, s5_b_glu, conv_w, gdn_conv_w, gdn_a_log, gdn_dt_bias, gdn_norm_w, moe_w_group, moe_b_group, moe_w_expert, moe_b_expert, moe_w_gate_up, moe_w_down, final_norm):
    p = dict(norm_mix=norm_mix, norm_ffn=norm_ffn, w_in=w_in, w_out=w_out, s5_lam_re=s5_lam_re,
             s5_lam_im=s5_lam_im, s5_log_step=s5_log_step, s5_b_re=s5_b_re, s5_b_im=s5_b_im, s5_c_re=s5_c_re,
             s5_c_im=s5_c_im, s5_d=s5_d, s5_w_glu=s5_w_glu, s5_b_glu=s5_b_glu, conv_w=conv_w,
             gdn_conv_w=gdn_conv_w, gdn_a_log=gdn_a_log, gdn_dt_bias=gdn_dt_bias, gdn_norm_w=gdn_norm_w,
             moe_w_group=moe_w_group, moe_b_group=moe_b_group, moe_w_expert=moe_w_expert,
             moe_b_expert=moe_b_expert, moe_w_gate_up=moe_w_gate_up, moe_w_down=moe_w_down)
    depth = w_ada.shape[0]
    layers = [_layer_params(l, p) for l in range(depth)]
    nbp = c_prompt.shape[0]
    mods = ada_modulation(jnp.concatenate([c_prompt, c_sample], axis=0), w_ada, b_ada)
    fnw = final_norm[None, :]
    y_prompt = _trunk(x_prompt, mods[:, :nbp], layers, fnw)
    y_sample = _trunk(x_sample, mods[:, nbp:], layers, fnw)
    return (y_prompt, y_sample)
```

```python
import functools

import jax
import jax.numpy as jnp
from jax import lax
from jax.experimental import pallas as pl
from jax.experimental.pallas import tpu as pltpu

F32 = jnp.float32
BF16 = jnp.bfloat16
HIGHEST = lax.Precision.HIGHEST

D_MODEL = 1024
DEPTH = 4
S5_WIDTH = 256
S5_GROUP = 16
S5_GROUPS = 16
S5_STATE = 64
S5_LANES = S5_GROUPS * S5_STATE
CONV_WIDTH = 256
GDN_WIDTH = 512
GDN_HEADS = 4
GDN_HEAD_DIM = 128
GDN_CHUNK = 64
PROJ_MAIN = S5_WIDTH + 3 * CONV_WIDTH + 4 * GDN_WIDTH
N_GATE_COLS = 4 * GDN_HEADS
MOE_GROUPS = 4
EXPERTS_PER_GROUP = 8
N_EXPERTS = 32
EXPERT_FF = 512
MOE_BLOCK = 512
EPS = 1e-6

SUBLANES = 8
LANES = 128
SLAB = (SUBLANES, LANES)
VMEM_LIMIT = 56 * 1024 * 1024

ROW_TILE = 512
S5_TIME_TILE = 64
GDN_STATE_CHUNKS = 8
ROUTE_TILE = 512
ROUTER_ROWS = N_EXPERTS + 16
DMA_UNROLL = 8


def _cparams(*sem):
    return pltpu.CompilerParams(dimension_semantics=sem, vmem_limit_bytes=VMEM_LIMIT)


def _dot(a, b, **kw):
    return jnp.dot(a, b, preferred_element_type=F32, **kw)


def _dot_nt(a, b, **kw):
    return lax.dot_general(a, b, (((1,), (1,)), ((), ())), preferred_element_type=F32, **kw)


def _dot_tn(a, b, **kw):
    return lax.dot_general(a, b, (((0,), (0,)), ((), ())), preferred_element_type=F32, **kw)


def _sigmoid(x):
    return 0.5 * jnp.tanh(0.5 * x) + 0.5


def _silu(x):
    return x * _sigmoid(x)


def _gelu_tanh(x):
    return 0.5 * x * (1.0 + jnp.tanh(0.7978845608028654 * (x + 0.044715 * (x * x * x))))


def _softplus(x):
    return jnp.maximum(x, 0.0) + jnp.log(1.0 + jnp.exp(-jnp.abs(x)))


def _rms(x):
    return x * lax.rsqrt(jnp.mean(x * x, axis=-1, keepdims=True) + EPS)


def _ada_kernel(c_ref, w_ref, b_ref, o_ref):
    o_ref[0] = _dot(_silu(c_ref[...]), w_ref[0], precision=HIGHEST) + b_ref[0]


def ada_modulation(c_all, w_ada, b_ada):
    nb = c_all.shape[0]
    depth, d, n = w_ada.shape
    tn = 1536
    return pl.pallas_call(
        _ada_kernel,
        grid=(depth, n // tn),
        in_specs=[pl.BlockSpec((nb, d), lambda l, j: (0, 0)),
                  pl.BlockSpec((1, d, tn), lambda l, j: (l, 0, j)),
                  pl.BlockSpec((1, 1, tn), lambda l, j: (l, 0, j))],
        out_specs=pl.BlockSpec((1, nb, tn), lambda l, j: (l, 0, j)),
        out_shape=jax.ShapeDtypeStruct((depth, nb, n), F32),
        compiler_params=_cparams("parallel", "parallel"),
        name="ada_modulation",
    )(c_all, w_ada, b_ada.reshape(depth, 1, n))


def _in_proj_kernel(x_ref, mod_ref, nw_ref, w_ref, wg_ref, proj_ref, u_ref, gate_ref):
    h = _rms(x_ref[...]) * nw_ref[...] * (1.0 + mod_ref[0, 1:2, :]) + mod_ref[0, 0:1, :]
    hb = h.astype(BF16)
    proj = _dot(hb, w_ref[...]).astype(proj_ref.dtype)
    proj_ref[...] = proj
    u_ref[...] = proj[:, :S5_WIDTH]
    gate_ref[0] = _dot_nt(wg_ref[...], hb)


def in_projection(x2d, mod, norm_w, w_main, w_gate, seq):
    m, d = x2d.shape
    tm = ROW_TILE
    per_seq = seq // tm
    nb = m // seq
    return pl.pallas_call(
        _in_proj_kernel,
        grid=(m // tm,),
        in_specs=[pl.BlockSpec((tm, d), lambda i: (i, 0)),
                  pl.BlockSpec((1, 6, d), lambda i: (i // per_seq, 0, 0)),
                  pl.BlockSpec((1, d), lambda i: (0, 0)),
                  pl.BlockSpec((d, PROJ_MAIN), lambda i: (0, 0)),
                  pl.BlockSpec((N_GATE_COLS, d), lambda i: (0, 0))],
        out_specs=[pl.BlockSpec((tm, PROJ_MAIN), lambda i: (i, 0)),
                   pl.BlockSpec((tm, S5_WIDTH), lambda i: (i % per_seq, i // per_seq)),
                   pl.BlockSpec((1, N_GATE_COLS, tm), lambda i: (i // per_seq, 0, i % per_seq))],
        out_shape=[jax.ShapeDtypeStruct((m, PROJ_MAIN), BF16),
                   jax.ShapeDtypeStruct((seq, nb * S5_WIDTH), BF16),
                   jax.ShapeDtypeStruct((nb, N_GATE_COLS, seq), F32)],
        compiler_params=_cparams("parallel"),
        name="in_projection",
    )(x2d, mod, norm_w, w_main, w_gate)


def _s5_scan(x_ref, h0_re, h0_im, a_re, a_im, steps, reverse):
    def body(s, carry):
        hr, hi = carry
        t = (steps - 1 - s) if reverse else s
        rows = pl.ds(pl.multiple_of(t * SUBLANES, SUBLANES), SUBLANES)
        xr = x_ref[rows, 0:S5_LANES]
        xi = x_ref[rows, S5_LANES:2 * S5_LANES]
        nr = a_re * hr - a_im * hi + xr
        ni = a_re * hi + a_im * hr + xi
        x_ref[rows, 0:S5_LANES] = nr
        x_ref[rows, S5_LANES:2 * S5_LANES] = ni
        return nr, ni
    return lax.fori_loop(0, steps, body, (h0_re, h0_im), unroll=4)


def _s5_direction(u_ref, a_ref, bmat_ref, cmat_ref, x_ref, st_ref, reverse):
    tc = u_ref.shape[0]
    w = S5_WIDTH
    nb = u_ref.shape[1] // w
    first = pl.program_id(0) == 0

    @pl.when(first)
    def _():
        st_ref[...] = jnp.zeros_like(st_ref)

    u = u_ref[...].astype(F32).reshape(tc * nb, w)
    x_ref[...] = _dot(u.astype(BF16), bmat_ref[...])
    a_re = jnp.broadcast_to(a_ref[0:1, :], (nb, S5_LANES))
    a_im = jnp.broadcast_to(a_ref[1:2, :], (nb, S5_LANES))
    hr, hi = _s5_scan(x_ref, st_ref[0], st_ref[1], a_re, a_im, tc, reverse)
    st_ref[0] = hr
    st_ref[1] = hi
    return u, _dot(x_ref[...].astype(BF16), cmat_ref[...])


def _s5_fwd_kernel(u_ref, a_ref, bmat_ref, cmat_ref, y_ref, x_ref, st_ref):
    _, y = _s5_direction(u_ref, a_ref, bmat_ref, cmat_ref, x_ref, st_ref, reverse=False)
    y_ref[...] = y.reshape(y_ref.shape)


def _s5_bwd_kernel(u_ref, yf_ref, a_ref, bmat_ref, cmat_ref, d_ref, wglu_ref, bglu_ref, o_ref, x_ref, st_ref):
    u, y = _s5_direction(u_ref, a_ref, bmat_ref, cmat_ref, x_ref, st_ref, reverse=True)
    y = y + u * d_ref[...] + yf_ref[...].reshape(u.shape)
    y = _gelu_tanh(y)
    out = y * _sigmoid(_dot(y.astype(BF16), wglu_ref[...]) + bglu_ref[...])
    o_ref[...] = out.reshape(o_ref.shape).astype(o_ref.dtype)


def s5_mixer(u_t, a_f, bmat_f, cmat_f, a_b, bmat_b, cmat_b, d_skip, w_glu, b_glu):
    seq = u_t.shape[0]
    w = S5_WIDTH
    nb = u_t.shape[1] // w
    assert nb == SUBLANES, "the scan keeps one batch row per sublane"
    tc = S5_TIME_TILE
    n = seq // tc
    rows = tc * nb
    const2 = lambda i: (0, 0)
    par_specs = [pl.BlockSpec((2, S5_LANES), const2),
                 pl.BlockSpec((w, 2 * S5_LANES), const2),
                 pl.BlockSpec((2 * S5_LANES, w), const2)]
    scratch = [pltpu.VMEM((rows, 2 * S5_LANES), F32), pltpu.VMEM((2, nb, S5_LANES), F32)]
    y_f = pl.pallas_call(
        _s5_fwd_kernel,
        grid=(n,),
        in_specs=[pl.BlockSpec((tc, nb * w), lambda i: (i, 0))] + par_specs,
        out_specs=pl.BlockSpec((tc, nb, w), lambda i: (i, 0, 0)),
        out_shape=jax.ShapeDtypeStruct((seq, nb, w), F32),
        scratch_shapes=scratch,
        compiler_params=_cparams("arbitrary"),
        name="s5_forward",
    )(u_t, a_f, bmat_f, cmat_f)
    rev = lambda i: (n - 1 - i, 0, 0)
    rev2 = lambda i: (n - 1 - i, 0)
    return pl.pallas_call(
        _s5_bwd_kernel,
        grid=(n,),
        in_specs=[pl.BlockSpec((tc, nb * w), rev2), pl.BlockSpec((tc, nb, w), rev)] + par_specs
                 + [pl.BlockSpec((1, w), const2), pl.BlockSpec((w, w), const2), pl.BlockSpec((1, w), const2)],
        out_specs=pl.BlockSpec((tc, nb * w), rev2),
        out_shape=jax.ShapeDtypeStruct((seq, nb * w), BF16),
        scratch_shapes=scratch,
        compiler_params=_cparams("arbitrary"),
        name="s5_backward_glu",
    )(u_t, y_f, a_b, bmat_b, cmat_b, d_skip, w_glu, b_glu)


def s5_discretise(lam_re, lam_im, log_step, b_re, b_im, c_re, c_im):
    g, p, h = S5_GROUPS, S5_STATE, S5_GROUP
    lr, li = lam_re.astype(F32), lam_im.astype(F32)
    dt = jnp.exp(log_step.astype(F32))[:, None]
    mag = jnp.exp(lr * dt)
    abr, abi = mag * jnp.cos(li * dt), mag * jnp.sin(li * dt)
    den = lr * lr + li * li
    cr = ((abr - 1.0) * lr + abi * li) / den
    ci = (abi * lr - (abr - 1.0) * li) / den
    bbr = cr[..., None] * b_re - ci[..., None] * b_im
    bbi = cr[..., None] * b_im + ci[..., None] * b_re
    eye = jnp.eye(g, dtype=F32)
    to_b = lambda t: jnp.einsum('gph,gk->ghkp', t, eye).reshape(g * h, g * p)
    bmat = jnp.concatenate([to_b(bbr), to_b(bbi)], axis=1)
    to_c = lambda t: jnp.einsum('ghp,gk->gpkh', t, eye).reshape(g * p, g * h)
    cmat = jnp.concatenate([to_c(c_re.astype(F32)), -to_c(c_im.astype(F32))], axis=0)
    a = jnp.stack([abr.reshape(g * p), abi.reshape(g * p)])
    return a, bmat.astype(BF16), cmat.astype(BF16)


def _conv3_rows(x, prev_row, next_row, w_ref):
    tm = x.shape[0]
    rows = lax.broadcasted_iota(jnp.int32, x.shape, 0)
    xp = jnp.where(rows == 0, prev_row, pltpu.roll(x, 1, 0))
    xn = jnp.where(rows == tm - 1, next_row, pltpu.roll(x, tm - 1, 0))
    return xp * w_ref[0:1, :] + x * w_ref[1:2, :] + xn * w_ref[2:3, :]


HALO_ROWS = 16
GDN_PREP_ROWS = 64


def _halo_specs(tm, width, col_block, n_rows):
    per = tm // HALO_ROWS
    last = n_rows // HALO_ROWS - 1
    prev = pl.BlockSpec((HALO_ROWS, width), lambda i, *_: (jnp.maximum(i * per - 1, 0), col_block(*_)))
    nxt = pl.BlockSpec((HALO_ROWS, width), lambda i, *_: (jnp.minimum((i + 1) * per, last), col_block(*_)))
    return prev, nxt


def _gdn_prep_kernel(x_ref, xp_ref, xn_ref, w_ref, o_ref, *, seq):
    tm = x_ref.shape[0]
    j = pl.program_id(1)
    t0 = (pl.program_id(0) * tm) % seq
    first_tile = t0 == 0
    last_tile = t0 + tm == seq
    normalise = j < 2
    scale = jnp.where(j == 0, GDN_HEAD_DIM ** -0.5, 1.0)
    rc = GDN_PREP_ROWS
    row = lambda ref, r, cols: ref[r:r + 1, cols].astype(F32)
    for h in range(GDN_HEADS):
        cols = slice(h * GDN_HEAD_DIM, (h + 1) * GDN_HEAD_DIM)
        w = w_ref[:, cols]
        for r0 in range(0, tm, rc):
            prev_row = (jnp.where(first_tile, 0.0, row(xp_ref, HALO_ROWS - 1, cols)) if r0 == 0
                        else row(x_ref, r0 - 1, cols))
            next_row = (jnp.where(last_tile, 0.0, row(xn_ref, 0, cols)) if r0 + rc == tm
                        else row(x_ref, r0 + rc, cols))
            y = _silu(_conv3_rows(x_ref[r0:r0 + rc, cols].astype(F32), prev_row, next_row, w))
            yn = y * (lax.rsqrt(jnp.sum(y * y, axis=-1, keepdims=True) + EPS) * scale)
            o_ref[r0:r0 + rc, cols] = jnp.where(normalise, yn, y).astype(o_ref.dtype)


def gdn_prepare(proj, conv_w, seq):
    m = proj.shape[0]
    tm = ROW_TILE
    first = (S5_WIDTH + 3 * CONV_WIDTH) // GDN_WIDTH
    prev, nxt = _halo_specs(tm, GDN_WIDTH, lambda j: first + j, m)
    return pl.pallas_call(
        functools.partial(_gdn_prep_kernel, seq=seq),
        grid=(m // tm, 3),
        in_specs=[pl.BlockSpec((tm, GDN_WIDTH), lambda i, j: (i, first + j)), prev, nxt,
                  pl.BlockSpec((3, GDN_WIDTH), lambda i, j: (0, j))],
        out_specs=pl.BlockSpec((tm, GDN_WIDTH), lambda i, j: (i, j)),
        out_shape=jax.ShapeDtypeStruct((m, 3 * GDN_WIDTH), BF16),
        compiler_params=_cparams("parallel", "parallel"),
        name="gdn_prepare",
    )(proj, proj, proj, conv_w)


def _gdn_gates_kernel(pre_ref, alog_ref, dtb_ref, o_ref):
    seq = pre_ref.shape[2]
    nh2 = 2 * GDN_HEADS
    g = -jnp.exp(alog_ref[...]) * _softplus(pre_ref[0, 0:nh2, :] + dtb_ref[...])
    o_ref[0, nh2:2 * nh2, :] = _sigmoid(pre_ref[0, nh2:2 * nh2, :])
    s = lax.broadcasted_iota(jnp.int32, (LANES, LANES), 0)
    t = lax.broadcasted_iota(jnp.int32, (LANES, LANES), 1)
    same = (s // GDN_CHUNK) == (t // GDN_CHUNK)
    m_f = jnp.where(same & (s <= t), 1.0, 0.0)
    m_b = jnp.where(same & (s >= t), 1.0, 0.0)
    for jb in range(seq // LANES):
        cols = slice(jb * LANES, (jb + 1) * LANES)
        gb = g[:, cols]
        o_ref[0, 0:GDN_HEADS, cols] = _dot(gb, m_f, precision=HIGHEST)[0:GDN_HEADS]
        o_ref[0, GDN_HEADS:nh2, cols] = _dot(gb, m_b, precision=HIGHEST)[GDN_HEADS:nh2]


def gdn_gates(pre_t, a_log, dt_bias):
    nb, rows, seq = pre_t.shape
    col = lambda t: t.astype(F32).reshape(2 * GDN_HEADS, 1)
    return pl.pallas_call(
        _gdn_gates_kernel,
        grid=(nb,),
        in_specs=[pl.BlockSpec((1, rows, seq), lambda b: (b, 0, 0)),
                  pl.BlockSpec((2 * GDN_HEADS, 1), lambda b: (0, 0)),
                  pl.BlockSpec((2 * GDN_HEADS, 1), lambda b: (0, 0))],
        out_specs=pl.BlockSpec((1, rows, seq), lambda b: (b, 0, 0)),
        out_shape=jax.ShapeDtypeStruct((nb, rows, seq), F32),
        compiler_params=_cparams("parallel"),
        name="gdn_gates",
    )(pre_t, col(a_log), col(dt_bias))


def _nilpotent_inverses(mats, nilpotency):
    n = mats[0].shape[0]
    eye = jnp.where(lax.broadcasted_iota(jnp.int32, (n, n), 0) == lax.broadcasted_iota(jnp.int32, (n, n), 1), 1.0, 0.0)
    ps = [eye - a for a in mats]
    a16 = [a.astype(BF16) for a in mats]
    sqs = [_dot(a, a) for a in a16]
    k = 2
    while True:
        sq16 = [s.astype(BF16) for s in sqs]
        ps = [p + _dot(p.astype(BF16), s) for p, s in zip(ps, sq16)]
        k *= 2
        if k >= nilpotency:
            return ps
        sqs = [_dot(s, s) for s in sq16]


def _gdn_local_kernel(q_ref, k_ref, v_ref, gcol_ref, grow_ref, uwqf_ref, uwqb_ref, ktf_ref, ktb_ref, in_ref, eg_ref):
    r = q_ref.shape[1]
    c = GDN_CHUNK
    hd = GDN_HEAD_DIM
    ii = lax.broadcasted_iota(jnp.int32, (r, r), 0)
    jj = lax.broadcasted_iota(jnp.int32, (r, r), 1)
    same = (ii >= c) == (jj >= c)
    incl = (same & (ii >= jj), same & (ii <= jj))
    strict = (same & (ii > jj), same & (ii < jj))
    first_chunk = lax.broadcasted_iota(jnp.int32, (r, 1), 0) < c
    zeros = jnp.zeros((r, r), F32)
    big_a, rhs, keep = [], [], []
    for h in range(GDN_HEADS):
        cols = slice(h * hd, (h + 1) * hd)
        q16, k16 = q_ref[0, :, cols], k_ref[0, :, cols]
        q, k, v = q16.astype(F32), k16.astype(F32), v_ref[0, :, cols].astype(F32)
        kk = _dot_nt(k16, k16)
        qk = _dot_nt(q16, k16)
        a_dir, rhs_dir = [], []
        for d in range(2):
            lane = d * GDN_HEADS + h
            gc_col = gcol_ref[0, :, lane:lane + 1]
            gc_row = grow_ref[0, lane:lane + 1, :]
            beta = gcol_ref[0, :, 2 * GDN_HEADS + lane:2 * GDN_HEADS + lane + 1]
            decay = jnp.exp(jnp.where(incl[d], gc_col - gc_row, -1e30))
            a_dir.append(jnp.where(strict[d], kk * beta * decay, 0.0))
            intra = jnp.where(incl[d], qk * decay, 0.0)
            ends = (gc_col[c - 1:c], gc_col[2 * c - 1:2 * c]) if d == 0 else (gc_col[0:1], gc_col[c:c + 1])
            g_last = jnp.where(first_chunk, ends[0], ends[1])
            eg = jnp.exp(gc_col)
            kb = k * beta
            rhs_dir.append(jnp.concatenate([v * beta, kb * eg], axis=1))
            keep.append((q * eg, (k * jnp.exp(g_last - gc_col)).T, intra[:, :c] + intra[:, c:]))
            for ch in range(2):
                eg_ref[0, ch, lane:lane + 1, :] = jnp.broadcast_to(jnp.exp(ends[ch]), (1, LANES))
        big_a.append(jnp.concatenate([jnp.concatenate([a_dir[0], zeros], axis=1),
                                      jnp.concatenate([zeros, a_dir[1]], axis=1)], axis=0))
        rhs.append(jnp.concatenate(rhs_dir, axis=0))
    t_inv = _nilpotent_inverses(big_a, c)
    for h in range(GDN_HEADS):
        uw = _dot(t_inv[h].astype(BF16), rhs[h].astype(BF16))
        (qd_f, kt_f, in_f), (qd_b, kt_b, in_b) = keep[2 * h], keep[2 * h + 1]
        uwqf_ref[0, h] = jnp.concatenate([uw[:r], qd_f], axis=1).astype(BF16)
        uwqb_ref[0, h] = jnp.concatenate([uw[r:], qd_b], axis=1).astype(BF16)
        ktf_ref[0, h] = kt_f.astype(BF16)
        ktb_ref[0, h] = kt_b.astype(BF16)
        in_ref[0, h] = jnp.concatenate([in_f, in_b], axis=1).astype(BF16)


def gdn_local(qkv, gates_col, gates_row):
    nb, seq, _ = qkv.shape
    r = 2 * GDN_CHUNK
    hd, nh = GDN_HEAD_DIM, GDN_HEADS
    uwq_sd = jax.ShapeDtypeStruct((nb, nh, seq, 3 * hd), BF16)
    kt_sd = jax.ShapeDtypeStruct((nb, nh, hd, seq), BF16)
    uwq_spec = pl.BlockSpec((1, nh, r, 3 * hd), lambda b, i: (b, 0, i, 0))
    kt_spec = pl.BlockSpec((1, nh, hd, r), lambda b, i: (b, 0, 0, i))
    return pl.pallas_call(
        _gdn_local_kernel,
        grid=(nb, seq // r),
        in_specs=[pl.BlockSpec((1, r, GDN_WIDTH), lambda b, i: (b, i, 0)),
                  pl.BlockSpec((1, r, GDN_WIDTH), lambda b, i: (b, i, 1)),
                  pl.BlockSpec((1, r, GDN_WIDTH), lambda b, i: (b, i, 2)),
                  pl.BlockSpec((1, r, 4 * nh), lambda b, i: (b, i, 0)),
                  pl.BlockSpec((1, 4 * nh, r), lambda b, i: (b, 0, i))],
        out_specs=[uwq_spec, uwq_spec, kt_spec, kt_spec,
                   pl.BlockSpec((1, nh, r, 2 * GDN_CHUNK), lambda b, i: (b, 0, i, 0)),
                   pl.BlockSpec((1, 2, SUBLANES, LANES), lambda b, i: (b, i, 0, 0))],
        out_shape=[uwq_sd, uwq_sd, kt_sd, kt_sd,
                   jax.ShapeDtypeStruct((nb, nh, seq, 2 * GDN_CHUNK), BF16),
                   jax.ShapeDtypeStruct((nb, seq // GDN_CHUNK, SUBLANES, LANES), F32)],
        compiler_params=_cparams("parallel", "parallel"),
        name="gdn_local",
    )(qkv, qkv, qkv, gates_col, gates_row)


def _gdn_state_kernel(uwqf, ktf, inf, egf, uwqb, ktb, inb, egb, of_ref, ob_ref, st_ref):
    @pl.when(pl.program_id(1) == 0)
    def _():
        st_ref[...] = jnp.zeros_like(st_ref)

    c = GDN_CHUNK
    hd = GDN_HEAD_DIM
    nc = uwqf.shape[2] // c
    chains = [(d, h) for d in range(2) for h in range(GDN_HEADS)]
    refs = ((uwqf, ktf, inf, egf, of_ref), (uwqb, ktb, inb, egb, ob_ref))
    states = [st_ref[j] for j in range(len(chains))]
    for step in range(nc):
        sws = []
        for j, (d, h) in enumerate(chains):
            ci = step if d == 0 else nc - 1 - step
            rows = slice(ci * c, (ci + 1) * c)
            uwq = refs[d][0]
            wq = jnp.concatenate([uwq[0, h, rows, hd:2 * hd], uwq[0, h, rows, 2 * hd:3 * hd]], axis=0)
            sws.append(_dot(wq, states[j].astype(BF16)))
        for j, (d, h) in enumerate(chains):
            ci = step if d == 0 else nc - 1 - step
            rows = slice(ci * c, (ci + 1) * c)
            uwq, kt, intra, eg, o_ref = refs[d]
            sw = sws[j]
            v16 = (uwq[0, h, rows, 0:hd].astype(F32) - sw[:c]).astype(BF16)
            out = sw[c:] + _dot(intra[0, h, rows, d * c:(d + 1) * c], v16)
            o_ref[0, rows, h * hd:(h + 1) * hd] = out.astype(o_ref.dtype)
            lane = d * GDN_HEADS + h
            states[j] = states[j] * eg[0, ci, lane:lane + 1, :] + _dot(kt[0, h, :, rows], v16)
    for j in range(len(chains)):
        st_ref[j] = states[j]


def gdn_state_scan(uwq_f, uwq_b, kt_f, kt_b, intra, eg):
    nb, nh, seq, _ = uwq_f.shape
    hd = GDN_HEAD_DIM
    nc = GDN_STATE_CHUNKS
    r = nc * GDN_CHUNK
    n = seq // r

    def specs(pos):
        return [pl.BlockSpec((1, nh, r, 3 * hd), lambda b, i: (b, 0, pos(i), 0)),
                pl.BlockSpec((1, nh, hd, r), lambda b, i: (b, 0, 0, pos(i))),
                pl.BlockSpec((1, nh, r, 2 * GDN_CHUNK), lambda b, i: (b, 0, pos(i), 0)),
                pl.BlockSpec((1, nc, SUBLANES, LANES), lambda b, i: (b, pos(i), 0, 0))]

    fwd = lambda i: i
    bwd = lambda i: n - 1 - i
    out_sd = jax.ShapeDtypeStruct((nb, seq, GDN_WIDTH), BF16)
    return pl.pallas_call(
        _gdn_state_kernel,
        grid=(nb, n),
        in_specs=specs(fwd) + specs(bwd),
        out_specs=[pl.BlockSpec((1, r, GDN_WIDTH), lambda b, i: (b, i, 0)),
                   pl.BlockSpec((1, r, GDN_WIDTH), lambda b, i: (b, n - 1 - i, 0))],
        out_shape=[out_sd, out_sd],
        scratch_shapes=[pltpu.VMEM((2 * nh, hd, hd), F32)],
        compiler_params=_cparams("parallel", "arbitrary"),
        name="gdn_state_scan",
    )(uwq_f, kt_f, intra, eg, uwq_b, kt_b, intra, eg)


def _out_proj_kernel(x_ref, mod_ref, ys5_ref, cx_ref, cb_ref, cc_ref, cxp_ref, cxn_ref, ccp_ref, ccn_ref,
                     cw_ref, of_ref, ob_ref, z_ref, gnw_ref, w_ref, o_ref, *, seq):
    tm = x_ref.shape[0]
    t0 = (pl.program_id(0) * tm) % seq
    f32 = lambda ref, rows=slice(None): ref[rows, :].astype(F32)
    last = slice(HALO_ROWS - 1, HALO_ROWS)
    prev_row = jnp.where(t0 == 0, 0.0, f32(ccp_ref, last) * f32(cxp_ref, last))
    next_row = jnp.where(t0 + tm == seq, 0.0, f32(ccn_ref, slice(0, 1)) * f32(cxn_ref, slice(0, 1)))
    y_conv = f32(cb_ref) * _conv3_rows(f32(cc_ref) * f32(cx_ref), prev_row, next_row, cw_ref)
    o = f32(of_ref) + f32(ob_ref)
    z = f32(z_ref)
    heads = []
    for h in range(GDN_HEADS):
        cols = slice(h * GDN_HEAD_DIM, (h + 1) * GDN_HEAD_DIM)
        heads.append(_rms(o[:, cols]) * gnw_ref[...] * _silu(z[:, cols]))
    y_gdn = jnp.concatenate(heads, axis=1)
    c0, c1 = S5_WIDTH, S5_WIDTH + CONV_WIDTH
    mix = (_dot(ys5_ref[...].astype(BF16), w_ref[0:c0, :])
           + _dot(y_conv.astype(BF16), w_ref[c0:c1, :])
           + _dot(y_gdn.astype(BF16), w_ref[c1:, :]))
    o_ref[...] = x_ref[...] + mod_ref[0, 2:3, :] * mix


def out_projection(x2d, mod, y_s5, proj, conv_w, o_f, o_b, gdn_norm_w, w_out, seq):
    m, d = x2d.shape
    tm = ROW_TILE
    per_seq = seq // tm
    cw = CONV_WIDTH
    row = lambda c: pl.BlockSpec((tm, cw), lambda i: (i, c))
    cxp, cxn = _halo_specs(tm, cw, lambda: 1, m)
    ccp, ccn = _halo_specs(tm, cw, lambda: 3, m)
    gw = GDN_WIDTH
    return pl.pallas_call(
        functools.partial(_out_proj_kernel, seq=seq),
        grid=(m // tm,),
        in_specs=[pl.BlockSpec((tm, d), lambda i: (i, 0)),
                  pl.BlockSpec((1, 6, d), lambda i: (i // per_seq, 0, 0)),
                  pl.BlockSpec((tm, S5_WIDTH), lambda i: (i % per_seq, i // per_seq)),
                  row(1), row(2), row(3), cxp, cxn, ccp, ccn,
                  pl.BlockSpec((3, cw), lambda i: (0, 0)),
                  pl.BlockSpec((tm, gw), lambda i: (i, 0)),
                  pl.BlockSpec((tm, gw), lambda i: (i, 0)),
                  pl.BlockSpec((tm, gw), lambda i: (i, (PROJ_MAIN - gw) // gw)),
                  pl.BlockSpec((1, GDN_HEAD_DIM), lambda i: (0, 0)),
                  pl.BlockSpec((d, d), lambda i: (0, 0))],
        out_specs=pl.BlockSpec((tm, d), lambda i: (i, 0)),
        out_shape=jax.ShapeDtypeStruct((m, d), F32),
        compiler_params=_cparams("parallel"),
        name="out_projection",
    )(x2d, mod, y_s5, proj, proj, proj, proj, proj, proj, proj, conv_w, o_f, o_b, proj, gdn_norm_w, w_out)


def _first_argmax(vals, rows, n):
    m = jnp.max(vals, axis=0, keepdims=True)
    return m, jnp.min(jnp.where(vals == m, rows, n), axis=0, keepdims=True)


def _route_kernel(x_ref, mod_ref, nw_ref, wr_ref, br_ref, h_ref, idx_ref, wgt_ref, cnt_ref, carry_ref, before_ref):
    step = pl.program_id(0)
    tm = x_ref.shape[0]

    @pl.when(step == 0)
    def _():
        carry_ref[...] = jnp.zeros_like(carry_ref)
        earlier = lax.broadcasted_iota(jnp.int32, (tm, tm), 0) < lax.broadcasted_iota(jnp.int32, (tm, tm), 1)
        before_ref[...] = jnp.where(earlier, 1.0, 0.0).astype(BF16)

    h = _rms(x_ref[...]) * nw_ref[...] * (1.0 + mod_ref[0, 4:5, :]) + mod_ref[0, 3:4, :]
    h_ref[...] = h.reshape(h_ref.shape)
    nr = br_ref.shape[0]
    h_hi = h.astype(BF16)
    h_lo = (h - h_hi.astype(F32)).astype(BF16)
    head = _dot_nt(wr_ref[...], h_hi)
    logits = head[:nr] + head[nr:] + _dot_nt(wr_ref[0:nr, :], h_lo) + br_ref[...]
    lg = logits[N_EXPERTS:N_EXPERTS + MOE_GROUPS, :]
    rows_g = lax.broadcasted_iota(jnp.int32, lg.shape, 0)
    g_max, g_idx = _first_argmax(lg, rows_g, MOE_GROUPS)
    g_w = 1.0 / jnp.sum(jnp.exp(lg - g_max), axis=0, keepdims=True)
    le = jnp.zeros((EXPERTS_PER_GROUP, tm), F32)
    for g in range(MOE_GROUPS):
        le = jnp.where(g_idx == g, logits[g * EXPERTS_PER_GROUP:(g + 1) * EXPERTS_PER_GROUP, :], le)
    rows_e = lax.broadcasted_iota(jnp.int32, le.shape, 0)
    m1, i1 = _first_argmax(le, rows_e, EXPERTS_PER_GROUP)
    m2, i2 = _first_argmax(jnp.where(rows_e == i1, -jnp.inf, le), rows_e, EXPERTS_PER_GROUP)
    r = jnp.exp(m2 - m1)
    w1 = g_w / (1.0 + r)
    e1 = g_idx * EXPERTS_PER_GROUP + i1
    e2 = g_idx * EXPERTS_PER_GROUP + i2
    rows_x = lax.broadcasted_iota(jnp.int32, (N_EXPERTS, tm), 0)
    hit1 = rows_x == e1
    hit2 = rows_x == e2
    onehot = jnp.where(hit1 | hit2, 1.0, 0.0)
    ahead = _dot(onehot.astype(BF16), before_ref[...]) + carry_ref[:, 0:1]
    rank1 = jnp.sum(jnp.where(hit1, ahead, 0.0), axis=0, keepdims=True)
    rank2 = jnp.sum(jnp.where(hit2, ahead, 0.0), axis=0, keepdims=True)
    idx_ref[0, 0:1, :] = e1
    idx_ref[0, 1:2, :] = e2
    idx_ref[0, 2:3, :] = rank1.astype(jnp.int32)
    idx_ref[0, 3:4, :] = rank2.astype(jnp.int32)
    wgt_ref[0:1, :] = w1
    wgt_ref[1:2, :] = w1 * r
    carry_ref[...] = carry_ref[...] + jnp.sum(onehot, axis=1, keepdims=True)
    cnt_ref[...] = carry_ref[...]


def moe_route(x2d, mod, norm_w, w_router_t, b_router, seq):
    m, d = x2d.shape
    tm = ROUTE_TILE
    per_seq = seq // tm
    nr = w_router_t.shape[0]
    return pl.pallas_call(
        _route_kernel,
        grid=(m // tm,),
        in_specs=[pl.BlockSpec((tm, d), lambda i: (i, 0)),
                  pl.BlockSpec((1, 6, d), lambda i: (i // per_seq, 0, 0)),
                  pl.BlockSpec((1, d), lambda i: (0, 0)),
                  pl.BlockSpec((nr, d), lambda i: (0, 0)),
                  pl.BlockSpec((nr // 2, 1), lambda i: (0, 0))],
        out_specs=[pl.BlockSpec((tm,) + SLAB, lambda i: (i, 0, 0)),
                   pl.BlockSpec((1, 4, tm), lambda i: (i, 0, 0)),
                   pl.BlockSpec((2, tm), lambda i: (0, i)),
                   pl.BlockSpec((N_EXPERTS, LANES), lambda i: (0, 0))],
        out_shape=[jax.ShapeDtypeStruct((m,) + SLAB, F32),
                   jax.ShapeDtypeStruct((m // tm, 4, tm), jnp.int32),
                   jax.ShapeDtypeStruct((2, m), F32),
                   jax.ShapeDtypeStruct((N_EXPERTS, LANES), F32)],
        scratch_shapes=[pltpu.VMEM((N_EXPERTS, LANES), F32), pltpu.VMEM((tm, tm), BF16)],
        compiler_params=_cparams("arbitrary"),
        name="moe_route",
    )(x2d, mod, norm_w, w_router_t, b_router)


def _row_copy(src_ref, src_row, dst_ref, dst_row, sem):
    return pltpu.make_async_copy(src_ref.at[pl.ds(src_row, 1)], dst_ref.at[pl.ds(dst_row, 1)], sem)


def _scatter_kernel(last_ref, slot_ref, h_ref, xs_ref, zero_ref, sem):
    tm = h_ref.shape[0]

    @pl.when(pl.program_id(0) == 0)
    def _():
        zero_ref[...] = jnp.zeros_like(zero_ref)

        def fill(e, c):
            @pl.when(last_ref[e] >= 0)
            def _():
                pltpu.make_async_copy(zero_ref, xs_ref.at[pl.ds(last_ref[e], MOE_BLOCK)], sem).start()
            return c

        def drain(e, c):
            @pl.when(last_ref[e] >= 0)
            def _():
                pltpu.make_async_copy(zero_ref, xs_ref.at[pl.ds(0, MOE_BLOCK)], sem).wait()
            return c

        lax.fori_loop(0, N_EXPERTS, fill, 0)
        lax.fori_loop(0, N_EXPERTS, drain, 0)

        n_blocks = xs_ref.shape[0] // MOE_BLOCK

        def fill_unused(b, c):
            pltpu.make_async_copy(zero_ref, xs_ref.at[pl.ds(b * MOE_BLOCK, MOE_BLOCK)], sem).start()
            return c

        def drain_unused(b, c):
            pltpu.make_async_copy(zero_ref, xs_ref.at[pl.ds(0, MOE_BLOCK)], sem).wait()
            return c

        lax.fori_loop(last_ref[N_EXPERTS], n_blocks, fill_unused, 0)
        lax.fori_loop(last_ref[N_EXPERTS], n_blocks, drain_unused, 0)

    def start(r, c):
        _row_copy(h_ref, r, xs_ref, slot_ref[0, 0, r], sem).start(priority=0)
        _row_copy(h_ref, r, xs_ref, slot_ref[0, 1, r], sem).start(priority=1)
        return c

    def wait(r, c):
        _row_copy(h_ref, 0, xs_ref, 0, sem).wait()
        _row_copy(h_ref, 0, xs_ref, 0, sem).wait()
        return c

    lax.fori_loop(0, tm, start, 0, unroll=DMA_UNROLL)
    lax.fori_loop(0, tm, wait, 0, unroll=DMA_UNROLL)


def _slot(start_ref, idx_ref, k, r):
    return start_ref[idx_ref[0, k, r]] + idx_ref[0, 2 + k, r]


def moe_scatter(h_slab, slots, last_block_start, n_slots):
    m = h_slab.shape[0]
    tm = ROUTE_TILE
    return pl.pallas_call(
        _scatter_kernel,
        grid_spec=pltpu.PrefetchScalarGridSpec(
            num_scalar_prefetch=1,
            grid=(m // tm,),
            in_specs=[pl.BlockSpec((1, 2, tm), lambda i, last: (i, 0, 0), memory_space=pltpu.SMEM),
                      pl.BlockSpec((tm,) + SLAB, lambda i, last: (i, 0, 0))],
            out_specs=pl.BlockSpec(memory_space=pl.ANY),
            scratch_shapes=[pltpu.VMEM((MOE_BLOCK,) + SLAB, F32), pltpu.SemaphoreType.DMA(())]),
        out_shape=jax.ShapeDtypeStruct((n_slots,) + SLAB, F32),
        compiler_params=_cparams("arbitrary"),
        name="moe_scatter",
    )(last_block_start, slots, h_slab)


def _expert_kernel(be_ref, nused_ref, xs_ref, wgu_ref, wd_ref, y_ref, wgu16_ref, wd16_ref):
    i = pl.program_id(0)
    changed = jnp.logical_or(i == 0, be_ref[i] != be_ref[jnp.maximum(i - 1, 0)])

    @pl.when(changed)
    def _():
        wgu16_ref[...] = wgu_ref[0, 0].astype(BF16)
        wd16_ref[...] = wd_ref[0, 0].astype(BF16)

    @pl.when(i < nused_ref[0])
    def _():
        x = xs_ref[...].reshape(MOE_BLOCK, D_MODEL).astype(BF16)
        gu = _dot(x, wgu16_ref[...])
        act = _silu(gu[:, :EXPERT_FF]) * gu[:, EXPERT_FF:]
        y = _dot(act.astype(BF16), wd16_ref[...])
        y_ref[...] = y.reshape(y_ref.shape)

    @pl.when(i >= nused_ref[0])
    def _():
        y_ref[...] = jnp.zeros_like(y_ref)


def moe_experts(xs, blk_expert, n_used, w_gate_up, w_down, layer):
    n_slots = xs.shape[0]
    n_blocks = n_slots // MOE_BLOCK
    d = D_MODEL
    return pl.pallas_call(
        _expert_kernel,
        grid_spec=pltpu.PrefetchScalarGridSpec(
            num_scalar_prefetch=2,
            grid=(n_blocks,),
            in_specs=[pl.BlockSpec((MOE_BLOCK,) + SLAB, lambda i, be, nu: (jnp.minimum(i, nu[0] - 1), 0, 0)),
                      pl.BlockSpec((1, 1, d, 2 * EXPERT_FF), lambda i, be, nu: (layer, be[i], 0, 0)),
                      pl.BlockSpec((1, 1, EXPERT_FF, d), lambda i, be, nu: (layer, be[i], 0, 0))],
            out_specs=pl.BlockSpec((MOE_BLOCK,) + SLAB, lambda i, be, nu: (i, 0, 0)),
            scratch_shapes=[pltpu.VMEM((d, 2 * EXPERT_FF), BF16), pltpu.VMEM((EXPERT_FF, d), BF16)]),
        out_shape=jax.ShapeDtypeStruct((n_slots,) + SLAB, F32),
        compiler_params=_cparams("arbitrary"),
        name="moe_experts",
    )(blk_expert, n_used, xs, w_gate_up, w_down)


def _combine_kernel(start_ref, idx_ref, x_ref, mod_ref, wgt_ref, fnw_ref, yb_ref, o_ref, y1_ref, y2_ref, sem, *,
                    final):
    tm = x_ref.shape[0]

    def start(r, c):
        _row_copy(yb_ref, _slot(start_ref, idx_ref, 0, r), y1_ref, r, sem).start(priority=0)
        _row_copy(yb_ref, _slot(start_ref, idx_ref, 1, r), y2_ref, r, sem).start(priority=1)
        return c

    def wait(r, c):
        _row_copy(yb_ref, 0, y1_ref, 0, sem).wait()
        _row_copy(yb_ref, 0, y2_ref, 0, sem).wait()
        return c

    lax.fori_loop(0, tm, start, 0, unroll=DMA_UNROLL)
    lax.fori_loop(0, tm, wait, 0, unroll=DMA_UNROLL)
    w1 = wgt_ref[:, 0:1]
    w2 = wgt_ref[:, 1:2]
    y = y1_ref[...].reshape(x_ref.shape) * w1 + y2_ref[...].reshape(x_ref.shape) * w2
    o_ref[...] = x_ref[...] + mod_ref[0, 5:6, :] * y
    if final:
        o_ref[...] = _rms(o_ref[...]) * fnw_ref[...]


def moe_combine(x2d, mod, yb, idx, pad_start, weights, final_norm_w, seq, final):
    m, d = x2d.shape
    tm = ROUTE_TILE
    per_seq = seq // tm
    return pl.pallas_call(
        functools.partial(_combine_kernel, final=final),
        grid_spec=pltpu.PrefetchScalarGridSpec(
            num_scalar_prefetch=1,
            grid=(m // tm,),
            in_specs=[pl.BlockSpec((1, 4, tm), lambda i, ps: (i, 0, 0), memory_space=pltpu.SMEM),
                      pl.BlockSpec((tm, d), lambda i, ps: (i, 0)),
                      pl.BlockSpec((1, 6, d), lambda i, ps: (i // per_seq, 0, 0)),
                      pl.BlockSpec((tm, 2), lambda i, ps: (i, 0)),
                      pl.BlockSpec((1, d), lambda i, ps: (0, 0)),
                      pl.BlockSpec(memory_space=pl.ANY)],
            out_specs=pl.BlockSpec((tm, d), lambda i, ps: (i, 0)),
            scratch_shapes=[pltpu.VMEM((tm,) + SLAB, F32), pltpu.VMEM((tm,) + SLAB, F32),
                            pltpu.SemaphoreType.DMA(())]),
        out_shape=jax.ShapeDtypeStruct((m, d), F32),
        compiler_params=_cparams("arbitrary"),
        name="moe_combine",
    )(pad_start, idx, x2d, mod, weights, final_norm_w, yb)


def hier_moe_layer(x2d, mod, norm_w, w_router_t, b_router, w_gate_up, w_down, layer, final_norm_w, seq, final):
    m = x2d.shape[0]
    h_slab, idx, wgt, counts = moe_route(x2d, mod, norm_w, w_router_t, b_router, seq)
    counts = counts[:, 0].astype(jnp.int32)
    padded = (counts + MOE_BLOCK - 1) // MOE_BLOCK * MOE_BLOCK
    pad_end = jnp.cumsum(padded)
    pad_start = (pad_end - padded).astype(jnp.int32)
    n_blocks = 2 * m // MOE_BLOCK + N_EXPERTS
    blk_start = jnp.arange(n_blocks, dtype=jnp.int32) * MOE_BLOCK
    blk_expert = jnp.minimum(jnp.sum((pad_end[None, :] <= blk_start[:, None]).astype(jnp.int32), axis=1),
                             N_EXPERTS - 1)
    n_used = (pad_end[-1:] // MOE_BLOCK).astype(jnp.int32)
    fill_plan = jnp.concatenate([jnp.where(padded > 0, pad_end - MOE_BLOCK, -1), n_used]).astype(jnp.int32)
    is_expert = idx[:, 0:2, :, None] == jnp.arange(N_EXPERTS, dtype=jnp.int32)
    slots = jnp.sum(jnp.where(is_expert, pad_start, 0), axis=-1) + idx[:, 2:4, :]
    xs = moe_scatter(h_slab, slots, fill_plan, n_blocks * MOE_BLOCK)
    yb = moe_experts(xs, blk_expert, n_used, w_gate_up, w_down, layer)
    return moe_combine(x2d, mod, yb, idx, pad_start, wgt.T, final_norm_w, seq, final)


def _layer_params(l, p):
    w_in = p['w_in'][l]
    w_gate = w_in[:, PROJ_MAIN:].T
    w_router_t = jnp.zeros((ROUTER_ROWS, D_MODEL), F32)
    w_router_t = w_router_t.at[:N_EXPERTS].set(p['moe_w_expert'][l].T)
    w_router_t = w_router_t.at[N_EXPERTS:N_EXPERTS + MOE_GROUPS].set(p['moe_w_group'][l].T)
    w_router_hi = w_router_t.astype(BF16)
    w_router_t = jnp.concatenate([w_router_hi, (w_router_t - w_router_hi.astype(F32)).astype(BF16)], axis=0)
    b_router = jnp.zeros((ROUTER_ROWS, 1), F32)
    b_router = b_router.at[:N_EXPERTS, 0].set(p['moe_b_expert'][l])
    b_router = b_router.at[N_EXPERTS:N_EXPERTS + MOE_GROUPS, 0].set(p['moe_b_group'][l])
    s5 = [s5_discretise(p['s5_lam_re'][l, d], p['s5_lam_im'][l, d], p['s5_log_step'][l, d],
                        p['s5_b_re'][l, d], p['s5_b_im'][l, d], p['s5_c_re'][l, d], p['s5_c_im'][l, d])
          for d in range(2)]
    return dict(
        norm_mix=p['norm_mix'][l][None, :], norm_ffn=p['norm_ffn'][l][None, :],
        w_main=w_in[:, :PROJ_MAIN].astype(BF16), w_gate=w_gate.astype(BF16),
        w_out=p['w_out'][l].astype(BF16), s5=s5,
        s5_d=p['s5_d'][l][None, :], s5_w_glu=p['s5_w_glu'][l].astype(BF16), s5_b_glu=p['s5_b_glu'][l][None, :],
        conv_w=p['conv_w'][l], gdn_conv_w=p['gdn_conv_w'][l], gdn_a_log=p['gdn_a_log'][l],
        gdn_dt_bias=p['gdn_dt_bias'][l], gdn_norm_w=p['gdn_norm_w'][l][None, :],
        w_router_t=w_router_t, b_router=b_router,
        w_gate_up=p['moe_w_gate_up'], w_down=p['moe_w_down'])


def _trunk(x, mods, layers, final_norm_w):
    nb, seq, d = x.shape
    m = nb * seq
    x2d = x.reshape(m, d)
    for l, lp in enumerate(layers):
        mod = mods[l].reshape(nb, 6, d)
        proj, u_t, pre_t = in_projection(x2d, mod, lp['norm_mix'], lp['w_main'], lp['w_gate'], seq)
        (a_f, bm_f, cm_f), (a_b, bm_b, cm_b) = lp['s5']
        y_s5 = s5_mixer(u_t, a_f, bm_f, cm_f, a_b, bm_b, cm_b, lp['s5_d'], lp['s5_w_glu'], lp['s5_b_glu'])
        qkv = gdn_prepare(proj, lp['gdn_conv_w'], seq).reshape(nb, seq, 3 * GDN_WIDTH)
        gates = gdn_gates(pre_t, lp['gdn_a_log'], lp['gdn_dt_bias'])
        uwq_f, uwq_b, kt_f, kt_b, intra, eg = gdn_local(qkv, gates.transpose(0, 2, 1), gates)
        o_f, o_b = gdn_state_scan(uwq_f, uwq_b, kt_f, kt_b, intra, eg)
        x2d = out_projection(x2d, mod, y_s5, proj, lp['conv_w'], o_f.reshape(m, GDN_WIDTH),
                             o_b.reshape(m, GDN_WIDTH), lp['gdn_norm_w'], lp['w_out'], seq)
        x2d = hier_moe_layer(x2d, mod, lp['norm_ffn'], lp['w_router_t'], lp['b_router'],
                             lp['w_gate_up'], lp['w_down'], l, final_norm_w, seq, final=(l == len(layers) - 1))
    return x2d.reshape(nb, seq, d)


def kernel(x_prompt, x_sample, c_prompt, c_sample, w_ada, b_ada, norm_mix, norm_ffn, w_in, w_out, s5_lam_re, s5_lam_im, s5_log_step, s5_b_re, s5_b_im, s5_c_re, s5_c_im, s5_d, s5_w_glu, s5_b_glu, conv_w, gdn_conv_w, gdn_a_log, gdn_dt_bias, gdn_norm_w, moe_w_group, moe_b_group, moe_w_expert, moe_b_expert, moe_w_gate_up, moe_w_down, final_norm):
    p = dict(norm_mix=norm_mix, norm_ffn=norm_ffn, w_in=w_in, w_out=w_out, s5_lam_re=s5_lam_re,
             s5_lam_im=s5_lam_im, s5_log_step=s5_log_step, s5_b_re=s5_b_re, s5_b_im=s5_b_im, s5_c_re=s5_c_re,
             s5_c_im=s5_c_im, s5_d=s5_d, s5_w_glu=s5_w_glu, s5_b_glu=s5_b_glu, conv_w=conv_w,
             gdn_conv_w=gdn_conv_w, gdn_a_log=gdn_a_log, gdn_dt_bias=gdn_dt_bias, gdn_norm_w=gdn_norm_w,
             moe_w_group=moe_w_group, moe_b_group=moe_b_group, moe_w_expert=moe_w_expert,
             moe_b_expert=moe_b_expert, moe_w_gate_up=moe_w_gate_up, moe_w_down=moe_w_down)
    depth = w_ada.shape[0]
    layers = [_layer_params(l, p) for l in range(depth)]
    nbp = c_prompt.shape[0]
    mods = ada_modulation(jnp.concatenate([c_prompt, c_sample], axis=0), w_ada, b_ada)
    fnw = final_norm[None, :]
    y_prompt = _trunk(x_prompt, mods[:, :nbp], layers, fnw)
    y_sample = _trunk(x_sample, mods[:, nbp:], layers, fnw)
    return (y_prompt, y_sample)
```

```python
import functools

import jax
import jax.numpy as jnp
from jax import lax
from jax.experimental import pallas as pl
from jax.experimental.pallas import tpu as pltpu

F32 = jnp.float32
BF16 = jnp.bfloat16
HIGHEST = lax.Precision.HIGHEST

D_MODEL = 1024
DEPTH = 4
S5_WIDTH = 256
S5_GROUP = 16
S5_GROUPS = 16
S5_STATE = 64
S5_LANES = S5_GROUPS * S5_STATE
CONV_WIDTH = 256
GDN_WIDTH = 512
GDN_HEADS = 4
GDN_HEAD_DIM = 128
GDN_CHUNK = 64
PROJ_MAIN = S5_WIDTH + 3 * CONV_WIDTH + 4 * GDN_WIDTH
N_GATE_COLS = 4 * GDN_HEADS
MOE_GROUPS = 4
EXPERTS_PER_GROUP = 8
N_EXPERTS = 32
EXPERT_FF = 512
MOE_BLOCK = 512
EPS = 1e-6

SUBLANES = 8
LANES = 128
SLAB = (SUBLANES, LANES)
VMEM_LIMIT = 56 * 1024 * 1024

ROW_TILE = 512
S5_TIME_TILE = 64
GDN_STATE_CHUNKS = 8
ROUTE_TILE = 512
ROUTER_ROWS = N_EXPERTS + 16
DMA_UNROLL = 8


def _cparams(*sem):
    return pltpu.CompilerParams(dimension_semantics=sem, vmem_limit_bytes=VMEM_LIMIT)


def _dot(a, b, **kw):
    return jnp.dot(a, b, preferred_element_type=F32, **kw)


def _dot_nt(a, b, **kw):
    return lax.dot_general(a, b, (((1,), (1,)), ((), ())), preferred_element_type=F32, **kw)


def _dot_tn(a, b, **kw):
    return lax.dot_general(a, b, (((0,), (0,)), ((), ())), preferred_element_type=F32, **kw)


def _sigmoid(x):
    return 0.5 * jnp.tanh(0.5 * x) + 0.5


def _silu(x):
    return x * _sigmoid(x)


def _gelu_tanh(x):
    return 0.5 * x * (1.0 + jnp.tanh(0.7978845608028654 * (x + 0.044715 * (x * x * x))))


def _softplus(x):
    return jnp.maximum(x, 0.0) + jnp.log(1.0 + jnp.exp(-jnp.abs(x)))


def _rms(x):
    return x * lax.rsqrt(jnp.mean(x * x, axis=-1, keepdims=True) + EPS)


def _ada_kernel(c_ref, w_ref, b_ref, o_ref):
    o_ref[0] = _dot(_silu(c_ref[...]), w_ref[0], precision=HIGHEST) + b_ref[0]


def ada_modulation(c_all, w_ada, b_ada):
    nb = c_all.shape[0]
    depth, d, n = w_ada.shape
    tn = 1536
    return pl.pallas_call(
        _ada_kernel,
        grid=(depth, n // tn),
        in_specs=[pl.BlockSpec((nb, d), lambda l, j: (0, 0)),
                  pl.BlockSpec((1, d, tn), lambda l, j: (l, 0, j)),
                  pl.BlockSpec((1, 1, tn), lambda l, j: (l, 0, j))],
        out_specs=pl.BlockSpec((1, nb, tn), lambda l, j: (l, 0, j)),
        out_shape=jax.ShapeDtypeStruct((depth, nb, n), F32),
        compiler_params=_cparams("parallel", "parallel"),
        name="ada_modulation",
    )(c_all, w_ada, b_ada.reshape(depth, 1, n))


def _in_proj_kernel(x_ref, mod_ref, nw_ref, w_ref, wg_ref, proj_ref, u_ref, gate_ref):
    h = _rms(x_ref[...]) * nw_ref[...] * (1.0 + mod_ref[0, 1:2, :]) + mod_ref[0, 0:1, :]
    hb = h.astype(BF16)
    proj = _dot(hb, w_ref[...]).astype(proj_ref.dtype)
    proj_ref[...] = proj
    u_ref[...] = proj[:, :S5_WIDTH]
    gate_ref[0] = _dot_nt(wg_ref[...], hb)


def in_projection(x2d, mod, norm_w, w_main, w_gate, seq):
    m, d = x2d.shape
    tm = ROW_TILE
    per_seq = seq // tm
    nb = m // seq
    return pl.pallas_call(
        _in_proj_kernel,
        grid=(m // tm,),
        in_specs=[pl.BlockSpec((tm, d), lambda i: (i, 0)),
                  pl.BlockSpec((1, 6, d), lambda i: (i // per_seq, 0, 0)),
                  pl.BlockSpec((1, d), lambda i: (0, 0)),
                  pl.BlockSpec((d, PROJ_MAIN), lambda i: (0, 0)),
                  pl.BlockSpec((N_GATE_COLS, d), lambda i: (0, 0))],
        out_specs=[pl.BlockSpec((tm, PROJ_MAIN), lambda i: (i, 0)),
                   pl.BlockSpec((tm, S5_WIDTH), lambda i: (i % per_seq, i // per_seq)),
                   pl.BlockSpec((1, N_GATE_COLS, tm), lambda i: (i // per_seq, 0, i % per_seq))],
        out_shape=[jax.ShapeDtypeStruct((m, PROJ_MAIN), BF16),
                   jax.ShapeDtypeStruct((seq, nb * S5_WIDTH), BF16),
                   jax.ShapeDtypeStruct((nb, N_GATE_COLS, seq), F32)],
        compiler_params=_cparams("parallel"),
        name="in_projection",
    )(x2d, mod, norm_w, w_main, w_gate)


def _s5_scan(x_ref, h0_re, h0_im, a_re, a_im, steps, reverse):
    def body(s, carry):
        hr, hi = carry
        t = (steps - 1 - s) if reverse else s
        rows = pl.ds(pl.multiple_of(t * SUBLANES, SUBLANES), SUBLANES)
        xr = x_ref[rows, 0:S5_LANES]
        xi = x_ref[rows, S5_LANES:2 * S5_LANES]
        nr = a_re * hr - a_im * hi + xr
        ni = a_re * hi + a_im * hr + xi
        x_ref[rows, 0:S5_LANES] = nr
        x_ref[rows, S5_LANES:2 * S5_LANES] = ni
        return nr, ni
    return lax.fori_loop(0, steps, body, (h0_re, h0_im), unroll=4)


def _s5_direction(u_ref, a_ref, bmat_ref, cmat_ref, x_ref, st_ref, reverse):
    tc = u_ref.shape[0]
    w = S5_WIDTH
    nb = u_ref.shape[1] // w
    first = pl.program_id(0) == 0

    @pl.when(first)
    def _():
        st_ref[...] = jnp.zeros_like(st_ref)

    u = u_ref[...].astype(F32).reshape(tc * nb, w)
    x_ref[...] = _dot(u.astype(BF16), bmat_ref[...])
    a_re = jnp.broadcast_to(a_ref[0:1, :], (nb, S5_LANES))
    a_im = jnp.broadcast_to(a_ref[1:2, :], (nb, S5_LANES))
    hr, hi = _s5_scan(x_ref, st_ref[0], st_ref[1], a_re, a_im, tc, reverse)
    st_ref[0] = hr
    st_ref[1] = hi
    return u, _dot(x_ref[...].astype(BF16), cmat_ref[...])


def _s5_fwd_kernel(u_ref, a_ref, bmat_ref, cmat_ref, y_ref, x_ref, st_ref):
    _, y = _s5_direction(u_ref, a_ref, bmat_ref, cmat_ref, x_ref, st_ref, reverse=False)
    y_ref[...] = y.reshape(y_ref.shape)


def _s5_bwd_kernel(u_ref, yf_ref, a_ref, bmat_ref, cmat_ref, d_ref, wglu_ref, bglu_ref, o_ref, x_ref, st_ref):
    u, y = _s5_direction(u_ref, a_ref, bmat_ref, cmat_ref, x_ref, st_ref, reverse=True)
    y = y + u * d_ref[...] + yf_ref[...].reshape(u.shape)
    y = _gelu_tanh(y)
    out = y * _sigmoid(_dot(y.astype(BF16), wglu_ref[...]) + bglu_ref[...])
    o_ref[...] = out.reshape(o_ref.shape).astype(o_ref.dtype)


def s5_mixer(u_t, a_f, bmat_f, cmat_f, a_b, bmat_b, cmat_b, d_skip, w_glu, b_glu):
    seq = u_t.shape[0]
    w = S5_WIDTH
    nb = u_t.shape[1] // w
    assert nb == SUBLANES, "the scan keeps one batch row per sublane"
    tc = S5_TIME_TILE
    n = seq // tc
    rows = tc * nb
    const2 = lambda i: (0, 0)
    par_specs = [pl.BlockSpec((2, S5_LANES), const2),
                 pl.BlockSpec((w, 2 * S5_LANES), const2),
                 pl.BlockSpec((2 * S5_LANES, w), const2)]
    scratch = [pltpu.VMEM((rows, 2 * S5_LANES), F32), pltpu.VMEM((2, nb, S5_LANES), F32)]
    y_f = pl.pallas_call(
        _s5_fwd_kernel,
        grid=(n,),
        in_specs=[pl.BlockSpec((tc, nb * w), lambda i: (i, 0))] + par_specs,
        out_specs=pl.BlockSpec((tc, nb, w), lambda i: (i, 0, 0)),
        out_shape=jax.ShapeDtypeStruct((seq, nb, w), F32),
        scratch_shapes=scratch,
        compiler_params=_cparams("arbitrary"),
        name="s5_forward",
    )(u_t, a_f, bmat_f, cmat_f)
    rev = lambda i: (n - 1 - i, 0, 0)
    rev2 = lambda i: (n - 1 - i, 0)
    return pl.pallas_call(
        _s5_bwd_kernel,
        grid=(n,),
        in_specs=[pl.BlockSpec((tc, nb * w), rev2), pl.BlockSpec((tc, nb, w), rev)] + par_specs
                 + [pl.BlockSpec((1, w), const2), pl.BlockSpec((w, w), const2), pl.BlockSpec((1, w), const2)],
        out_specs=pl.BlockSpec((tc, nb * w), rev2),
        out_shape=jax.ShapeDtypeStruct((seq, nb * w), BF16),
        scratch_shapes=scratch,
        compiler_params=_cparams("arbitrary"),
        name="s5_backward_glu",
    )(u_t, y_f, a_b, bmat_b, cmat_b, d_skip, w_glu, b_glu)


def s5_discretise(lam_re, lam_im, log_step, b_re, b_im, c_re, c_im):
    g, p, h = S5_GROUPS, S5_STATE, S5_GROUP
    lr, li = lam_re.astype(F32), lam_im.astype(F32)
    dt = jnp.exp(log_step.astype(F32))[:, None]
    mag = jnp.exp(lr * dt)
    abr, abi = mag * jnp.cos(li * dt), mag * jnp.sin(li * dt)
    den = lr * lr + li * li
    cr = ((abr - 1.0) * lr + abi * li) / den
    ci = (abi * lr - (abr - 1.0) * li) / den
    bbr = cr[..., None] * b_re - ci[..., None] * b_im
    bbi = cr[..., None] * b_im + ci[..., None] * b_re
    eye = jnp.eye(g, dtype=F32)
    to_b = lambda t: jnp.einsum('gph,gk->ghkp', t, eye).reshape(g * h, g * p)
    bmat = jnp.concatenate([to_b(bbr), to_b(bbi)], axis=1)
    to_c = lambda t: jnp.einsum('ghp,gk->gpkh', t, eye).reshape(g * p, g * h)
    cmat = jnp.concatenate([to_c(c_re.astype(F32)), -to_c(c_im.astype(F32))], axis=0)
    a = jnp.stack([abr.reshape(g * p), abi.reshape(g * p)])
    return a, bmat.astype(BF16), cmat.astype(BF16)


def _conv3_rows(x, prev_row, next_row, w_ref):
    tm = x.shape[0]
    rows = lax.broadcasted_iota(jnp.int32, x.shape, 0)
    xp = jnp.where(rows == 0, prev_row, pltpu.roll(x, 1, 0))
    xn = jnp.where(rows == tm - 1, next_row, pltpu.roll(x, tm - 1, 0))
    return xp * w_ref[0:1, :] + x * w_ref[1:2, :] + xn * w_ref[2:3, :]


HALO_ROWS = 16
GDN_PREP_ROWS = 64


def _halo_specs(tm, width, col_block, n_rows):
    per = tm // HALO_ROWS
    last = n_rows // HALO_ROWS - 1
    prev = pl.BlockSpec((HALO_ROWS, width), lambda i, *_: (jnp.maximum(i * per - 1, 0), col_block(*_)))
    nxt = pl.BlockSpec((HALO_ROWS, width), lambda i, *_: (jnp.minimum((i + 1) * per, last), col_block(*_)))
    return prev, nxt


def _gdn_prep_kernel(x_ref, xp_ref, xn_ref, w_ref, o_ref, *, seq):
    tm = x_ref.shape[0]
    j = pl.program_id(1)
    t0 = (pl.program_id(0) * tm) % seq
    first_tile = t0 == 0
    last_tile = t0 + tm == seq
    normalise = j < 2
    scale = jnp.where(j == 0, GDN_HEAD_DIM ** -0.5, 1.0)
    rc = GDN_PREP_ROWS
    row = lambda ref, r, cols: ref[r:r + 1, cols].astype(F32)
    for h in range(GDN_HEADS):
        cols = slice(h * GDN_HEAD_DIM, (h + 1) * GDN_HEAD_DIM)
        w = w_ref[:, cols]
        for r0 in range(0, tm, rc):
            prev_row = (jnp.where(first_tile, 0.0, row(xp_ref, HALO_ROWS - 1, cols)) if r0 == 0
                        else row(x_ref, r0 - 1, cols))
            next_row = (jnp.where(last_tile, 0.0, row(xn_ref, 0, cols)) if r0 + rc == tm
                        else row(x_ref, r0 + rc, cols))
            y = _silu(_conv3_rows(x_ref[r0:r0 + rc, cols].astype(F32), prev_row, next_row, w))
            yn = y * (lax.rsqrt(jnp.sum(y * y, axis=-1, keepdims=True) + EPS) * scale)
            o_ref[r0:r0 + rc, cols] = jnp.where(normalise, yn, y).astype(o_ref.dtype)


def gdn_prepare(proj, conv_w, seq):
    m = proj.shape[0]
    tm = ROW_TILE
    first = (S5_WIDTH + 3 * CONV_WIDTH) // GDN_WIDTH
    prev, nxt = _halo_specs(tm, GDN_WIDTH, lambda j: first + j, m)
    return pl.pallas_call(
        functools.partial(_gdn_prep_kernel, seq=seq),
        grid=(m // tm, 3),
        in_specs=[pl.BlockSpec((tm, GDN_WIDTH), lambda i, j: (i, first + j)), prev, nxt,
                  pl.BlockSpec((3, GDN_WIDTH), lambda i, j: (0, j))],
        out_specs=pl.BlockSpec((tm, GDN_WIDTH), lambda i, j: (i, j)),
        out_shape=jax.ShapeDtypeStruct((m, 3 * GDN_WIDTH), BF16),
        compiler_params=_cparams("parallel", "parallel"),
        name="gdn_prepare",
    )(proj, proj, proj, conv_w)


def _gdn_gates_kernel(pre_ref, alog_ref, dtb_ref, o_ref):
    seq = pre_ref.shape[2]
    nh2 = 2 * GDN_HEADS
    g = -jnp.exp(alog_ref[...]) * _softplus(pre_ref[0, 0:nh2, :] + dtb_ref[...])
    o_ref[0, nh2:2 * nh2, :] = _sigmoid(pre_ref[0, nh2:2 * nh2, :])
    s = lax.broadcasted_iota(jnp.int32, (LANES, LANES), 0)
    t = lax.broadcasted_iota(jnp.int32, (LANES, LANES), 1)
    same = (s // GDN_CHUNK) == (t // GDN_CHUNK)
    m_f = jnp.where(same & (s <= t), 1.0, 0.0)
    m_b = jnp.where(same & (s >= t), 1.0, 0.0)
    for jb in range(seq // LANES):
        cols = slice(jb * LANES, (jb + 1) * LANES)
        gb = g[:, cols]
        o_ref[0, 0:GDN_HEADS, cols] = _dot(gb, m_f, precision=HIGHEST)[0:GDN_HEADS]
        o_ref[0, GDN_HEADS:nh2, cols] = _dot(gb, m_b, precision=HIGHEST)[GDN_HEADS:nh2]


def gdn_gates(pre_t, a_log, dt_bias):
    nb, rows, seq = pre_t.shape
    col = lambda t: t.astype(F32).reshape(2 * GDN_HEADS, 1)
    return pl.pallas_call(
        _gdn_gates_kernel,
        grid=(nb,),
        in_specs=[pl.BlockSpec((1, rows, seq), lambda b: (b, 0, 0)),
                  pl.BlockSpec((2 * GDN_HEADS, 1), lambda b: (0, 0)),
                  pl.BlockSpec((2 * GDN_HEADS, 1), lambda b: (0, 0))],
        out_specs=pl.BlockSpec((1, rows, seq), lambda b: (b, 0, 0)),
        out_shape=jax.ShapeDtypeStruct((nb, rows, seq), F32),
        compiler_params=_cparams("parallel"),
        name="gdn_gates",
    )(pre_t, col(a_log), col(dt_bias))


def _nilpotent_inverses(mats, nilpotency):
    n = mats[0].shape[0]
    eye = jnp.where(lax.broadcasted_iota(jnp.int32, (n, n), 0) == lax.broadcasted_iota(jnp.int32, (n, n), 1), 1.0, 0.0)
    ps = [eye - a for a in mats]
    a16 = [a.astype(BF16) for a in mats]
    sqs = [_dot(a, a) for a in a16]
    k = 2
    while True:
        sq16 = [s.astype(BF16) for s in sqs]
        ps = [p + _dot(p.astype(BF16), s) for p, s in zip(ps, sq16)]
        k *= 2
        if k >= nilpotency:
            return ps
        sqs = [_dot(s, s) for s in sq16]


def _gdn_local_kernel(q_ref, k_ref, v_ref, gcol_ref, grow_ref, uwqf_ref, uwqb_ref, ktf_ref, ktb_ref, in_ref, eg_ref):
    r = q_ref.shape[1]
    c = GDN_CHUNK
    hd = GDN_HEAD_DIM
    ii = lax.broadcasted_iota(jnp.int32, (r, r), 0)
    jj = lax.broadcasted_iota(jnp.int32, (r, r), 1)
    same = (ii >= c) == (jj >= c)
    incl = (same & (ii >= jj), same & (ii <= jj))
    strict = (same & (ii > jj), same & (ii < jj))
    first_chunk = lax.broadcasted_iota(jnp.int32, (r, 1), 0) < c
    zeros = jnp.zeros((r, r), F32)
    big_a, rhs, keep = [], [], []
    for h in range(GDN_HEADS):
        cols = slice(h * hd, (h + 1) * hd)
        q16, k16 = q_ref[0, :, cols], k_ref[0, :, cols]
        q, k, v = q16.astype(F32), k16.astype(F32), v_ref[0, :, cols].astype(F32)
        kk = _dot_nt(k16, k16)
        qk = _dot_nt(q16, k16)
        a_dir, rhs_dir = [], []
        for d in range(2):
            lane = d * GDN_HEADS + h
            gc_col = gcol_ref[0, :, lane:lane + 1]
            gc_row = grow_ref[0, lane:lane + 1, :]
            beta = gcol_ref[0, :, 2 * GDN_HEADS + lane:2 * GDN_HEADS + lane + 1]
            decay = jnp.exp(jnp.where(incl[d], gc_col - gc_row, -1e30))
            a_dir.append(jnp.where(strict[d], kk * beta * decay, 0.0))
            intra = jnp.where(incl[d], qk * decay, 0.0)
            ends = (gc_col[c - 1:c], gc_col[2 * c - 1:2 * c]) if d == 0 else (gc_col[0:1], gc_col[c:c + 1])
            g_last = jnp.where(first_chunk, ends[0], ends[1])
            eg = jnp.exp(gc_col)
            kb = k * beta
            rhs_dir.append(jnp.concatenate([v * beta, kb * eg], axis=1))
            keep.append((q * eg, (k * jnp.exp(g_last - gc_col)).T, intra[:, :c] + intra[:, c:]))
            for ch in range(2):
                eg_ref[0, ch, lane:lane + 1, :] = jnp.broadcast_to(jnp.exp(ends[ch]), (1, LANES))
        big_a.append(jnp.concatenate([jnp.concatenate([a_dir[0], zeros], axis=1),
                                      jnp.concatenate([zeros, a_dir[1]], axis=1)], axis=0))
        rhs.append(jnp.concatenate(rhs_dir, axis=0))
    t_inv = _nilpotent_inverses(big_a, c)
    for h in range(GDN_HEADS):
        uw = _dot(t_inv[h].astype(BF16), rhs[h].astype(BF16))
        (qd_f, kt_f, in_f), (qd_b, kt_b, in_b) = keep[2 * h], keep[2 * h + 1]
        uwqf_ref[0, h] = jnp.concatenate([uw[:r], qd_f], axis=1).astype(BF16)
        uwqb_ref[0, h] = jnp.concatenate([uw[r:], qd_b], axis=1).astype(BF16)
        ktf_ref[0, h] = kt_f.astype(BF16)
        ktb_ref[0, h] = kt_b.astype(BF16)
        in_ref[0, h] = jnp.concatenate([in_f, in_b], axis=1).astype(BF16)


def gdn_local(qkv, gates_col, gates_row):
    nb, seq, _ = qkv.shape
    r = 2 * GDN_CHUNK
    hd, nh = GDN_HEAD_DIM, GDN_HEADS
    uwq_sd = jax.ShapeDtypeStruct((nb, nh, seq, 3 * hd), BF16)
    kt_sd = jax.ShapeDtypeStruct((nb, nh, hd, seq), BF16)
    uwq_spec = pl.BlockSpec((1, nh, r, 3 * hd), lambda b, i: (b, 0, i, 0))
    kt_spec = pl.BlockSpec((1, nh, hd, r), lambda b, i: (b, 0, 0, i))
    return pl.pallas_call(
        _gdn_local_kernel,
        grid=(nb, seq // r),
        in_specs=[pl.BlockSpec((1, r, GDN_WIDTH), lambda b, i: (b, i, 0)),
                  pl.BlockSpec((1, r, GDN_WIDTH), lambda b, i: (b, i, 1)),
                  pl.BlockSpec((1, r, GDN_WIDTH), lambda b, i: (b, i, 2)),
                  pl.BlockSpec((1, r, 4 * nh), lambda b, i: (b, i, 0)),
                  pl.BlockSpec((1, 4 * nh, r), lambda b, i: (b, 0, i))],
        out_specs=[uwq_spec, uwq_spec, kt_spec, kt_spec,
                   pl.BlockSpec((1, nh, r, 2 * GDN_CHUNK), lambda b, i: (b, 0, i, 0)),
                   pl.BlockSpec((1, 2, SUBLANES, LANES), lambda b, i: (b, i, 0, 0))],
        out_shape=[uwq_sd, uwq_sd, kt_sd, kt_sd,
                   jax.ShapeDtypeStruct((nb, nh, seq, 2 * GDN_CHUNK), BF16),
                   jax.ShapeDtypeStruct((nb, seq // GDN_CHUNK, SUBLANES, LANES), F32)],
        compiler_params=_cparams("parallel", "parallel"),
        name="gdn_local",
    )(qkv, qkv, qkv, gates_col, gates_row)


def _gdn_state_kernel(uwqf, ktf, inf, egf, uwqb, ktb, inb, egb, of_ref, ob_ref, st_ref):
    @pl.when(pl.program_id(1) == 0)
    def _():
        st_ref[...] = jnp.zeros_like(st_ref)

    c = GDN_CHUNK
    hd = GDN_HEAD_DIM
    nc = uwqf.shape[2] // c
    chains = [(d, h) for d in range(2) for h in range(GDN_HEADS)]
    refs = ((uwqf, ktf, inf, egf, of_ref), (uwqb, ktb, inb, egb, ob_ref))
    states = [st_ref[j] for j in range(len(chains))]
    for step in range(nc):
        sws = []
        for j, (d, h) in enumerate(chains):
            ci = step if d == 0 else nc - 1 - step
            rows = slice(ci * c, (ci + 1) * c)
            uwq = refs[d][0]
            wq = jnp.concatenate([uwq[0, h, rows, hd:2 * hd], uwq[0, h, rows, 2 * hd:3 * hd]], axis=0)
            sws.append(_dot(wq, states[j].astype(BF16)))
        for j, (d, h) in enumerate(chains):
            ci = step if d == 0 else nc - 1 - step
            rows = slice(ci * c, (ci + 1) * c)
            uwq, kt, intra, eg, o_ref = refs[d]
            sw = sws[j]
            v16 = (uwq[0, h, rows, 0:hd].astype(F32) - sw[:c]).astype(BF16)
            out = sw[c:] + _dot(intra[0, h, rows, d * c:(d + 1) * c], v16)
            o_ref[0, rows, h * hd:(h + 1) * hd] = out.astype(o_ref.dtype)
            lane = d * GDN_HEADS + h
            states[j] = states[j] * eg[0, ci, lane:lane + 1, :] + _dot(kt[0, h, :, rows], v16)
    for j in range(len(chains)):
        st_ref[j] = states[j]


def gdn_state_scan(uwq_f, uwq_b, kt_f, kt_b, intra, eg):
    nb, nh, seq, _ = uwq_f.shape
    hd = GDN_HEAD_DIM
    nc = GDN_STATE_CHUNKS
    r = nc * GDN_CHUNK
    n = seq // r

    def specs(pos):
        return [pl.BlockSpec((1, nh, r, 3 * hd), lambda b, i: (b, 0, pos(i), 0)),
                pl.BlockSpec((1, nh, hd, r), lambda b, i: (b, 0, 0, pos(i))),
                pl.BlockSpec((1, nh, r, 2 * GDN_CHUNK), lambda b, i: (b, 0, pos(i), 0)),
                pl.BlockSpec((1, nc, SUBLANES, LANES), lambda b, i: (b, pos(i), 0, 0))]

    fwd = lambda i: i
    bwd = lambda i: n - 1 - i
    out_sd = jax.ShapeDtypeStruct((nb, seq, GDN_WIDTH), BF16)
    return pl.pallas_call(
        _gdn_state_kernel,
        grid=(nb, n),
        in_specs=specs(fwd) + specs(bwd),
        out_specs=[pl.BlockSpec((1, r, GDN_WIDTH), lambda b, i: (b, i, 0)),
                   pl.BlockSpec((1, r, GDN_WIDTH), lambda b, i: (b, n - 1 - i, 0))],
        out_shape=[out_sd, out_sd],
        scratch_shapes=[pltpu.VMEM((2 * nh, hd, hd), F32)],
        compiler_params=_cparams("parallel", "arbitrary"),
        name="gdn_state_scan",
    )(uwq_f, kt_f, intra, eg, uwq_b, kt_b, intra, eg)


def _out_proj_kernel(x_ref, mod_ref, ys5_ref, cx_ref, cb_ref, cc_ref, cxp_ref, cxn_ref, ccp_ref, ccn_ref,
                     cw_ref, of_ref, ob_ref, z_ref, gnw_ref, w_ref, o_ref, *, seq):
    tm = x_ref.shape[0]
    t0 = (pl.program_id(0) * tm) % seq
    f32 = lambda ref, rows=slice(None): ref[rows, :].astype(F32)
    last = slice(HALO_ROWS - 1, HALO_ROWS)
    prev_row = jnp.where(t0 == 0, 0.0, f32(ccp_ref, last) * f32(cxp_ref, last))
    next_row = jnp.where(t0 + tm == seq, 0.0, f32(ccn_ref, slice(0, 1)) * f32(cxn_ref, slice(0, 1)))
    y_conv = f32(cb_ref) * _conv3_rows(f32(cc_ref) * f32(cx_ref), prev_row, next_row, cw_ref)
    o = f32(of_ref) + f32(ob_ref)
    z = f32(z_ref)
    heads = []
    for h in range(GDN_HEADS):
        cols = slice(h * GDN_HEAD_DIM, (h + 1) * GDN_HEAD_DIM)
        heads.append(_rms(o[:, cols]) * gnw_ref[...] * _silu(z[:, cols]))
    y_gdn = jnp.concatenate(heads, axis=1)
    c0, c1 = S5_WIDTH, S5_WIDTH + CONV_WIDTH
    mix = (_dot(ys5_ref[...].astype(BF16), w_ref[0:c0, :])
           + _dot(y_conv.astype(BF16), w_ref[c0:c1, :])
           + _dot(y_gdn.astype(BF16), w_ref[c1:, :]))
    o_ref[...] = x_ref[...] + mod_ref[0, 2:3, :] * mix


def out_projection(x2d, mod, y_s5, proj, conv_w, o_f, o_b, gdn_norm_w, w_out, seq):
    m, d = x2d.shape
    tm = ROW_TILE
    per_seq = seq // tm
    cw = CONV_WIDTH
    row = lambda c: pl.BlockSpec((tm, cw), lambda i: (i, c))
    cxp, cxn = _halo_specs(tm, cw, lambda: 1, m)
    ccp, ccn = _halo_specs(tm, cw, lambda: 3, m)
    gw = GDN_WIDTH
    return pl.pallas_call(
        functools.partial(_out_proj_kernel, seq=seq),
        grid=(m // tm,),
        in_specs=[pl.BlockSpec((tm, d), lambda i: (i, 0)),
                  pl.BlockSpec((1, 6, d), lambda i: (i // per_seq, 0, 0)),
                  pl.BlockSpec((tm, S5_WIDTH), lambda i: (i % per_seq, i // per_seq)),
                  row(1), row(2), row(3), cxp, cxn, ccp, ccn,
                  pl.BlockSpec((3, cw), lambda i: (0, 0)),
                  pl.BlockSpec((tm, gw), lambda i: (i, 0)),
                  pl.BlockSpec((tm, gw), lambda i: (i, 0)),
                  pl.BlockSpec((tm, gw), lambda i: (i, (PROJ_MAIN - gw) // gw)),
                  pl.BlockSpec((1, GDN_HEAD_DIM), lambda i: (0, 0)),
                  pl.BlockSpec((d, d), lambda i: (0, 0))],
        out_specs=pl.BlockSpec((tm, d), lambda i: (i, 0)),
        out_shape=jax.ShapeDtypeStruct((m, d), F32),
        compiler_params=_cparams("parallel"),
        name="out_projection",
    )(x2d, mod, y_s5, proj, proj, proj, proj, proj, proj, proj, conv_w, o_f, o_b, proj, gdn_norm_w, w_out)


def _first_argmax(vals, rows, n):
    m = jnp.max(vals, axis=0, keepdims=True)
    return m, jnp.min(jnp.where(vals == m, rows, n), axis=0, keepdims=True)


def _route_kernel(x_ref, mod_ref, nw_ref, wr_ref, br_ref, h_ref, idx_ref, wgt_ref, cnt_ref, carry_ref, before_ref):
    step = pl.program_id(0)
    tm = x_ref.shape[0]

    @pl.when(step == 0)
    def _():
        carry_ref[...] = jnp.zeros_like(carry_ref)
        earlier = lax.broadcasted_iota(jnp.int32, (tm, tm), 0) < lax.broadcasted_iota(jnp.int32, (tm, tm), 1)
        before_ref[...] = jnp.where(earlier, 1.0, 0.0).astype(BF16)

    h = _rms(x_ref[...]) * nw_ref[...] * (1.0 + mod_ref[0, 4:5, :]) + mod_ref[0, 3:4, :]
    h_ref[...] = h.reshape(h_ref.shape)
    nr = br_ref.shape[0]
    h_hi = h.astype(BF16)
    h_lo = (h - h_hi.astype(F32)).astype(BF16)
    head = _dot_nt(wr_ref[...], h_hi)
    logits = head[:nr] + head[nr:] + _dot_nt(wr_ref[0:nr, :], h_lo) + br_ref[...]
    lg = logits[N_EXPERTS:N_EXPERTS + MOE_GROUPS, :]
    rows_g = lax.broadcasted_iota(jnp.int32, lg.shape, 0)
    g_max, g_idx = _first_argmax(lg, rows_g, MOE_GROUPS)
    g_w = 1.0 / jnp.sum(jnp.exp(lg - g_max), axis=0, keepdims=True)
    le = jnp.zeros((EXPERTS_PER_GROUP, tm), F32)
    for g in range(MOE_GROUPS):
        le = jnp.where(g_idx == g, logits[g * EXPERTS_PER_GROUP:(g + 1) * EXPERTS_PER_GROUP, :], le)
    rows_e = lax.broadcasted_iota(jnp.int32, le.shape, 0)
    m1, i1 = _first_argmax(le, rows_e, EXPERTS_PER_GROUP)
    m2, i2 = _first_argmax(jnp.where(rows_e == i1, -jnp.inf, le), rows_e, EXPERTS_PER_GROUP)
    r = jnp.exp(m2 - m1)
    w1 = g_w / (1.0 + r)
    e1 = g_idx * EXPERTS_PER_GROUP + i1
    e2 = g_idx * EXPERTS_PER_GROUP + i2
    rows_x = lax.broadcasted_iota(jnp.int32, (N_EXPERTS, tm), 0)
    hit1 = rows_x == e1
    hit2 = rows_x == e2
    onehot = jnp.where(hit1 | hit2, 1.0, 0.0)
    ahead = _dot(onehot.astype(BF16), before_ref[...]) + carry_ref[:, 0:1]
    rank1 = jnp.sum(jnp.where(hit1, ahead, 0.0), axis=0, keepdims=True)
    rank2 = jnp.sum(jnp.where(hit2, ahead, 0.0), axis=0, keepdims=True)
    idx_ref[0, 0:1, :] = e1
    idx_ref[0, 1:2, :] = e2
    idx_ref[0, 2:3, :] = rank1.astype(jnp.int32)
    idx_ref[0, 3:4, :] = rank2.astype(jnp.int32)
    wgt_ref[0:1, :] = w1
    wgt_ref[1:2, :] = w1 * r
    carry_ref[...] = carry_ref[...] + jnp.sum(onehot, axis=1, keepdims=True)
    cnt_ref[...] = carry_ref[...]


def moe_route(x2d, mod, norm_w, w_router_t, b_router, seq):
    m, d = x2d.shape
    tm = ROUTE_TILE
    per_seq = seq // tm
    nr = w_router_t.shape[0]
    return pl.pallas_call(
        _route_kernel,
        grid=(m // tm,),
        in_specs=[pl.BlockSpec((tm, d), lambda i: (i, 0)),
                  pl.BlockSpec((1, 6, d), lambda i: (i // per_seq, 0, 0)),
                  pl.BlockSpec((1, d), lambda i: (0, 0)),
                  pl.BlockSpec((nr, d), lambda i: (0, 0)),
                  pl.BlockSpec((nr // 2, 1), lambda i: (0, 0))],
        out_specs=[pl.BlockSpec((tm,) + SLAB, lambda i: (i, 0, 0)),
                   pl.BlockSpec((1, 4, tm), lambda i: (i, 0, 0)),
                   pl.BlockSpec((2, tm), lambda i: (0, i)),
                   pl.BlockSpec((N_EXPERTS, LANES), lambda i: (0, 0))],
        out_shape=[jax.ShapeDtypeStruct((m,) + SLAB, F32),
                   jax.ShapeDtypeStruct((m // tm, 4, tm), jnp.int32),
                   jax.ShapeDtypeStruct((2, m), F32),
                   jax.ShapeDtypeStruct((N_EXPERTS, LANES), F32)],
        scratch_shapes=[pltpu.VMEM((N_EXPERTS, LANES), F32), pltpu.VMEM((tm, tm), BF16)],
        compiler_params=_cparams("arbitrary"),
        name="moe_route",
    )(x2d, mod, norm_w, w_router_t, b_router)


def _row_copy(src_ref, src_row, dst_ref, dst_row, sem):
    return pltpu.make_async_copy(src_ref.at[pl.ds(src_row, 1)], dst_ref.at[pl.ds(dst_row, 1)], sem)


def _scatter_kernel(last_ref, slot_ref, h_ref, xs_ref, zero_ref, sem):
    tm = h_ref.shape[0]

    @pl.when(pl.program_id(0) == 0)
    def _():
        zero_ref[...] = jnp.zeros_like(zero_ref)

        def fill(e, c):
            @pl.when(last_ref[e] >= 0)
            def _():
                pltpu.make_async_copy(zero_ref, xs_ref.at[pl.ds(last_ref[e], MOE_BLOCK)], sem).start()
            return c

        def drain(e, c):
            @pl.when(last_ref[e] >= 0)
            def _():
                pltpu.make_async_copy(zero_ref, xs_ref.at[pl.ds(0, MOE_BLOCK)], sem).wait()
            return c

        lax.fori_loop(0, N_EXPERTS, fill, 0)
        lax.fori_loop(0, N_EXPERTS, drain, 0)

        n_blocks = xs_ref.shape[0] // MOE_BLOCK

        def fill_unused(b, c):
            pltpu.make_async_copy(zero_ref, xs_ref.at[pl.ds(b * MOE_BLOCK, MOE_BLOCK)], sem).start()
            return c

        def drain_unused(b, c):
            pltpu.make_async_copy(zero_ref, xs_ref.at[pl.ds(0, MOE_BLOCK)], sem).wait()
            return c

        lax.fori_loop(last_ref[N_EXPERTS], n_blocks, fill_unused, 0)
        lax.fori_loop(last_ref[N_EXPERTS], n_blocks, drain_unused, 0)

    def start(r, c):
        _row_copy(h_ref, r, xs_ref, slot_ref[0, 0, r], sem).start(priority=0)
        _row_copy(h_ref, r, xs_ref, slot_ref[0, 1, r], sem).start(priority=1)
        return c

    def wait(r, c):
        _row_copy(h_ref, 0, xs_ref, 0, sem).wait()
        _row_copy(h_ref, 0, xs_ref, 0, sem).wait()
        return c

    lax.fori_loop(0, tm, start, 0, unroll=DMA_UNROLL)
    lax.fori_loop(0, tm, wait, 0, unroll=DMA_UNROLL)


def moe_scatter(h_slab, slots, last_block_start, n_slots):
    m = h_slab.shape[0]
    tm = ROUTE_TILE
    return pl.pallas_call(
        _scatter_kernel,
        grid_spec=pltpu.PrefetchScalarGridSpec(
            num_scalar_prefetch=1,
            grid=(m // tm,),
            in_specs=[pl.BlockSpec((1, 2, tm), lambda i, last: (i, 0, 0), memory_space=pltpu.SMEM),
                      pl.BlockSpec((tm,) + SLAB, lambda i, last: (i, 0, 0))],
            out_specs=pl.BlockSpec(memory_space=pl.ANY),
            scratch_shapes=[pltpu.VMEM((MOE_BLOCK,) + SLAB, F32), pltpu.SemaphoreType.DMA(())]),
        out_shape=jax.ShapeDtypeStruct((n_slots,) + SLAB, F32),
        compiler_params=_cparams("arbitrary"),
        name="moe_scatter",
    )(last_block_start, slots, h_slab)


def _expert_kernel(be_ref, nused_ref, xs_ref, wgu_ref, wd_ref, y_ref, wgu16_ref, wd16_ref):
    i = pl.program_id(0)
    changed = jnp.logical_or(i == 0, be_ref[i] != be_ref[jnp.maximum(i - 1, 0)])

    @pl.when(changed)
    def _():
        wgu16_ref[...] = wgu_ref[0, 0].astype(BF16)
        wd16_ref[...] = wd_ref[0, 0].astype(BF16)

    @pl.when(i < nused_ref[0])
    def _():
        x = xs_ref[...].reshape(MOE_BLOCK, D_MODEL).astype(BF16)
        gu = _dot(x, wgu16_ref[...])
        act = _silu(gu[:, :EXPERT_FF]) * gu[:, EXPERT_FF:]
        y = _dot(act.astype(BF16), wd16_ref[...])
        y_ref[...] = y.reshape(y_ref.shape)

    @pl.when(i >= nused_ref[0])
    def _():
        y_ref[...] = jnp.zeros_like(y_ref)


def moe_experts(xs, blk_expert, n_used, w_gate_up, w_down, layer):
    n_slots = xs.shape[0]
    n_blocks = n_slots // MOE_BLOCK
    d = D_MODEL
    return pl.pallas_call(
        _expert_kernel,
        grid_spec=pltpu.PrefetchScalarGridSpec(
            num_scalar_prefetch=2,
            grid=(n_blocks,),
            in_specs=[pl.BlockSpec((MOE_BLOCK,) + SLAB, lambda i, be, nu: (jnp.minimum(i, nu[0] - 1), 0, 0)),
                      pl.BlockSpec((1, 1, d, 2 * EXPERT_FF), lambda i, be, nu: (layer, be[i], 0, 0)),
                      pl.BlockSpec((1, 1, EXPERT_FF, d), lambda i, be, nu: (layer, be[i], 0, 0))],
            out_specs=pl.BlockSpec((MOE_BLOCK,) + SLAB, lambda i, be, nu: (i, 0, 0)),
            scratch_shapes=[pltpu.VMEM((d, 2 * EXPERT_FF), BF16), pltpu.VMEM((EXPERT_FF, d), BF16)]),
        out_shape=jax.ShapeDtypeStruct((n_slots,) + SLAB, F32),
        compiler_params=_cparams("arbitrary"),
        name="moe_experts",
    )(blk_expert, n_used, xs, w_gate_up, w_down)


def _combine_kernel(slot_ref, next_slot_ref, x_ref, mod_ref, wgt_ref, fnw_ref, yb_ref, o_ref, y1_ref, y2_ref, sem, *,
                    final):
    tm = x_ref.shape[0]
    i = pl.program_id(0)
    cur = i % 2

    def gather(slots, buf):
        def start(r, c):
            _row_copy(yb_ref, slots[0, 0, r], y1_ref.at[buf], r, sem.at[buf]).start(priority=0)
            _row_copy(yb_ref, slots[0, 1, r], y2_ref.at[buf], r, sem.at[buf]).start(priority=1)
            return c
        lax.fori_loop(0, tm, start, 0, unroll=DMA_UNROLL)

    @pl.when(i == 0)
    def _():
        gather(slot_ref, 0)

    @pl.when(i + 1 < pl.num_programs(0))
    def _():
        gather(next_slot_ref, 1 - cur)

    def wait(r, c):
        _row_copy(yb_ref, 0, y1_ref.at[cur], 0, sem.at[cur]).wait()
        _row_copy(yb_ref, 0, y2_ref.at[cur], 0, sem.at[cur]).wait()
        return c

    lax.fori_loop(0, tm, wait, 0, unroll=DMA_UNROLL)
    w1 = wgt_ref[:, 0:1]
    w2 = wgt_ref[:, 1:2]
    y = y1_ref[cur].reshape(x_ref.shape) * w1 + y2_ref[cur].reshape(x_ref.shape) * w2
    o_ref[...] = x_ref[...] + mod_ref[0, 5:6, :] * y
    if final:
        o_ref[...] = _rms(o_ref[...]) * fnw_ref[...]


def moe_combine(x2d, mod, yb, slots, weights, final_norm_w, seq, final):
    m, d = x2d.shape
    tm = ROUTE_TILE
    per_seq = seq // tm
    n = m // tm
    slot_spec = lambda pos: pl.BlockSpec((1, 2, tm), lambda i: (pos(i), 0, 0), memory_space=pltpu.SMEM)
    return pl.pallas_call(
        functools.partial(_combine_kernel, final=final),
        grid=(n,),
        in_specs=[slot_spec(lambda i: i), slot_spec(lambda i: jnp.minimum(i + 1, n - 1)),
                  pl.BlockSpec((tm, d), lambda i: (i, 0)),
                  pl.BlockSpec((1, 6, d), lambda i: (i // per_seq, 0, 0)),
                  pl.BlockSpec((tm, 2), lambda i: (i, 0)),
                  pl.BlockSpec((1, d), lambda i: (0, 0)),
                  pl.BlockSpec(memory_space=pl.ANY)],
        out_specs=pl.BlockSpec((tm, d), lambda i: (i, 0)),
        out_shape=jax.ShapeDtypeStruct((m, d), F32),
        scratch_shapes=[pltpu.VMEM((2, tm) + SLAB, F32), pltpu.VMEM((2, tm) + SLAB, F32),
                        pltpu.SemaphoreType.DMA((2,))],
        compiler_params=_cparams("arbitrary"),
        name="moe_combine",
    )(slots, slots, x2d, mod, weights, final_norm_w, yb)


def hier_moe_layer(x2d, mod, norm_w, w_router_t, b_router, w_gate_up, w_down, layer, final_norm_w, seq, final):
    m = x2d.shape[0]
    h_slab, idx, wgt, counts = moe_route(x2d, mod, norm_w, w_router_t, b_router, seq)
    counts = counts[:, 0].astype(jnp.int32)
    padded = (counts + MOE_BLOCK - 1) // MOE_BLOCK * MOE_BLOCK
    pad_end = jnp.cumsum(padded)
    pad_start = (pad_end - padded).astype(jnp.int32)
    n_blocks = 2 * m // MOE_BLOCK + N_EXPERTS
    blk_start = jnp.arange(n_blocks, dtype=jnp.int32) * MOE_BLOCK
    blk_expert = jnp.minimum(jnp.sum((pad_end[None, :] <= blk_start[:, None]).astype(jnp.int32), axis=1),
                             N_EXPERTS - 1)
    n_used = (pad_end[-1:] // MOE_BLOCK).astype(jnp.int32)
    fill_plan = jnp.concatenate([jnp.where(padded > 0, pad_end - MOE_BLOCK, -1), n_used]).astype(jnp.int32)
    is_expert = idx[:, 0:2, :, None] == jnp.arange(N_EXPERTS, dtype=jnp.int32)
    slots = jnp.sum(jnp.where(is_expert, pad_start, 0), axis=-1) + idx[:, 2:4, :]
    xs = moe_scatter(h_slab, slots, fill_plan, n_blocks * MOE_BLOCK)
    yb = moe_experts(xs, blk_expert, n_used, w_gate_up, w_down, layer)
    return moe_combine(x2d, mod, yb, slots, wgt.T, final_norm_w, seq, final)


def _layer_params(l, p):
    w_in = p['w_in'][l]
    w_gate = w_in[:, PROJ_MAIN:].T
    w_router_t = jnp.zeros((ROUTER_ROWS, D_MODEL), F32)
    w_router_t = w_router_t.at[:N_EXPERTS].set(p['moe_w_expert'][l].T)
    w_router_t = w_router_t.at[N_EXPERTS:N_EXPERTS + MOE_GROUPS].set(p['moe_w_group'][l].T)
    w_router_hi = w_router_t.astype(BF16)
    w_router_t = jnp.concatenate([w_router_hi, (w_router_t - w_router_hi.astype(F32)).astype(BF16)], axis=0)
    b_router = jnp.zeros((ROUTER_ROWS, 1), F32)
    b_router = b_router.at[:N_EXPERTS, 0].set(p['moe_b_expert'][l])
    b_router = b_router.at[N_EXPERTS:N_EXPERTS + MOE_GROUPS, 0].set(p['moe_b_group'][l])
    s5 = [s5_discretise(p['s5_lam_re'][l, d], p['s5_lam_im'][l, d], p['s5_log_step'][l, d],
                        p['s5_b_re'][l, d], p['s5_b_im'][l, d], p['s5_c_re'][l, d], p['s5_c_im'][l, d])
          for d in range(2)]
    return dict(
        norm_mix=p['norm_mix'][l][None, :], norm_ffn=p['norm_ffn'][l][None, :],
        w_main=w_in[:, :PROJ_MAIN].astype(BF16), w_gate=w_gate.astype(BF16),
        w_out=p['w_out'][l].astype(BF16), s5=s5,
        s5_d=p['s5_d'][l][None, :], s5_w_glu=p['s5_w_glu'][l].astype(BF16), s5_b_glu=p['s5_b_glu'][l][None, :],
        conv_w=p['conv_w'][l], gdn_conv_w=p['gdn_conv_w'][l], gdn_a_log=p['gdn_a_log'][l],
        gdn_dt_bias=p['gdn_dt_bias'][l], gdn_norm_w=p['gdn_norm_w'][l][None, :],
        w_router_t=w_router_t, b_router=b_router,
        w_gate_up=p['moe_w_gate_up'], w_down=p['moe_w_down'])


def _trunk(x, mods, layers, final_norm_w):
    nb, seq, d = x.shape
    m = nb * seq
    x2d = x.reshape(m, d)
    for l, lp in enumerate(layers):
        mod = mods[l].reshape(nb, 6, d)
        proj, u_t, pre_t = in_projection(x2d, mod, lp['norm_mix'], lp['w_main'], lp['w_gate'], seq)
        (a_f, bm_f, cm_f), (a_b, bm_b, cm_b) = lp['s5']
        y_s5 = s5_mixer(u_t, a_f, bm_f, cm_f, a_b, bm_b, cm_b, lp['s5_d'], lp['s5_w_glu'], lp['s5_b_glu'])
        qkv = gdn_prepare(proj, lp['gdn_conv_w'], seq).reshape(nb, seq, 3 * GDN_WIDTH)
        gates = gdn_gates(pre_t, lp['gdn_a_log'], lp['gdn_dt_bias'])
        uwq_f, uwq_b, kt_f, kt_b, intra, eg = gdn_local(qkv, gates.transpose(0, 2, 1), gates)
        o_f, o_b = gdn_state_scan(uwq_f, uwq_b, kt_f, kt_b, intra, eg)
        x2d = out_projection(x2d, mod, y_s5, proj, lp['conv_w'], o_f.reshape(m, GDN_WIDTH),
                             o_b.reshape(m, GDN_WIDTH), lp['gdn_norm_w'], lp['w_out'], seq)
        x2d = hier_moe_layer(x2d, mod, lp['norm_ffn'], lp['w_router_t'], lp['b_router'],
                             lp['w_gate_up'], lp['w_down'], l, final_norm_w, seq, final=(l == len(layers) - 1))
    return x2d.reshape(nb, seq, d)


def kernel(x_prompt, x_sample, c_prompt, c_sample, w_ada, b_ada, norm_mix, norm_ffn, w_in, w_out, s5_lam_re, s5_lam_im, s5_log_step, s5_b_re, s5_b_im, s5_c_re, s5_c_im, s5_d, s5_w_glu, s5_b_glu, conv_w, gdn_conv_w, gdn_a_log, gdn_dt_bias, gdn_norm_w, moe_w_group, moe_b_group, moe_w_expert, moe_b_expert, moe_w_gate_up, moe_w_down, final_norm):
    p = dict(norm_mix=norm_mix, norm_ffn=norm_ffn, w_in=w_in, w_out=w_out, s5_lam_re=s5_lam_re,
             s5_lam_im=s5_lam_im, s5_log_step=s5_log_step, s5_b_re=s5_b_re, s5_b_im=s5_b_im, s5_c_re=s5_c_re,
             s5_c_im=s5_c_im, s5_d=s5_d, s5_w_glu=s5_w_glu, s5_b_glu=s5_b_glu, conv_w=conv_w,
             gdn_conv_w=gdn_conv_w, gdn_a_log=gdn_a_log, gdn_dt_bias=gdn_dt_bias, gdn_norm_w=gdn_norm_w,
             moe_w_group=moe_w_group, moe_b_group=moe_b_group, moe_w_expert=moe_w_expert,
             moe_b_expert=moe_b_expert, moe_w_gate_up=moe_w_gate_up, moe_w_down=moe_w_down)
    depth = w_ada.shape[0]
    layers = [_layer_params(l, p) for l in range(depth)]
    nbp = c_prompt.shape[0]
    mods = ada_modulation(jnp.concatenate([c_prompt, c_sample], axis=0), w_ada, b_ada)
    fnw = final_norm[None, :]
    y_prompt = _trunk(x_prompt, mods[:, :nbp], layers, fnw)
    y_sample = _trunk(x_sample, mods[:, nbp:], layers, fnw)
    return (y_prompt, y_sample)
```

```python
import functools

import jax
import jax.numpy as jnp
from jax import lax
from jax.experimental import pallas as pl
from jax.experimental.pallas import tpu as pltpu

F32 = jnp.float32
BF16 = jnp.bfloat16
HIGHEST = lax.Precision.HIGHEST

D_MODEL = 1024
DEPTH = 4
S5_WIDTH = 256
S5_GROUP = 16
S5_GROUPS = 16
S5_STATE = 64
S5_LANES = S5_GROUPS * S5_STATE
CONV_WIDTH = 256
GDN_WIDTH = 512
GDN_HEADS = 4
GDN_HEAD_DIM = 128
GDN_CHUNK = 64
PROJ_MAIN = S5_WIDTH + 3 * CONV_WIDTH + 4 * GDN_WIDTH
N_GATE_COLS = 4 * GDN_HEADS
MOE_GROUPS = 4
EXPERTS_PER_GROUP = 8
N_EXPERTS = 32
EXPERT_FF = 512
MOE_BLOCK = 512
EPS = 1e-6

SUBLANES = 8
LANES = 128
SLAB = (SUBLANES, LANES)
VMEM_LIMIT = 56 * 1024 * 1024

ROW_TILE = 512
S5_TIME_TILE = 64
GDN_STATE_CHUNKS = 8
GDN_LOCAL_PAIRS = 2
ROUTE_TILE = 512
ROUTER_ROWS = N_EXPERTS + 16
DMA_UNROLL = 8


def _cparams(*sem):
    return pltpu.CompilerParams(dimension_semantics=sem, vmem_limit_bytes=VMEM_LIMIT)


def _dot(a, b, **kw):
    return jnp.dot(a, b, preferred_element_type=F32, **kw)


def _dot_nt(a, b, **kw):
    return lax.dot_general(a, b, (((1,), (1,)), ((), ())), preferred_element_type=F32, **kw)


def _dot_tn(a, b, **kw):
    return lax.dot_general(a, b, (((0,), (0,)), ((), ())), preferred_element_type=F32, **kw)


def _sigmoid(x):
    return 0.5 * jnp.tanh(0.5 * x) + 0.5


def _silu(x):
    return x * _sigmoid(x)


def _gelu_tanh(x):
    return 0.5 * x * (1.0 + jnp.tanh(0.7978845608028654 * (x + 0.044715 * (x * x * x))))


def _softplus(x):
    return jnp.maximum(x, 0.0) + jnp.log(1.0 + jnp.exp(-jnp.abs(x)))


def _rms(x):
    return x * lax.rsqrt(jnp.mean(x * x, axis=-1, keepdims=True) + EPS)


def _ada_kernel(c_ref, w_ref, b_ref, o_ref):
    o_ref[0] = _dot(_silu(c_ref[...]), w_ref[0], precision=HIGHEST) + b_ref[0]


def ada_modulation(c_all, w_ada, b_ada):
    nb = c_all.shape[0]
    depth, d, n = w_ada.shape
    tn = 1536
    return pl.pallas_call(
        _ada_kernel,
        grid=(depth, n // tn),
        in_specs=[pl.BlockSpec((nb, d), lambda l, j: (0, 0)),
                  pl.BlockSpec((1, d, tn), lambda l, j: (l, 0, j)),
                  pl.BlockSpec((1, 1, tn), lambda l, j: (l, 0, j))],
        out_specs=pl.BlockSpec((1, nb, tn), lambda l, j: (l, 0, j)),
        out_shape=jax.ShapeDtypeStruct((depth, nb, n), F32),
        compiler_params=_cparams("parallel", "parallel"),
        name="ada_modulation",
    )(c_all, w_ada, b_ada.reshape(depth, 1, n))


def _in_proj_kernel(x_ref, mod_ref, nw_ref, w_ref, wg_ref, proj_ref, u_ref, gate_ref):
    h = _rms(x_ref[...]) * nw_ref[...] * (1.0 + mod_ref[0, 1:2, :]) + mod_ref[0, 0:1, :]
    hb = h.astype(BF16)
    proj = _dot(hb, w_ref[...]).astype(proj_ref.dtype)
    proj_ref[...] = proj
    u_ref[...] = proj[:, :S5_WIDTH]
    gate_ref[0] = _dot_nt(wg_ref[...], hb)


def in_projection(x2d, mod, norm_w, w_main, w_gate, seq):
    m, d = x2d.shape
    tm = ROW_TILE
    per_seq = seq // tm
    nb = m // seq
    return pl.pallas_call(
        _in_proj_kernel,
        grid=(m // tm,),
        in_specs=[pl.BlockSpec((tm, d), lambda i: (i, 0)),
                  pl.BlockSpec((1, 6, d), lambda i: (i // per_seq, 0, 0)),
                  pl.BlockSpec((1, d), lambda i: (0, 0)),
                  pl.BlockSpec((d, PROJ_MAIN), lambda i: (0, 0)),
                  pl.BlockSpec((N_GATE_COLS, d), lambda i: (0, 0))],
        out_specs=[pl.BlockSpec((tm, PROJ_MAIN), lambda i: (i, 0)),
                   pl.BlockSpec((tm, S5_WIDTH), lambda i: (i % per_seq, i // per_seq)),
                   pl.BlockSpec((1, N_GATE_COLS, tm), lambda i: (i // per_seq, 0, i % per_seq))],
        out_shape=[jax.ShapeDtypeStruct((m, PROJ_MAIN), BF16),
                   jax.ShapeDtypeStruct((seq, nb * S5_WIDTH), BF16),
                   jax.ShapeDtypeStruct((nb, N_GATE_COLS, seq), F32)],
        compiler_params=_cparams("parallel"),
        name="in_projection",
    )(x2d, mod, norm_w, w_main, w_gate)


def _s5_scan(x_ref, h0_re, h0_im, a_re, a_im, steps, reverse):
    def body(s, carry):
        hr, hi = carry
        t = (steps - 1 - s) if reverse else s
        rows = pl.ds(pl.multiple_of(t * SUBLANES, SUBLANES), SUBLANES)
        xr = x_ref[rows, 0:S5_LANES]
        xi = x_ref[rows, S5_LANES:2 * S5_LANES]
        nr = a_re * hr - a_im * hi + xr
        ni = a_re * hi + a_im * hr + xi
        x_ref[rows, 0:S5_LANES] = nr
        x_ref[rows, S5_LANES:2 * S5_LANES] = ni
        return nr, ni
    return lax.fori_loop(0, steps, body, (h0_re, h0_im), unroll=4)


def _s5_direction(u_ref, a_ref, bmat_ref, cmat_ref, x_ref, st_ref, reverse):
    tc = u_ref.shape[0]
    w = S5_WIDTH
    nb = u_ref.shape[1] // w
    first = pl.program_id(0) == 0

    @pl.when(first)
    def _():
        st_ref[...] = jnp.zeros_like(st_ref)

    u = u_ref[...].astype(F32).reshape(tc * nb, w)
    x_ref[...] = _dot(u.astype(BF16), bmat_ref[...])
    a_re = jnp.broadcast_to(a_ref[0:1, :], (nb, S5_LANES))
    a_im = jnp.broadcast_to(a_ref[1:2, :], (nb, S5_LANES))
    hr, hi = _s5_scan(x_ref, st_ref[0], st_ref[1], a_re, a_im, tc, reverse)
    st_ref[0] = hr
    st_ref[1] = hi
    return u, _dot(x_ref[...].astype(BF16), cmat_ref[...])


def _s5_fwd_kernel(u_ref, a_ref, bmat_ref, cmat_ref, y_ref, x_ref, st_ref):
    _, y = _s5_direction(u_ref, a_ref, bmat_ref, cmat_ref, x_ref, st_ref, reverse=False)
    y_ref[...] = y.reshape(y_ref.shape)


def _s5_bwd_kernel(u_ref, yf_ref, a_ref, bmat_ref, cmat_ref, d_ref, wglu_ref, bglu_ref, o_ref, x_ref, st_ref):
    u, y = _s5_direction(u_ref, a_ref, bmat_ref, cmat_ref, x_ref, st_ref, reverse=True)
    y = y + u * d_ref[...] + yf_ref[...].reshape(u.shape)
    y = _gelu_tanh(y)
    out = y * _sigmoid(_dot(y.astype(BF16), wglu_ref[...]) + bglu_ref[...])
    o_ref[...] = out.reshape(o_ref.shape).astype(o_ref.dtype)


def s5_mixer(u_t, a_f, bmat_f, cmat_f, a_b, bmat_b, cmat_b, d_skip, w_glu, b_glu):
    seq = u_t.shape[0]
    w = S5_WIDTH
    nb = u_t.shape[1] // w
    assert nb == SUBLANES, "the scan keeps one batch row per sublane"
    tc = S5_TIME_TILE
    n = seq // tc
    rows = tc * nb
    const2 = lambda i: (0, 0)
    par_specs = [pl.BlockSpec((2, S5_LANES), const2),
                 pl.BlockSpec((w, 2 * S5_LANES), const2),
                 pl.BlockSpec((2 * S5_LANES, w), const2)]
    scratch = [pltpu.VMEM((rows, 2 * S5_LANES), F32), pltpu.VMEM((2, nb, S5_LANES), F32)]
    y_f = pl.pallas_call(
        _s5_fwd_kernel,
        grid=(n,),
        in_specs=[pl.BlockSpec((tc, nb * w), lambda i: (i, 0))] + par_specs,
        out_specs=pl.BlockSpec((tc, nb, w), lambda i: (i, 0, 0)),
        out_shape=jax.ShapeDtypeStruct((seq, nb, w), F32),
        scratch_shapes=scratch,
        compiler_params=_cparams("arbitrary"),
        name="s5_forward",
    )(u_t, a_f, bmat_f, cmat_f)
    rev = lambda i: (n - 1 - i, 0, 0)
    rev2 = lambda i: (n - 1 - i, 0)
    return pl.pallas_call(
        _s5_bwd_kernel,
        grid=(n,),
        in_specs=[pl.BlockSpec((tc, nb * w), rev2), pl.BlockSpec((tc, nb, w), rev)] + par_specs
                 + [pl.BlockSpec((1, w), const2), pl.BlockSpec((w, w), const2), pl.BlockSpec((1, w), const2)],
        out_specs=pl.BlockSpec((tc, nb * w), rev2),
        out_shape=jax.ShapeDtypeStruct((seq, nb * w), BF16),
        scratch_shapes=scratch,
        compiler_params=_cparams("arbitrary"),
        name="s5_backward_glu",
    )(u_t, y_f, a_b, bmat_b, cmat_b, d_skip, w_glu, b_glu)


def s5_discretise(lam_re, lam_im, log_step, b_re, b_im, c_re, c_im):
    g, p, h = S5_GROUPS, S5_STATE, S5_GROUP
    lr, li = lam_re.astype(F32), lam_im.astype(F32)
    dt = jnp.exp(log_step.astype(F32))[:, None]
    mag = jnp.exp(lr * dt)
    abr, abi = mag * jnp.cos(li * dt), mag * jnp.sin(li * dt)
    den = lr * lr + li * li
    cr = ((abr - 1.0) * lr + abi * li) / den
    ci = (abi * lr - (abr - 1.0) * li) / den
    bbr = cr[..., None] * b_re - ci[..., None] * b_im
    bbi = cr[..., None] * b_im + ci[..., None] * b_re
    eye = jnp.eye(g, dtype=F32)
    to_b = lambda t: jnp.einsum('gph,gk->ghkp', t, eye).reshape(g * h, g * p)
    bmat = jnp.concatenate([to_b(bbr), to_b(bbi)], axis=1)
    to_c = lambda t: jnp.einsum('ghp,gk->gpkh', t, eye).reshape(g * p, g * h)
    cmat = jnp.concatenate([to_c(c_re.astype(F32)), -to_c(c_im.astype(F32))], axis=0)
    a = jnp.stack([abr.reshape(g * p), abi.reshape(g * p)])
    return a, bmat.astype(BF16), cmat.astype(BF16)


def _conv3_rows(x, prev_row, next_row, w_ref):
    tm = x.shape[0]
    rows = lax.broadcasted_iota(jnp.int32, x.shape, 0)
    xp = jnp.where(rows == 0, prev_row, pltpu.roll(x, 1, 0))
    xn = jnp.where(rows == tm - 1, next_row, pltpu.roll(x, tm - 1, 0))
    return xp * w_ref[0:1, :] + x * w_ref[1:2, :] + xn * w_ref[2:3, :]


HALO_ROWS = 16
GDN_PREP_ROWS = 64


def _halo_specs(tm, width, col_block, n_rows):
    per = tm // HALO_ROWS
    last = n_rows // HALO_ROWS - 1
    prev = pl.BlockSpec((HALO_ROWS, width), lambda i, *_: (jnp.maximum(i * per - 1, 0), col_block(*_)))
    nxt = pl.BlockSpec((HALO_ROWS, width), lambda i, *_: (jnp.minimum((i + 1) * per, last), col_block(*_)))
    return prev, nxt


def _gdn_prep_kernel(x_ref, xp_ref, xn_ref, w_ref, o_ref, *, seq):
    tm = x_ref.shape[0]
    j = pl.program_id(1)
    t0 = (pl.program_id(0) * tm) % seq
    first_tile = t0 == 0
    last_tile = t0 + tm == seq
    normalise = j < 2
    scale = jnp.where(j == 0, GDN_HEAD_DIM ** -0.5, 1.0)
    rc = GDN_PREP_ROWS
    row = lambda ref, r, cols: ref[r:r + 1, cols].astype(F32)
    for h in range(GDN_HEADS):
        cols = slice(h * GDN_HEAD_DIM, (h + 1) * GDN_HEAD_DIM)
        w = w_ref[:, cols]
        for r0 in range(0, tm, rc):
            prev_row = (jnp.where(first_tile, 0.0, row(xp_ref, HALO_ROWS - 1, cols)) if r0 == 0
                        else row(x_ref, r0 - 1, cols))
            next_row = (jnp.where(last_tile, 0.0, row(xn_ref, 0, cols)) if r0 + rc == tm
                        else row(x_ref, r0 + rc, cols))
            y = _silu(_conv3_rows(x_ref[r0:r0 + rc, cols].astype(F32), prev_row, next_row, w))
            yn = y * (lax.rsqrt(jnp.sum(y * y, axis=-1, keepdims=True) + EPS) * scale)
            o_ref[r0:r0 + rc, cols] = jnp.where(normalise, yn, y).astype(o_ref.dtype)


def gdn_prepare(proj, conv_w, seq):
    m = proj.shape[0]
    tm = ROW_TILE
    first = (S5_WIDTH + 3 * CONV_WIDTH) // GDN_WIDTH
    prev, nxt = _halo_specs(tm, GDN_WIDTH, lambda j: first + j, m)
    return pl.pallas_call(
        functools.partial(_gdn_prep_kernel, seq=seq),
        grid=(m // tm, 3),
        in_specs=[pl.BlockSpec((tm, GDN_WIDTH), lambda i, j: (i, first + j)), prev, nxt,
                  pl.BlockSpec((3, GDN_WIDTH), lambda i, j: (0, j))],
        out_specs=pl.BlockSpec((tm, GDN_WIDTH), lambda i, j: (i, j)),
        out_shape=jax.ShapeDtypeStruct((m, 3 * GDN_WIDTH), BF16),
        compiler_params=_cparams("parallel", "parallel"),
        name="gdn_prepare",
    )(proj, proj, proj, conv_w)


def _gdn_gates_kernel(pre_ref, alog_ref, dtb_ref, o_ref):
    seq = pre_ref.shape[2]
    nh2 = 2 * GDN_HEADS
    g = -jnp.exp(alog_ref[...]) * _softplus(pre_ref[0, 0:nh2, :] + dtb_ref[...])
    o_ref[0, nh2:2 * nh2, :] = _sigmoid(pre_ref[0, nh2:2 * nh2, :])
    s = lax.broadcasted_iota(jnp.int32, (LANES, LANES), 0)
    t = lax.broadcasted_iota(jnp.int32, (LANES, LANES), 1)
    same = (s // GDN_CHUNK) == (t // GDN_CHUNK)
    m_f = jnp.where(same & (s <= t), 1.0, 0.0)
    m_b = jnp.where(same & (s >= t), 1.0, 0.0)
    for jb in range(seq // LANES):
        cols = slice(jb * LANES, (jb + 1) * LANES)
        gb = g[:, cols]
        o_ref[0, 0:GDN_HEADS, cols] = _dot(gb, m_f, precision=HIGHEST)[0:GDN_HEADS]
        o_ref[0, GDN_HEADS:nh2, cols] = _dot(gb, m_b, precision=HIGHEST)[GDN_HEADS:nh2]


def gdn_gates(pre_t, a_log, dt_bias):
    nb, rows, seq = pre_t.shape
    col = lambda t: t.astype(F32).reshape(2 * GDN_HEADS, 1)
    return pl.pallas_call(
        _gdn_gates_kernel,
        grid=(nb,),
        in_specs=[pl.BlockSpec((1, rows, seq), lambda b: (b, 0, 0)),
                  pl.BlockSpec((2 * GDN_HEADS, 1), lambda b: (0, 0)),
                  pl.BlockSpec((2 * GDN_HEADS, 1), lambda b: (0, 0))],
        out_specs=pl.BlockSpec((1, rows, seq), lambda b: (b, 0, 0)),
        out_shape=jax.ShapeDtypeStruct((nb, rows, seq), F32),
        compiler_params=_cparams("parallel"),
        name="gdn_gates",
    )(pre_t, col(a_log), col(dt_bias))


def _nilpotent_inverses(mats, nilpotency):
    n = mats[0].shape[0]
    eye = jnp.where(lax.broadcasted_iota(jnp.int32, (n, n), 0) == lax.broadcasted_iota(jnp.int32, (n, n), 1), 1.0, 0.0)
    ps = [eye - a for a in mats]
    a16 = [a.astype(BF16) for a in mats]
    sqs = [_dot(a, a) for a in a16]
    k = 2
    while True:
        sq16 = [s.astype(BF16) for s in sqs]
        ps = [p + _dot(p.astype(BF16), s) for p, s in zip(ps, sq16)]
        k *= 2
        if k >= nilpotency:
            return ps
        sqs = [_dot(s, s) for s in sq16]


def _gdn_local_kernel(q_ref, k_ref, v_ref, gcol_ref, grow_ref, uwqf_ref, uwqb_ref, ktf_ref, ktb_ref, in_ref, eg_ref):
    c = GDN_CHUNK
    r = 2 * c
    pairs = q_ref.shape[1] // r
    hd = GDN_HEAD_DIM
    ii = lax.broadcasted_iota(jnp.int32, (r, r), 0)
    jj = lax.broadcasted_iota(jnp.int32, (r, r), 1)
    same = (ii >= c) == (jj >= c)
    incl = (same & (ii >= jj), same & (ii <= jj))
    strict = (same & (ii > jj), same & (ii < jj))
    first_chunk = lax.broadcasted_iota(jnp.int32, (r, 1), 0) < c
    zeros = jnp.zeros((r, r), F32)
    big_a, rhs, keep = [], [], []
    units = [(p, h) for p in range(pairs) for h in range(GDN_HEADS)]
    for p, h in units:
        rows = slice(p * r, (p + 1) * r)
        cols = slice(h * hd, (h + 1) * hd)
        q16, k16 = q_ref[0, rows, cols], k_ref[0, rows, cols]
        q, k, v = q16.astype(F32), k16.astype(F32), v_ref[0, rows, cols].astype(F32)
        kk = _dot_nt(k16, k16)
        qk = _dot_nt(q16, k16)
        a_dir, rhs_dir = [], []
        for d in range(2):
            lane = d * GDN_HEADS + h
            gc_col = gcol_ref[0, rows, lane:lane + 1]
            gc_row = grow_ref[0, lane:lane + 1, rows]
            beta = gcol_ref[0, rows, 2 * GDN_HEADS + lane:2 * GDN_HEADS + lane + 1]
            decay = jnp.exp(jnp.where(incl[d], gc_col - gc_row, -1e30))
            a_dir.append(jnp.where(strict[d], kk * beta * decay, 0.0))
            intra = jnp.where(incl[d], qk * decay, 0.0)
            ends = (gc_col[c - 1:c], gc_col[2 * c - 1:2 * c]) if d == 0 else (gc_col[0:1], gc_col[c:c + 1])
            g_last = jnp.where(first_chunk, ends[0], ends[1])
            eg = jnp.exp(gc_col)
            kb = k * beta
            rhs_dir.append(jnp.concatenate([v * beta, kb * eg], axis=1))
            keep.append((q * eg, (k * jnp.exp(g_last - gc_col)).T, intra[:, :c] + intra[:, c:]))
            for ch in range(2):
                eg_ref[0, 2 * p + ch, lane:lane + 1, :] = jnp.broadcast_to(jnp.exp(ends[ch]), (1, LANES))
        big_a.append(jnp.concatenate([jnp.concatenate([a_dir[0], zeros], axis=1),
                                      jnp.concatenate([zeros, a_dir[1]], axis=1)], axis=0))
        rhs.append(jnp.concatenate(rhs_dir, axis=0))
    t_inv = _nilpotent_inverses(big_a, c)
    for u, (p, h) in enumerate(units):
        rows = slice(p * r, (p + 1) * r)
        uw = _dot(t_inv[u].astype(BF16), rhs[u].astype(BF16))
        (qd_f, kt_f, in_f), (qd_b, kt_b, in_b) = keep[2 * u], keep[2 * u + 1]
        uwqf_ref[0, h, rows, :] = jnp.concatenate([uw[:r], qd_f], axis=1).astype(BF16)
        uwqb_ref[0, h, rows, :] = jnp.concatenate([uw[r:], qd_b], axis=1).astype(BF16)
        ktf_ref[0, h, :, rows] = kt_f.astype(BF16)
        ktb_ref[0, h, :, rows] = kt_b.astype(BF16)
        in_ref[0, h, rows, :] = jnp.concatenate([in_f, in_b], axis=1).astype(BF16)


def gdn_local(qkv, gates_col, gates_row):
    nb, seq, _ = qkv.shape
    r = 2 * GDN_CHUNK * GDN_LOCAL_PAIRS
    hd, nh = GDN_HEAD_DIM, GDN_HEADS
    uwq_sd = jax.ShapeDtypeStruct((nb, nh, seq, 3 * hd), BF16)
    kt_sd = jax.ShapeDtypeStruct((nb, nh, hd, seq), BF16)
    uwq_spec = pl.BlockSpec((1, nh, r, 3 * hd), lambda b, i: (b, 0, i, 0))
    kt_spec = pl.BlockSpec((1, nh, hd, r), lambda b, i: (b, 0, 0, i))
    return pl.pallas_call(
        _gdn_local_kernel,
        grid=(nb, seq // r),
        in_specs=[pl.BlockSpec((1, r, GDN_WIDTH), lambda b, i: (b, i, 0)),
                  pl.BlockSpec((1, r, GDN_WIDTH), lambda b, i: (b, i, 1)),
                  pl.BlockSpec((1, r, GDN_WIDTH), lambda b, i: (b, i, 2)),
                  pl.BlockSpec((1, r, 4 * nh), lambda b, i: (b, i, 0)),
                  pl.BlockSpec((1, 4 * nh, r), lambda b, i: (b, 0, i))],
        out_specs=[uwq_spec, uwq_spec, kt_spec, kt_spec,
                   pl.BlockSpec((1, nh, r, 2 * GDN_CHUNK), lambda b, i: (b, 0, i, 0)),
                   pl.BlockSpec((1, r // GDN_CHUNK, SUBLANES, LANES), lambda b, i: (b, i, 0, 0))],
        out_shape=[uwq_sd, uwq_sd, kt_sd, kt_sd,
                   jax.ShapeDtypeStruct((nb, nh, seq, 2 * GDN_CHUNK), BF16),
                   jax.ShapeDtypeStruct((nb, seq // GDN_CHUNK, SUBLANES, LANES), F32)],
        compiler_params=_cparams("parallel", "parallel"),
        name="gdn_local",
    )(qkv, qkv, qkv, gates_col, gates_row)


def _gdn_state_kernel(uwqf, ktf, inf, egf, uwqb, ktb, inb, egb, of_ref, ob_ref, st_ref):
    @pl.when(pl.program_id(1) == 0)
    def _():
        st_ref[...] = jnp.zeros_like(st_ref)

    c = GDN_CHUNK
    hd = GDN_HEAD_DIM
    nc = uwqf.shape[2] // c
    chains = [(d, h) for d in range(2) for h in range(GDN_HEADS)]
    refs = ((uwqf, ktf, inf, egf, of_ref), (uwqb, ktb, inb, egb, ob_ref))
    states = [st_ref[j] for j in range(len(chains))]
    for step in range(nc):
        sws = []
        for j, (d, h) in enumerate(chains):
            ci = step if d == 0 else nc - 1 - step
            rows = slice(ci * c, (ci + 1) * c)
            uwq = refs[d][0]
            wq = jnp.concatenate([uwq[0, h, rows, hd:2 * hd], uwq[0, h, rows, 2 * hd:3 * hd]], axis=0)
            sws.append(_dot(wq, states[j].astype(BF16)))
        for j, (d, h) in enumerate(chains):
            ci = step if d == 0 else nc - 1 - step
            rows = slice(ci * c, (ci + 1) * c)
            uwq, kt, intra, eg, o_ref = refs[d]
            sw = sws[j]
            v16 = (uwq[0, h, rows, 0:hd].astype(F32) - sw[:c]).astype(BF16)
            out = sw[c:] + _dot(intra[0, h, rows, d * c:(d + 1) * c], v16)
            o_ref[0, rows, h * hd:(h + 1) * hd] = out.astype(o_ref.dtype)
            lane = d * GDN_HEADS + h
            states[j] = states[j] * eg[0, ci, lane:lane + 1, :] + _dot(kt[0, h, :, rows], v16)
    for j in range(len(chains)):
        st_ref[j] = states[j]


def gdn_state_scan(uwq_f, uwq_b, kt_f, kt_b, intra, eg):
    nb, nh, seq, _ = uwq_f.shape
    hd = GDN_HEAD_DIM
    nc = GDN_STATE_CHUNKS
    r = nc * GDN_CHUNK
    n = seq // r

    def specs(pos):
        return [pl.BlockSpec((1, nh, r, 3 * hd), lambda b, i: (b, 0, pos(i), 0)),
                pl.BlockSpec((1, nh, hd, r), lambda b, i: (b, 0, 0, pos(i))),
                pl.BlockSpec((1, nh, r, 2 * GDN_CHUNK), lambda b, i: (b, 0, pos(i), 0)),
                pl.BlockSpec((1, nc, SUBLANES, LANES), lambda b, i: (b, pos(i), 0, 0))]

    fwd = lambda i: i
    bwd = lambda i: n - 1 - i
    out_sd = jax.ShapeDtypeStruct((nb, seq, GDN_WIDTH), BF16)
    return pl.pallas_call(
        _gdn_state_kernel,
        grid=(nb, n),
        in_specs=specs(fwd) + specs(bwd),
        out_specs=[pl.BlockSpec((1, r, GDN_WIDTH), lambda b, i: (b, i, 0)),
                   pl.BlockSpec((1, r, GDN_WIDTH), lambda b, i: (b, n - 1 - i, 0))],
        out_shape=[out_sd, out_sd],
        scratch_shapes=[pltpu.VMEM((2 * nh, hd, hd), F32)],
        compiler_params=_cparams("parallel", "arbitrary"),
        name="gdn_state_scan",
    )(uwq_f, kt_f, intra, eg, uwq_b, kt_b, intra, eg)


def _out_proj_kernel(x_ref, mod_ref, ys5_ref, cx_ref, cb_ref, cc_ref, cxp_ref, cxn_ref, ccp_ref, ccn_ref,
                     cw_ref, of_ref, ob_ref, z_ref, gnw_ref, w_ref, o_ref, *, seq):
    tm = x_ref.shape[0]
    t0 = (pl.program_id(0) * tm) % seq
    f32 = lambda ref, rows=slice(None): ref[rows, :].astype(F32)
    last = slice(HALO_ROWS - 1, HALO_ROWS)
    prev_row = jnp.where(t0 == 0, 0.0, f32(ccp_ref, last) * f32(cxp_ref, last))
    next_row = jnp.where(t0 + tm == seq, 0.0, f32(ccn_ref, slice(0, 1)) * f32(cxn_ref, slice(0, 1)))
    y_conv = f32(cb_ref) * _conv3_rows(f32(cc_ref) * f32(cx_ref), prev_row, next_row, cw_ref)
    o = f32(of_ref) + f32(ob_ref)
    z = f32(z_ref)
    heads = []
    for h in range(GDN_HEADS):
        cols = slice(h * GDN_HEAD_DIM, (h + 1) * GDN_HEAD_DIM)
        heads.append(_rms(o[:, cols]) * gnw_ref[...] * _silu(z[:, cols]))
    y_gdn = jnp.concatenate(heads, axis=1)
    c0, c1 = S5_WIDTH, S5_WIDTH + CONV_WIDTH
    mix = (_dot(ys5_ref[...].astype(BF16), w_ref[0:c0, :])
           + _dot(y_conv.astype(BF16), w_ref[c0:c1, :])
           + _dot(y_gdn.astype(BF16), w_ref[c1:, :]))
    o_ref[...] = x_ref[...] + mod_ref[0, 2:3, :] * mix


def out_projection(x2d, mod, y_s5, proj, conv_w, o_f, o_b, gdn_norm_w, w_out, seq):
    m, d = x2d.shape
    tm = ROW_TILE
    per_seq = seq // tm
    cw = CONV_WIDTH
    row = lambda c: pl.BlockSpec((tm, cw), lambda i: (i, c))
    cxp, cxn = _halo_specs(tm, cw, lambda: 1, m)
    ccp, ccn = _halo_specs(tm, cw, lambda: 3, m)
    gw = GDN_WIDTH
    return pl.pallas_call(
        functools.partial(_out_proj_kernel, seq=seq),
        grid=(m // tm,),
        in_specs=[pl.BlockSpec((tm, d), lambda i: (i, 0)),
                  pl.BlockSpec((1, 6, d), lambda i: (i // per_seq, 0, 0)),
                  pl.BlockSpec((tm, S5_WIDTH), lambda i: (i % per_seq, i // per_seq)),
                  row(1), row(2), row(3), cxp, cxn, ccp, ccn,
                  pl.BlockSpec((3, cw), lambda i: (0, 0)),
                  pl.BlockSpec((tm, gw), lambda i: (i, 0)),
                  pl.BlockSpec((tm, gw), lambda i: (i, 0)),
                  pl.BlockSpec((tm, gw), lambda i: (i, (PROJ_MAIN - gw) // gw)),
                  pl.BlockSpec((1, GDN_HEAD_DIM), lambda i: (0, 0)),
                  pl.BlockSpec((d, d), lambda i: (0, 0))],
        out_specs=pl.BlockSpec((tm, d), lambda i: (i, 0)),
        out_shape=jax.ShapeDtypeStruct((m, d), F32),
        compiler_params=_cparams("parallel"),
        name="out_projection",
    )(x2d, mod, y_s5, proj, proj, proj, proj, proj, proj, proj, conv_w, o_f, o_b, proj, gdn_norm_w, w_out)


def _first_argmax(vals, rows, n):
    m = jnp.max(vals, axis=0, keepdims=True)
    return m, jnp.min(jnp.where(vals == m, rows, n), axis=0, keepdims=True)


def _route_kernel(x_ref, mod_ref, nw_ref, wr_ref, br_ref, h_ref, idx_ref, wgt_ref, cnt_ref, carry_ref, before_ref):
    step = pl.program_id(0)
    tm = x_ref.shape[0]

    @pl.when(step == 0)
    def _():
        carry_ref[...] = jnp.zeros_like(carry_ref)
        earlier = lax.broadcasted_iota(jnp.int32, (tm, tm), 0) < lax.broadcasted_iota(jnp.int32, (tm, tm), 1)
        before_ref[...] = jnp.where(earlier, 1.0, 0.0).astype(BF16)

    h = _rms(x_ref[...]) * nw_ref[...] * (1.0 + mod_ref[0, 4:5, :]) + mod_ref[0, 3:4, :]
    h_ref[...] = h.reshape(h_ref.shape)
    nr = br_ref.shape[0]
    h_hi = h.astype(BF16)
    h_lo = (h - h_hi.astype(F32)).astype(BF16)
    head = _dot_nt(wr_ref[...], h_hi)
    logits = head[:nr] + head[nr:] + _dot_nt(wr_ref[0:nr, :], h_lo) + br_ref[...]
    lg = logits[N_EXPERTS:N_EXPERTS + MOE_GROUPS, :]
    rows_g = lax.broadcasted_iota(jnp.int32, lg.shape, 0)
    g_max, g_idx = _first_argmax(lg, rows_g, MOE_GROUPS)
    g_w = 1.0 / jnp.sum(jnp.exp(lg - g_max), axis=0, keepdims=True)
    le = jnp.zeros((EXPERTS_PER_GROUP, tm), F32)
    for g in range(MOE_GROUPS):
        le = jnp.where(g_idx == g, logits[g * EXPERTS_PER_GROUP:(g + 1) * EXPERTS_PER_GROUP, :], le)
    rows_e = lax.broadcasted_iota(jnp.int32, le.shape, 0)
    m1, i1 = _first_argmax(le, rows_e, EXPERTS_PER_GROUP)
    m2, i2 = _first_argmax(jnp.where(rows_e == i1, -jnp.inf, le), rows_e, EXPERTS_PER_GROUP)
    r = jnp.exp(m2 - m1)
    w1 = g_w / (1.0 + r)
    e1 = g_idx * EXPERTS_PER_GROUP + i1
    e2 = g_idx * EXPERTS_PER_GROUP + i2
    rows_x = lax.broadcasted_iota(jnp.int32, (N_EXPERTS, tm), 0)
    hit1 = rows_x == e1
    hit2 = rows_x == e2
    onehot = jnp.where(hit1 | hit2, 1.0, 0.0)
    ahead = _dot(onehot.astype(BF16), before_ref[...]) + carry_ref[:, 0:1]
    rank1 = jnp.sum(jnp.where(hit1, ahead, 0.0), axis=0, keepdims=True)
    rank2 = jnp.sum(jnp.where(hit2, ahead, 0.0), axis=0, keepdims=True)
    idx_ref[0, 0:1, :] = e1
    idx_ref[0, 1:2, :] = e2
    idx_ref[0, 2:3, :] = rank1.astype(jnp.int32)
    idx_ref[0, 3:4, :] = rank2.astype(jnp.int32)
    wgt_ref[0:1, :] = w1
    wgt_ref[1:2, :] = w1 * r
    carry_ref[...] = carry_ref[...] + jnp.sum(onehot, axis=1, keepdims=True)
    cnt_ref[...] = carry_ref[...]


def moe_route(x2d, mod, norm_w, w_router_t, b_router, seq):
    m, d = x2d.shape
    tm = ROUTE_TILE
    per_seq = seq // tm
    nr = w_router_t.shape[0]
    return pl.pallas_call(
        _route_kernel,
        grid=(m // tm,),
        in_specs=[pl.BlockSpec((tm, d), lambda i: (i, 0)),
                  pl.BlockSpec((1, 6, d), lambda i: (i // per_seq, 0, 0)),
                  pl.BlockSpec((1, d), lambda i: (0, 0)),
                  pl.BlockSpec((nr, d), lambda i: (0, 0)),
                  pl.BlockSpec((nr // 2, 1), lambda i: (0, 0))],
        out_specs=[pl.BlockSpec((tm,) + SLAB, lambda i: (i, 0, 0)),
                   pl.BlockSpec((1, 4, tm), lambda i: (i, 0, 0)),
                   pl.BlockSpec((2, tm), lambda i: (0, i)),
                   pl.BlockSpec((N_EXPERTS, LANES), lambda i: (0, 0))],
        out_shape=[jax.ShapeDtypeStruct((m,) + SLAB, F32),
                   jax.ShapeDtypeStruct((m // tm, 4, tm), jnp.int32),
                   jax.ShapeDtypeStruct((2, m), F32),
                   jax.ShapeDtypeStruct((N_EXPERTS, LANES), F32)],
        scratch_shapes=[pltpu.VMEM((N_EXPERTS, LANES), F32), pltpu.VMEM((tm, tm), BF16)],
        compiler_params=_cparams("arbitrary"),
        name="moe_route",
    )(x2d, mod, norm_w, w_router_t, b_router)


def _row_copy(src_ref, src_row, dst_ref, dst_row, sem):
    return pltpu.make_async_copy(src_ref.at[pl.ds(src_row, 1)], dst_ref.at[pl.ds(dst_row, 1)], sem)


def _scatter_kernel(last_ref, slot_ref, h_ref, xs_ref, zero_ref, sem):
    tm = h_ref.shape[0]

    @pl.when(pl.program_id(0) == 0)
    def _():
        zero_ref[...] = jnp.zeros_like(zero_ref)

        def fill(e, c):
            @pl.when(last_ref[e] >= 0)
            def _():
                pltpu.make_async_copy(zero_ref, xs_ref.at[pl.ds(last_ref[e], MOE_BLOCK)], sem).start()
            return c

        def drain(e, c):
            @pl.when(last_ref[e] >= 0)
            def _():
                pltpu.make_async_copy(zero_ref, xs_ref.at[pl.ds(0, MOE_BLOCK)], sem).wait()
            return c

        lax.fori_loop(0, N_EXPERTS, fill, 0)
        lax.fori_loop(0, N_EXPERTS, drain, 0)

        n_blocks = xs_ref.shape[0] // MOE_BLOCK

        def fill_unused(b, c):
            pltpu.make_async_copy(zero_ref, xs_ref.at[pl.ds(b * MOE_BLOCK, MOE_BLOCK)], sem).start()
            return c

        def drain_unused(b, c):
            pltpu.make_async_copy(zero_ref, xs_ref.at[pl.ds(0, MOE_BLOCK)], sem).wait()
            return c

        lax.fori_loop(last_ref[N_EXPERTS], n_blocks, fill_unused, 0)
        lax.fori_loop(last_ref[N_EXPERTS], n_blocks, drain_unused, 0)

    def start(r, c):
        _row_copy(h_ref, r, xs_ref, slot_ref[0, 0, r], sem).start(priority=0)
        _row_copy(h_ref, r, xs_ref, slot_ref[0, 1, r], sem).start(priority=1)
        return c

    def wait(r, c):
        _row_copy(h_ref, 0, xs_ref, 0, sem).wait()
        _row_copy(h_ref, 0, xs_ref, 0, sem).wait()
        return c

    lax.fori_loop(0, tm, start, 0, unroll=DMA_UNROLL)
    lax.fori_loop(0, tm, wait, 0, unroll=DMA_UNROLL)


def moe_scatter(h_slab, slots, last_block_start, n_slots):
    m = h_slab.shape[0]
    tm = ROUTE_TILE
    return pl.pallas_call(
        _scatter_kernel,
        grid_spec=pltpu.PrefetchScalarGridSpec(
            num_scalar_prefetch=1,
            grid=(m // tm,),
            in_specs=[pl.BlockSpec((1, 2, tm), lambda i, last: (i, 0, 0), memory_space=pltpu.SMEM),
                      pl.BlockSpec((tm,) + SLAB, lambda i, last: (i, 0, 0))],
            out_specs=pl.BlockSpec(memory_space=pl.ANY),
            scratch_shapes=[pltpu.VMEM((MOE_BLOCK,) + SLAB, F32), pltpu.SemaphoreType.DMA(())]),
        out_shape=jax.ShapeDtypeStruct((n_slots,) + SLAB, F32),
        compiler_params=_cparams("arbitrary"),
        name="moe_scatter",
    )(last_block_start, slots, h_slab)


def _expert_kernel(be_ref, nused_ref, xs_ref, wgu_ref, wd_ref, y_ref, wgu16_ref, wd16_ref):
    i = pl.program_id(0)
    changed = jnp.logical_or(i == 0, be_ref[i] != be_ref[jnp.maximum(i - 1, 0)])

    @pl.when(changed)
    def _():
        wgu16_ref[...] = wgu_ref[0, 0].astype(BF16)
        wd16_ref[...] = wd_ref[0, 0].astype(BF16)

    @pl.when(i < nused_ref[0])
    def _():
        x = xs_ref[...].reshape(MOE_BLOCK, D_MODEL).astype(BF16)
        gu = _dot(x, wgu16_ref[...])
        act = _silu(gu[:, :EXPERT_FF]) * gu[:, EXPERT_FF:]
        y = _dot(act.astype(BF16), wd16_ref[...])
        y_ref[...] = y.reshape(y_ref.shape)

    @pl.when(i >= nused_ref[0])
    def _():
        y_ref[...] = jnp.zeros_like(y_ref)


def moe_experts(xs, blk_expert, n_used, w_gate_up, w_down, layer):
    n_slots = xs.shape[0]
    n_blocks = n_slots // MOE_BLOCK
    d = D_MODEL
    return pl.pallas_call(
        _expert_kernel,
        grid_spec=pltpu.PrefetchScalarGridSpec(
            num_scalar_prefetch=2,
            grid=(n_blocks,),
            in_specs=[pl.BlockSpec((MOE_BLOCK,) + SLAB, lambda i, be, nu: (jnp.minimum(i, nu[0] - 1), 0, 0)),
                      pl.BlockSpec((1, 1, d, 2 * EXPERT_FF), lambda i, be, nu: (layer, be[i], 0, 0)),
                      pl.BlockSpec((1, 1, EXPERT_FF, d), lambda i, be, nu: (layer, be[i], 0, 0))],
            out_specs=pl.BlockSpec((MOE_BLOCK,) + SLAB, lambda i, be, nu: (i, 0, 0)),
            scratch_shapes=[pltpu.VMEM((d, 2 * EXPERT_FF), BF16), pltpu.VMEM((EXPERT_FF, d), BF16)]),
        out_shape=jax.ShapeDtypeStruct((n_slots,) + SLAB, F32),
        compiler_params=_cparams("arbitrary"),
        name="moe_experts",
    )(blk_expert, n_used, xs, w_gate_up, w_down)


def _combine_kernel(slot_ref, next_slot_ref, x_ref, mod_ref, wgt_ref, fnw_ref, yb_ref, o_ref, y1_ref, y2_ref, sem, *,
                    final):
    tm = x_ref.shape[0]
    i = pl.program_id(0)
    cur = i % 2

    def gather(slots, buf):
        def start(r, c):
            _row_copy(yb_ref, slots[0, 0, r], y1_ref.at[buf], r, sem.at[buf]).start(priority=0)
            _row_copy(yb_ref, slots[0, 1, r], y2_ref.at[buf], r, sem.at[buf]).start(priority=1)
            return c
        lax.fori_loop(0, tm, start, 0, unroll=DMA_UNROLL)

    @pl.when(i == 0)
    def _():
        gather(slot_ref, 0)

    @pl.when(i + 1 < pl.num_programs(0))
    def _():
        gather(next_slot_ref, 1 - cur)

    def wait(r, c):
        _row_copy(yb_ref, 0, y1_ref.at[cur], 0, sem.at[cur]).wait()
        _row_copy(yb_ref, 0, y2_ref.at[cur], 0, sem.at[cur]).wait()
        return c

    lax.fori_loop(0, tm, wait, 0, unroll=DMA_UNROLL)
    w1 = wgt_ref[:, 0:1]
    w2 = wgt_ref[:, 1:2]
    y = y1_ref[cur].reshape(x_ref.shape) * w1 + y2_ref[cur].reshape(x_ref.shape) * w2
    o_ref[...] = x_ref[...] + mod_ref[0, 5:6, :] * y
    if final:
        o_ref[...] = _rms(o_ref[...]) * fnw_ref[...]


def moe_combine(x2d, mod, yb, slots, weights, final_norm_w, seq, final):
    m, d = x2d.shape
    tm = ROUTE_TILE
    per_seq = seq // tm
    n = m // tm
    slot_spec = lambda pos: pl.BlockSpec((1, 2, tm), lambda i: (pos(i), 0, 0), memory_space=pltpu.SMEM)
    return pl.pallas_call(
        functools.partial(_combine_kernel, final=final),
        grid=(n,),
        in_specs=[slot_spec(lambda i: i), slot_spec(lambda i: jnp.minimum(i + 1, n - 1)),
                  pl.BlockSpec((tm, d), lambda i: (i, 0)),
                  pl.BlockSpec((1, 6, d), lambda i: (i // per_seq, 0, 0)),
                  pl.BlockSpec((tm, 2), lambda i: (i, 0)),
                  pl.BlockSpec((1, d), lambda i: (0, 0)),
                  pl.BlockSpec(memory_space=pl.ANY)],
        out_specs=pl.BlockSpec((tm, d), lambda i: (i, 0)),
        out_shape=jax.ShapeDtypeStruct((m, d), F32),
        scratch_shapes=[pltpu.VMEM((2, tm) + SLAB, F32), pltpu.VMEM((2, tm) + SLAB, F32),
                        pltpu.SemaphoreType.DMA((2,))],
        compiler_params=_cparams("arbitrary"),
        name="moe_combine",
    )(slots, slots, x2d, mod, weights, final_norm_w, yb)


def hier_moe_layer(x2d, mod, norm_w, w_router_t, b_router, w_gate_up, w_down, layer, final_norm_w, seq, final):
    m = x2d.shape[0]
    h_slab, idx, wgt, counts = moe_route(x2d, mod, norm_w, w_router_t, b_router, seq)
    counts = counts[:, 0].astype(jnp.int32)
    padded = (counts + MOE_BLOCK - 1) // MOE_BLOCK * MOE_BLOCK
    pad_end = jnp.cumsum(padded)
    pad_start = (pad_end - padded).astype(jnp.int32)
    n_blocks = 2 * m // MOE_BLOCK + N_EXPERTS
    blk_start = jnp.arange(n_blocks, dtype=jnp.int32) * MOE_BLOCK
    blk_expert = jnp.minimum(jnp.sum((pad_end[None, :] <= blk_start[:, None]).astype(jnp.int32), axis=1),
                             N_EXPERTS - 1)
    n_used = (pad_end[-1:] // MOE_BLOCK).astype(jnp.int32)
    fill_plan = jnp.concatenate([jnp.where(padded > 0, pad_end - MOE_BLOCK, -1), n_used]).astype(jnp.int32)
    is_expert = idx[:, 0:2, :, None] == jnp.arange(N_EXPERTS, dtype=jnp.int32)
    slots = jnp.sum(jnp.where(is_expert, pad_start, 0), axis=-1) + idx[:, 2:4, :]
    xs = moe_scatter(h_slab, slots, fill_plan, n_blocks * MOE_BLOCK)
    yb = moe_experts(xs, blk_expert, n_used, w_gate_up, w_down, layer)
    return moe_combine(x2d, mod, yb, slots, wgt.T, final_norm_w, seq, final)


def _layer_params(l, p):
    w_in = p['w_in'][l]
    w_gate = w_in[:, PROJ_MAIN:].T
    w_router_t = jnp.zeros((ROUTER_ROWS, D_MODEL), F32)
    w_router_t = w_router_t.at[:N_EXPERTS].set(p['moe_w_expert'][l].T)
    w_router_t = w_router_t.at[N_EXPERTS:N_EXPERTS + MOE_GROUPS].set(p['moe_w_group'][l].T)
    w_router_hi = w_router_t.astype(BF16)
    w_router_t = jnp.concatenate([w_router_hi, (w_router_t - w_router_hi.astype(F32)).astype(BF16)], axis=0)
    b_router = jnp.zeros((ROUTER_ROWS, 1), F32)
    b_router = b_router.at[:N_EXPERTS, 0].set(p['moe_b_expert'][l])
    b_router = b_router.at[N_EXPERTS:N_EXPERTS + MOE_GROUPS, 0].set(p['moe_b_group'][l])
    s5 = [s5_discretise(p['s5_lam_re'][l, d], p['s5_lam_im'][l, d], p['s5_log_step'][l, d],
                        p['s5_b_re'][l, d], p['s5_b_im'][l, d], p['s5_c_re'][l, d], p['s5_c_im'][l, d])
          for d in range(2)]
    return dict(
        norm_mix=p['norm_mix'][l][None, :], norm_ffn=p['norm_ffn'][l][None, :],
        w_main=w_in[:, :PROJ_MAIN].astype(BF16), w_gate=w_gate.astype(BF16),
        w_out=p['w_out'][l].astype(BF16), s5=s5,
        s5_d=p['s5_d'][l][None, :], s5_w_glu=p['s5_w_glu'][l].astype(BF16), s5_b_glu=p['s5_b_glu'][l][None, :],
        conv_w=p['conv_w'][l], gdn_conv_w=p['gdn_conv_w'][l], gdn_a_log=p['gdn_a_log'][l],
        gdn_dt_bias=p['gdn_dt_bias'][l], gdn_norm_w=p['gdn_norm_w'][l][None, :],
        w_router_t=w_router_t, b_router=b_router,
        w_gate_up=p['moe_w_gate_up'], w_down=p['moe_w_down'])


def _trunk(x, mods, layers, final_norm_w):
    nb, seq, d = x.shape
    m = nb * seq
    x2d = x.reshape(m, d)
    for l, lp in enumerate(layers):
        mod = mods[l].reshape(nb, 6, d)
        proj, u_t, pre_t = in_projection(x2d, mod, lp['norm_mix'], lp['w_main'], lp['w_gate'], seq)
        (a_f, bm_f, cm_f), (a_b, bm_b, cm_b) = lp['s5']
        y_s5 = s5_mixer(u_t, a_f, bm_f, cm_f, a_b, bm_b, cm_b, lp['s5_d'], lp['s5_w_glu'], lp['s5_b_glu'])
        qkv = gdn_prepare(proj, lp['gdn_conv_w'], seq).reshape(nb, seq, 3 * GDN_WIDTH)
        gates = gdn_gates(pre_t, lp['gdn_a_log'], lp['gdn_dt_bias'])
        uwq_f, uwq_b, kt_f, kt_b, intra, eg = gdn_local(qkv, gates.transpose(0, 2, 1), gates)
        o_f, o_b = gdn_state_scan(uwq_f, uwq_b, kt_f, kt_b, intra, eg)
        x2d = out_projection(x2d, mod, y_s5, proj, lp['conv_w'], o_f.reshape(m, GDN_WIDTH),
                             o_b.reshape(m, GDN_WIDTH), lp['gdn_norm_w'], lp['w_out'], seq)
        x2d = hier_moe_layer(x2d, mod, lp['norm_ffn'], lp['w_router_t'], lp['b_router'],
                             lp['w_gate_up'], lp['w_down'], l, final_norm_w, seq, final=(l == len(layers) - 1))
    return x2d.reshape(nb, seq, d)


def kernel(x_prompt, x_sample, c_prompt, c_sample, w_ada, b_ada, norm_mix, norm_ffn, w_in, w_out, s5_lam_re, s5_lam_im, s5_log_step, s5_b_re, s5_b_im, s5_c_re, s5_c_im, s5_d, s5_w_glu, s5_b_glu, conv_w, gdn_conv_w, gdn_a_log, gdn_dt_bias, gdn_norm_w, moe_w_group, moe_b_group, moe_w_expert, moe_b_expert, moe_w_gate_up, moe_w_down, final_norm):
    p = dict(norm_mix=norm_mix, norm_ffn=norm_ffn, w_in=w_in, w_out=w_out, s5_lam_re=s5_lam_re,
             s5_lam_im=s5_lam_im, s5_log_step=s5_log_step, s5_b_re=s5_b_re, s5_b_im=s5_b_im, s5_c_re=s5_c_re,
             s5_c_im=s5_c_im, s5_d=s5_d, s5_w_glu=s5_w_glu, s5_b_glu=s5_b_glu, conv_w=conv_w,
             gdn_conv_w=gdn_conv_w, gdn_a_log=gdn_a_log, gdn_dt_bias=gdn_dt_bias, gdn_norm_w=gdn_norm_w,
             moe_w_group=moe_w_group, moe_b_group=moe_b_group, moe_w_expert=moe_w_expert,
             moe_b_expert=moe_b_expert, moe_w_gate_up=moe_w_gate_up, moe_w_down=moe_w_down)
    depth = w_ada.shape[0]
    layers = [_layer_params(l, p) for l in range(depth)]
    nbp = c_prompt.shape[0]
    mods = ada_modulation(jnp.concatenate([c_prompt, c_sample], axis=0), w_ada, b_ada)
    fnw = final_norm[None, :]
    y_prompt = _trunk(x_prompt, mods[:, :nbp], layers, fnw)
    y_sample = _trunk(x_sample, mods[:, nbp:], layers, fnw)
    return (y_prompt, y_sample)
```

```python
import functools

import jax
import jax.numpy as jnp
from jax import lax
from jax.experimental import pallas as pl
from jax.experimental.pallas import tpu as pltpu

F32 = jnp.float32
BF16 = jnp.bfloat16
HIGHEST = lax.Precision.HIGHEST

D_MODEL = 1024
DEPTH = 4
S5_WIDTH = 256
S5_GROUP = 16
S5_GROUPS = 16
S5_STATE = 64
S5_LANES = S5_GROUPS * S5_STATE
CONV_WIDTH = 256
GDN_WIDTH = 512
GDN_HEADS = 4
GDN_HEAD_DIM = 128
GDN_CHUNK = 64
PROJ_MAIN = S5_WIDTH + 3 * CONV_WIDTH + 4 * GDN_WIDTH
N_GATE_COLS = 4 * GDN_HEADS
MOE_GROUPS = 4
EXPERTS_PER_GROUP = 8
N_EXPERTS = 32
EXPERT_FF = 512
MOE_BLOCK = 512
EPS = 1e-6

SUBLANES = 8
LANES = 128
SLAB = (SUBLANES, LANES)
VMEM_LIMIT = 56 * 1024 * 1024

ROW_TILE = 512
S5_TIME_TILE = 128
GDN_STATE_CHUNKS = 8
GDN_STATE_BATCH = 2
GDN_LOCAL_PAIRS = 2
ROUTE_TILE = 512
ROUTER_ROWS = N_EXPERTS + 16
DMA_UNROLL = 8


def _cparams(*sem):
    return pltpu.CompilerParams(dimension_semantics=sem, vmem_limit_bytes=VMEM_LIMIT)


def _dot(a, b, **kw):
    return jnp.dot(a, b, preferred_element_type=F32, **kw)


def _dot_nt(a, b, **kw):
    return lax.dot_general(a, b, (((1,), (1,)), ((), ())), preferred_element_type=F32, **kw)


def _dot_tn(a, b, **kw):
    return lax.dot_general(a, b, (((0,), (0,)), ((), ())), preferred_element_type=F32, **kw)


def _sigmoid(x):
    return 0.5 * jnp.tanh(0.5 * x) + 0.5


def _silu(x):
    return x * _sigmoid(x)


def _gelu_tanh(x):
    return 0.5 * x * (1.0 + jnp.tanh(0.7978845608028654 * (x + 0.044715 * (x * x * x))))


def _softplus(x):
    return jnp.maximum(x, 0.0) + jnp.log(1.0 + jnp.exp(-jnp.abs(x)))


def _rms(x):
    return x * lax.rsqrt(jnp.mean(x * x, axis=-1, keepdims=True) + EPS)


def _ada_kernel(c_ref, w_ref, b_ref, o_ref):
    o_ref[0] = _dot(_silu(c_ref[...]), w_ref[0], precision=HIGHEST) + b_ref[0]


def ada_modulation(c_all, w_ada, b_ada):
    nb = c_all.shape[0]
    depth, d, n = w_ada.shape
    tn = 1536
    return pl.pallas_call(
        _ada_kernel,
        grid=(depth, n // tn),
        in_specs=[pl.BlockSpec((nb, d), lambda l, j: (0, 0)),
                  pl.BlockSpec((1, d, tn), lambda l, j: (l, 0, j)),
                  pl.BlockSpec((1, 1, tn), lambda l, j: (l, 0, j))],
        out_specs=pl.BlockSpec((1, nb, tn), lambda l, j: (l, 0, j)),
        out_shape=jax.ShapeDtypeStruct((depth, nb, n), F32),
        compiler_params=_cparams("parallel", "parallel"),
        name="ada_modulation",
    )(c_all, w_ada, b_ada.reshape(depth, 1, n))


def _in_proj_kernel(x_ref, mod_ref, nw_ref, w_ref, wg_ref, proj_ref, u_ref, gate_ref):
    h = _rms(x_ref[...]) * nw_ref[...] * (1.0 + mod_ref[0, 1:2, :]) + mod_ref[0, 0:1, :]
    hb = h.astype(BF16)
    proj = _dot(hb, w_ref[...]).astype(proj_ref.dtype)
    proj_ref[...] = proj
    u_ref[...] = proj[:, :S5_WIDTH]
    gate_ref[0] = _dot_nt(wg_ref[...], hb)


def in_projection(x2d, mod, norm_w, w_main, w_gate, seq):
    m, d = x2d.shape
    tm = ROW_TILE
    per_seq = seq // tm
    nb = m // seq
    return pl.pallas_call(
        _in_proj_kernel,
        grid=(m // tm,),
        in_specs=[pl.BlockSpec((tm, d), lambda i: (i, 0)),
                  pl.BlockSpec((1, 6, d), lambda i: (i // per_seq, 0, 0)),
                  pl.BlockSpec((1, d), lambda i: (0, 0)),
                  pl.BlockSpec((d, PROJ_MAIN), lambda i: (0, 0)),
                  pl.BlockSpec((N_GATE_COLS, d), lambda i: (0, 0))],
        out_specs=[pl.BlockSpec((tm, PROJ_MAIN), lambda i: (i, 0)),
                   pl.BlockSpec((tm, S5_WIDTH), lambda i: (i % per_seq, i // per_seq)),
                   pl.BlockSpec((1, N_GATE_COLS, tm), lambda i: (i // per_seq, 0, i % per_seq))],
        out_shape=[jax.ShapeDtypeStruct((m, PROJ_MAIN), BF16),
                   jax.ShapeDtypeStruct((seq, nb * S5_WIDTH), BF16),
                   jax.ShapeDtypeStruct((nb, N_GATE_COLS, seq), F32)],
        compiler_params=_cparams("parallel"),
        name="in_projection",
    )(x2d, mod, norm_w, w_main, w_gate)


def _s5_scan(x_ref, h0_re, h0_im, a_re, a_im, steps, reverse):
    def body(s, carry):
        hr, hi = carry
        t = (steps - 1 - s) if reverse else s
        rows = pl.ds(pl.multiple_of(t * SUBLANES, SUBLANES), SUBLANES)
        xr = x_ref[rows, 0:S5_LANES]
        xi = x_ref[rows, S5_LANES:2 * S5_LANES]
        nr = a_re * hr - a_im * hi + xr
        ni = a_re * hi + a_im * hr + xi
        x_ref[rows, 0:S5_LANES] = nr
        x_ref[rows, S5_LANES:2 * S5_LANES] = ni
        return nr, ni
    return lax.fori_loop(0, steps, body, (h0_re, h0_im), unroll=4)


def _s5_direction(u_ref, a_ref, bmat_ref, cmat_ref, x_ref, st_ref, reverse):
    tc = u_ref.shape[0]
    w = S5_WIDTH
    nb = u_ref.shape[1] // w
    first = pl.program_id(0) == 0

    @pl.when(first)
    def _():
        st_ref[...] = jnp.zeros_like(st_ref)

    u = u_ref[...].astype(F32).reshape(tc * nb, w)
    x_ref[...] = _dot(u.astype(BF16), bmat_ref[...])
    a_re = jnp.broadcast_to(a_ref[0:1, :], (nb, S5_LANES))
    a_im = jnp.broadcast_to(a_ref[1:2, :], (nb, S5_LANES))
    hr, hi = _s5_scan(x_ref, st_ref[0], st_ref[1], a_re, a_im, tc, reverse)
    st_ref[0] = hr
    st_ref[1] = hi
    return u, _dot(x_ref[...].astype(BF16), cmat_ref[...])


def _s5_fwd_kernel(u_ref, a_ref, bmat_ref, cmat_ref, y_ref, x_ref, st_ref):
    _, y = _s5_direction(u_ref, a_ref, bmat_ref, cmat_ref, x_ref, st_ref, reverse=False)
    y_ref[...] = y.reshape(y_ref.shape)


def _s5_bwd_kernel(u_ref, yf_ref, a_ref, bmat_ref, cmat_ref, d_ref, wglu_ref, bglu_ref, o_ref, x_ref, st_ref):
    u, y = _s5_direction(u_ref, a_ref, bmat_ref, cmat_ref, x_ref, st_ref, reverse=True)
    y = y + u * d_ref[...] + yf_ref[...].reshape(u.shape)
    y = _gelu_tanh(y)
    out = y * _sigmoid(_dot(y.astype(BF16), wglu_ref[...]) + bglu_ref[...])
    o_ref[...] = out.reshape(o_ref.shape).astype(o_ref.dtype)


def s5_mixer(u_t, a_f, bmat_f, cmat_f, a_b, bmat_b, cmat_b, d_skip, w_glu, b_glu):
    seq = u_t.shape[0]
    w = S5_WIDTH
    nb = u_t.shape[1] // w
    assert nb == SUBLANES, "the scan keeps one batch row per sublane"
    tc = S5_TIME_TILE
    n = seq // tc
    rows = tc * nb
    const2 = lambda i: (0, 0)
    par_specs = [pl.BlockSpec((2, S5_LANES), const2),
                 pl.BlockSpec((w, 2 * S5_LANES), const2),
                 pl.BlockSpec((2 * S5_LANES, w), const2)]
    scratch = [pltpu.VMEM((rows, 2 * S5_LANES), F32), pltpu.VMEM((2, nb, S5_LANES), F32)]
    y_f = pl.pallas_call(
        _s5_fwd_kernel,
        grid=(n,),
        in_specs=[pl.BlockSpec((tc, nb * w), lambda i: (i, 0))] + par_specs,
        out_specs=pl.BlockSpec((tc, nb, w), lambda i: (i, 0, 0)),
        out_shape=jax.ShapeDtypeStruct((seq, nb, w), F32),
        scratch_shapes=scratch,
        compiler_params=_cparams("arbitrary"),
        name="s5_forward",
    )(u_t, a_f, bmat_f, cmat_f)
    rev = lambda i: (n - 1 - i, 0, 0)
    rev2 = lambda i: (n - 1 - i, 0)
    return pl.pallas_call(
        _s5_bwd_kernel,
        grid=(n,),
        in_specs=[pl.BlockSpec((tc, nb * w), rev2), pl.BlockSpec((tc, nb, w), rev)] + par_specs
                 + [pl.BlockSpec((1, w), const2), pl.BlockSpec((w, w), const2), pl.BlockSpec((1, w), const2)],
        out_specs=pl.BlockSpec((tc, nb * w), rev2),
        out_shape=jax.ShapeDtypeStruct((seq, nb * w), BF16),
        scratch_shapes=scratch,
        compiler_params=_cparams("arbitrary"),
        name="s5_backward_glu",
    )(u_t, y_f, a_b, bmat_b, cmat_b, d_skip, w_glu, b_glu)


def s5_discretise(lam_re, lam_im, log_step, b_re, b_im, c_re, c_im):
    g, p, h = S5_GROUPS, S5_STATE, S5_GROUP
    lr, li = lam_re.astype(F32), lam_im.astype(F32)
    dt = jnp.exp(log_step.astype(F32))[:, None]
    mag = jnp.exp(lr * dt)
    abr, abi = mag * jnp.cos(li * dt), mag * jnp.sin(li * dt)
    den = lr * lr + li * li
    cr = ((abr - 1.0) * lr + abi * li) / den
    ci = (abi * lr - (abr - 1.0) * li) / den
    bbr = cr[..., None] * b_re - ci[..., None] * b_im
    bbi = cr[..., None] * b_im + ci[..., None] * b_re
    eye = jnp.eye(g, dtype=F32)
    to_b = lambda t: jnp.einsum('gph,gk->ghkp', t, eye).reshape(g * h, g * p)
    bmat = jnp.concatenate([to_b(bbr), to_b(bbi)], axis=1)
    to_c = lambda t: jnp.einsum('ghp,gk->gpkh', t, eye).reshape(g * p, g * h)
    cmat = jnp.concatenate([to_c(c_re.astype(F32)), -to_c(c_im.astype(F32))], axis=0)
    a = jnp.stack([abr.reshape(g * p), abi.reshape(g * p)])
    return a, bmat.astype(BF16), cmat.astype(BF16)


def _conv3_rows(x, prev_row, next_row, w_ref):
    tm = x.shape[0]
    rows = lax.broadcasted_iota(jnp.int32, x.shape, 0)
    xp = jnp.where(rows == 0, prev_row, pltpu.roll(x, 1, 0))
    xn = jnp.where(rows == tm - 1, next_row, pltpu.roll(x, tm - 1, 0))
    return xp * w_ref[0:1, :] + x * w_ref[1:2, :] + xn * w_ref[2:3, :]


HALO_ROWS = 16
GDN_PREP_ROWS = 64


def _halo_specs(tm, width, col_block, n_rows):
    per = tm // HALO_ROWS
    last = n_rows // HALO_ROWS - 1
    prev = pl.BlockSpec((HALO_ROWS, width), lambda i, *_: (jnp.maximum(i * per - 1, 0), col_block(*_)))
    nxt = pl.BlockSpec((HALO_ROWS, width), lambda i, *_: (jnp.minimum((i + 1) * per, last), col_block(*_)))
    return prev, nxt


def _gdn_prep_kernel(x_ref, xp_ref, xn_ref, w_ref, o_ref, *, seq):
    tm = x_ref.shape[0]
    j = pl.program_id(1)
    t0 = (pl.program_id(0) * tm) % seq
    first_tile = t0 == 0
    last_tile = t0 + tm == seq
    normalise = j < 2
    scale = jnp.where(j == 0, GDN_HEAD_DIM ** -0.5, 1.0)
    rc = GDN_PREP_ROWS
    row = lambda ref, r, cols: ref[r:r + 1, cols].astype(F32)
    for h in range(GDN_HEADS):
        cols = slice(h * GDN_HEAD_DIM, (h + 1) * GDN_HEAD_DIM)
        w = w_ref[:, cols]
        for r0 in range(0, tm, rc):
            prev_row = (jnp.where(first_tile, 0.0, row(xp_ref, HALO_ROWS - 1, cols)) if r0 == 0
                        else row(x_ref, r0 - 1, cols))
            next_row = (jnp.where(last_tile, 0.0, row(xn_ref, 0, cols)) if r0 + rc == tm
                        else row(x_ref, r0 + rc, cols))
            y = _silu(_conv3_rows(x_ref[r0:r0 + rc, cols].astype(F32), prev_row, next_row, w))
            yn = y * (lax.rsqrt(jnp.sum(y * y, axis=-1, keepdims=True) + EPS) * scale)
            o_ref[r0:r0 + rc, cols] = jnp.where(normalise, yn, y).astype(o_ref.dtype)


def gdn_prepare(proj, conv_w, seq):
    m = proj.shape[0]
    tm = ROW_TILE
    first = (S5_WIDTH + 3 * CONV_WIDTH) // GDN_WIDTH
    prev, nxt = _halo_specs(tm, GDN_WIDTH, lambda j: first + j, m)
    return pl.pallas_call(
        functools.partial(_gdn_prep_kernel, seq=seq),
        grid=(m // tm, 3),
        in_specs=[pl.BlockSpec((tm, GDN_WIDTH), lambda i, j: (i, first + j)), prev, nxt,
                  pl.BlockSpec((3, GDN_WIDTH), lambda i, j: (0, j))],
        out_specs=pl.BlockSpec((tm, GDN_WIDTH), lambda i, j: (i, j)),
        out_shape=jax.ShapeDtypeStruct((m, 3 * GDN_WIDTH), BF16),
        compiler_params=_cparams("parallel", "parallel"),
        name="gdn_prepare",
    )(proj, proj, proj, conv_w)


def _gdn_gates_kernel(pre_ref, alog_ref, dtb_ref, o_ref):
    seq = pre_ref.shape[2]
    nh2 = 2 * GDN_HEADS
    g = -jnp.exp(alog_ref[...]) * _softplus(pre_ref[0, 0:nh2, :] + dtb_ref[...])
    o_ref[0, nh2:2 * nh2, :] = _sigmoid(pre_ref[0, nh2:2 * nh2, :])
    s = lax.broadcasted_iota(jnp.int32, (LANES, LANES), 0)
    t = lax.broadcasted_iota(jnp.int32, (LANES, LANES), 1)
    same = (s // GDN_CHUNK) == (t // GDN_CHUNK)
    m_f = jnp.where(same & (s <= t), 1.0, 0.0)
    m_b = jnp.where(same & (s >= t), 1.0, 0.0)
    for jb in range(seq // LANES):
        cols = slice(jb * LANES, (jb + 1) * LANES)
        gb = g[:, cols]
        o_ref[0, 0:GDN_HEADS, cols] = _dot(gb, m_f, precision=HIGHEST)[0:GDN_HEADS]
        o_ref[0, GDN_HEADS:nh2, cols] = _dot(gb, m_b, precision=HIGHEST)[GDN_HEADS:nh2]


def gdn_gates(pre_t, a_log, dt_bias):
    nb, rows, seq = pre_t.shape
    col = lambda t: t.astype(F32).reshape(2 * GDN_HEADS, 1)
    return pl.pallas_call(
        _gdn_gates_kernel,
        grid=(nb,),
        in_specs=[pl.BlockSpec((1, rows, seq), lambda b: (b, 0, 0)),
                  pl.BlockSpec((2 * GDN_HEADS, 1), lambda b: (0, 0)),
                  pl.BlockSpec((2 * GDN_HEADS, 1), lambda b: (0, 0))],
        out_specs=pl.BlockSpec((1, rows, seq), lambda b: (b, 0, 0)),
        out_shape=jax.ShapeDtypeStruct((nb, rows, seq), F32),
        compiler_params=_cparams("parallel"),
        name="gdn_gates",
    )(pre_t, col(a_log), col(dt_bias))


def _nilpotent_inverses(mats, nilpotency):
    n = mats[0].shape[0]
    eye = jnp.where(lax.broadcasted_iota(jnp.int32, (n, n), 0) == lax.broadcasted_iota(jnp.int32, (n, n), 1), 1.0, 0.0)
    ps = [eye - a for a in mats]
    a16 = [a.astype(BF16) for a in mats]
    sqs = [_dot(a, a) for a in a16]
    k = 2
    while True:
        sq16 = [s.astype(BF16) for s in sqs]
        ps = [p + _dot(p.astype(BF16), s) for p, s in zip(ps, sq16)]
        k *= 2
        if k >= nilpotency:
            return ps
        sqs = [_dot(s, s) for s in sq16]


def _gdn_local_kernel(q_ref, k_ref, v_ref, gcol_ref, grow_ref, uwqf_ref, uwqb_ref, ktf_ref, ktb_ref, in_ref, eg_ref):
    c = GDN_CHUNK
    r = 2 * c
    pairs = q_ref.shape[1] // r
    hd = GDN_HEAD_DIM
    ii = lax.broadcasted_iota(jnp.int32, (r, r), 0)
    jj = lax.broadcasted_iota(jnp.int32, (r, r), 1)
    same = (ii >= c) == (jj >= c)
    incl = (same & (ii >= jj), same & (ii <= jj))
    strict = (same & (ii > jj), same & (ii < jj))
    first_chunk = lax.broadcasted_iota(jnp.int32, (r, 1), 0) < c
    zeros = jnp.zeros((r, r), F32)
    big_a, rhs, keep = [], [], []
    units = [(p, h) for p in range(pairs) for h in range(GDN_HEADS)]
    for p, h in units:
        rows = slice(p * r, (p + 1) * r)
        cols = slice(h * hd, (h + 1) * hd)
        q16, k16 = q_ref[0, rows, cols], k_ref[0, rows, cols]
        q, k, v = q16.astype(F32), k16.astype(F32), v_ref[0, rows, cols].astype(F32)
        kk = _dot_nt(k16, k16)
        qk = _dot_nt(q16, k16)
        a_dir, rhs_dir = [], []
        for d in range(2):
            lane = d * GDN_HEADS + h
            gc_col = gcol_ref[0, rows, lane:lane + 1]
            gc_row = grow_ref[0, lane:lane + 1, rows]
            beta = gcol_ref[0, rows, 2 * GDN_HEADS + lane:2 * GDN_HEADS + lane + 1]
            decay = jnp.exp(jnp.where(incl[d], gc_col - gc_row, -1e30))
            a_dir.append(jnp.where(strict[d], kk * beta * decay, 0.0))
            intra = jnp.where(incl[d], qk * decay, 0.0)
            ends = (gc_col[c - 1:c], gc_col[2 * c - 1:2 * c]) if d == 0 else (gc_col[0:1], gc_col[c:c + 1])
            g_last = jnp.where(first_chunk, ends[0], ends[1])
            eg = jnp.exp(gc_col)
            kb = k * beta
            rhs_dir.append(jnp.concatenate([v * beta, kb * eg], axis=1))
            keep.append((q * eg, (k * jnp.exp(g_last - gc_col)).T, intra[:, :c] + intra[:, c:]))
            for ch in range(2):
                eg_ref[0, 2 * p + ch, lane:lane + 1, :] = jnp.broadcast_to(jnp.exp(ends[ch]), (1, LANES))
        big_a.append(jnp.concatenate([jnp.concatenate([a_dir[0], zeros], axis=1),
                                      jnp.concatenate([zeros, a_dir[1]], axis=1)], axis=0))
        rhs.append(jnp.concatenate(rhs_dir, axis=0))
    t_inv = _nilpotent_inverses(big_a, c)
    for u, (p, h) in enumerate(units):
        rows = slice(p * r, (p + 1) * r)
        uw = _dot(t_inv[u].astype(BF16), rhs[u].astype(BF16))
        (qd_f, kt_f, in_f), (qd_b, kt_b, in_b) = keep[2 * u], keep[2 * u + 1]
        uwqf_ref[0, h, rows, :] = jnp.concatenate([uw[:r], qd_f], axis=1).astype(BF16)
        uwqb_ref[0, h, rows, :] = jnp.concatenate([uw[r:], qd_b], axis=1).astype(BF16)
        ktf_ref[0, h, :, rows] = kt_f.astype(BF16)
        ktb_ref[0, h, :, rows] = kt_b.astype(BF16)
        in_ref[0, h, rows, :] = jnp.concatenate([in_f, in_b], axis=1).astype(BF16)


def gdn_local(qkv, gates_col, gates_row):
    nb, seq, _ = qkv.shape
    r = 2 * GDN_CHUNK * GDN_LOCAL_PAIRS
    hd, nh = GDN_HEAD_DIM, GDN_HEADS
    uwq_sd = jax.ShapeDtypeStruct((nb, nh, seq, 3 * hd), BF16)
    kt_sd = jax.ShapeDtypeStruct((nb, nh, hd, seq), BF16)
    uwq_spec = pl.BlockSpec((1, nh, r, 3 * hd), lambda b, i: (b, 0, i, 0))
    kt_spec = pl.BlockSpec((1, nh, hd, r), lambda b, i: (b, 0, 0, i))
    return pl.pallas_call(
        _gdn_local_kernel,
        grid=(nb, seq // r),
        in_specs=[pl.BlockSpec((1, r, GDN_WIDTH), lambda b, i: (b, i, 0)),
                  pl.BlockSpec((1, r, GDN_WIDTH), lambda b, i: (b, i, 1)),
                  pl.BlockSpec((1, r, GDN_WIDTH), lambda b, i: (b, i, 2)),
                  pl.BlockSpec((1, r, 4 * nh), lambda b, i: (b, i, 0)),
                  pl.BlockSpec((1, 4 * nh, r), lambda b, i: (b, 0, i))],
        out_specs=[uwq_spec, uwq_spec, kt_spec, kt_spec,
                   pl.BlockSpec((1, nh, r, 2 * GDN_CHUNK), lambda b, i: (b, 0, i, 0)),
                   pl.BlockSpec((1, r // GDN_CHUNK, SUBLANES, LANES), lambda b, i: (b, i, 0, 0))],
        out_shape=[uwq_sd, uwq_sd, kt_sd, kt_sd,
                   jax.ShapeDtypeStruct((nb, nh, seq, 2 * GDN_CHUNK), BF16),
                   jax.ShapeDtypeStruct((nb, seq // GDN_CHUNK, SUBLANES, LANES), F32)],
        compiler_params=_cparams("parallel", "parallel"),
        name="gdn_local",
    )(qkv, qkv, qkv, gates_col, gates_row)


def _gdn_state_kernel(uwqf, ktf, inf, egf, uwqb, ktb, inb, egb, of_ref, ob_ref, st_ref):
    @pl.when(pl.program_id(1) == 0)
    def _():
        st_ref[...] = jnp.zeros_like(st_ref)

    c = GDN_CHUNK
    hd = GDN_HEAD_DIM
    nc = uwqf.shape[2] // c
    chains = [(bb, d, h) for bb in range(uwqf.shape[0]) for d in range(2) for h in range(GDN_HEADS)]
    refs = ((uwqf, ktf, inf, egf, of_ref), (uwqb, ktb, inb, egb, ob_ref))
    states = [st_ref[j] for j in range(len(chains))]
    for step in range(nc):
        sws = []
        for j, (bb, d, h) in enumerate(chains):
            ci = step if d == 0 else nc - 1 - step
            rows = slice(ci * c, (ci + 1) * c)
            uwq = refs[d][0]
            wq = jnp.concatenate([uwq[bb, h, rows, hd:2 * hd], uwq[bb, h, rows, 2 * hd:3 * hd]], axis=0)
            sws.append(_dot(wq, states[j].astype(BF16)))
        for j, (bb, d, h) in enumerate(chains):
            ci = step if d == 0 else nc - 1 - step
            rows = slice(ci * c, (ci + 1) * c)
            uwq, kt, intra, eg, o_ref = refs[d]
            sw = sws[j]
            v16 = (uwq[bb, h, rows, 0:hd].astype(F32) - sw[:c]).astype(BF16)
            out = sw[c:] + _dot(intra[bb, h, rows, d * c:(d + 1) * c], v16)
            o_ref[bb, rows, h * hd:(h + 1) * hd] = out.astype(o_ref.dtype)
            lane = d * GDN_HEADS + h
            states[j] = states[j] * eg[bb, ci, lane:lane + 1, :] + _dot(kt[bb, h, :, rows], v16)
    for j in range(len(chains)):
        st_ref[j] = states[j]


def gdn_state_scan(uwq_f, uwq_b, kt_f, kt_b, intra, eg):
    nb, nh, seq, _ = uwq_f.shape
    hd = GDN_HEAD_DIM
    nc = GDN_STATE_CHUNKS
    bb = GDN_STATE_BATCH
    r = nc * GDN_CHUNK
    n = seq // r

    def specs(pos):
        return [pl.BlockSpec((bb, nh, r, 3 * hd), lambda b, i: (b, 0, pos(i), 0)),
                pl.BlockSpec((bb, nh, hd, r), lambda b, i: (b, 0, 0, pos(i))),
                pl.BlockSpec((bb, nh, r, 2 * GDN_CHUNK), lambda b, i: (b, 0, pos(i), 0)),
                pl.BlockSpec((bb, nc, SUBLANES, LANES), lambda b, i: (b, pos(i), 0, 0))]

    fwd = lambda i: i
    bwd = lambda i: n - 1 - i
    out_sd = jax.ShapeDtypeStruct((nb, seq, GDN_WIDTH), BF16)
    return pl.pallas_call(
        _gdn_state_kernel,
        grid=(nb // bb, n),
        in_specs=specs(fwd) + specs(bwd),
        out_specs=[pl.BlockSpec((bb, r, GDN_WIDTH), lambda b, i: (b, i, 0)),
                   pl.BlockSpec((bb, r, GDN_WIDTH), lambda b, i: (b, n - 1 - i, 0))],
        out_shape=[out_sd, out_sd],
        scratch_shapes=[pltpu.VMEM((bb * 2 * nh, hd, hd), F32)],
        compiler_params=_cparams("parallel", "arbitrary"),
        name="gdn_state_scan",
    )(uwq_f, kt_f, intra, eg, uwq_b, kt_b, intra, eg)


def _out_proj_kernel(x_ref, mod_ref, ys5_ref, cx_ref, cb_ref, cc_ref, cxp_ref, cxn_ref, ccp_ref, ccn_ref,
                     cw_ref, of_ref, ob_ref, z_ref, gnw_ref, w_ref, o_ref, *, seq):
    tm = x_ref.shape[0]
    t0 = (pl.program_id(0) * tm) % seq
    f32 = lambda ref, rows=slice(None): ref[rows, :].astype(F32)
    last = slice(HALO_ROWS - 1, HALO_ROWS)
    prev_row = jnp.where(t0 == 0, 0.0, f32(ccp_ref, last) * f32(cxp_ref, last))
    next_row = jnp.where(t0 + tm == seq, 0.0, f32(ccn_ref, slice(0, 1)) * f32(cxn_ref, slice(0, 1)))
    y_conv = f32(cb_ref) * _conv3_rows(f32(cc_ref) * f32(cx_ref), prev_row, next_row, cw_ref)
    o = f32(of_ref) + f32(ob_ref)
    z = f32(z_ref)
    heads = []
    for h in range(GDN_HEADS):
        cols = slice(h * GDN_HEAD_DIM, (h + 1) * GDN_HEAD_DIM)
        heads.append(_rms(o[:, cols]) * gnw_ref[...] * _silu(z[:, cols]))
    y_gdn = jnp.concatenate(heads, axis=1)
    c0, c1 = S5_WIDTH, S5_WIDTH + CONV_WIDTH
    mix = (_dot(ys5_ref[...].astype(BF16), w_ref[0:c0, :])
           + _dot(y_conv.astype(BF16), w_ref[c0:c1, :])
           + _dot(y_gdn.astype(BF16), w_ref[c1:, :]))
    o_ref[...] = x_ref[...] + mod_ref[0, 2:3, :] * mix


def out_projection(x2d, mod, y_s5, proj, conv_w, o_f, o_b, gdn_norm_w, w_out, seq):
    m, d = x2d.shape
    tm = ROW_TILE
    per_seq = seq // tm
    cw = CONV_WIDTH
    row = lambda c: pl.BlockSpec((tm, cw), lambda i: (i, c))
    cxp, cxn = _halo_specs(tm, cw, lambda: 1, m)
    ccp, ccn = _halo_specs(tm, cw, lambda: 3, m)
    gw = GDN_WIDTH
    return pl.pallas_call(
        functools.partial(_out_proj_kernel, seq=seq),
        grid=(m // tm,),
        in_specs=[pl.BlockSpec((tm, d), lambda i: (i, 0)),
                  pl.BlockSpec((1, 6, d), lambda i: (i // per_seq, 0, 0)),
                  pl.BlockSpec((tm, S5_WIDTH), lambda i: (i % per_seq, i // per_seq)),
                  row(1), row(2), row(3), cxp, cxn, ccp, ccn,
                  pl.BlockSpec((3, cw), lambda i: (0, 0)),
                  pl.BlockSpec((tm, gw), lambda i: (i, 0)),
                  pl.BlockSpec((tm, gw), lambda i: (i, 0)),
                  pl.BlockSpec((tm, gw), lambda i: (i, (PROJ_MAIN - gw) // gw)),
                  pl.BlockSpec((1, GDN_HEAD_DIM), lambda i: (0, 0)),
                  pl.BlockSpec((d, d), lambda i: (0, 0))],
        out_specs=pl.BlockSpec((tm, d), lambda i: (i, 0)),
        out_shape=jax.ShapeDtypeStruct((m, d), F32),
        compiler_params=_cparams("parallel"),
        name="out_projection",
    )(x2d, mod, y_s5, proj, proj, proj, proj, proj, proj, proj, conv_w, o_f, o_b, proj, gdn_norm_w, w_out)


def _first_argmax(vals, rows, n):
    m = jnp.max(vals, axis=0, keepdims=True)
    return m, jnp.min(jnp.where(vals == m, rows, n), axis=0, keepdims=True)


def _route_kernel(x_ref, mod_ref, nw_ref, wr_ref, br_ref, h_ref, idx_ref, wgt_ref, cnt_ref, carry_ref, before_ref):
    step = pl.program_id(0)
    tm = x_ref.shape[0]

    @pl.when(step == 0)
    def _():
        carry_ref[...] = jnp.zeros_like(carry_ref)
        earlier = lax.broadcasted_iota(jnp.int32, (tm, tm), 0) < lax.broadcasted_iota(jnp.int32, (tm, tm), 1)
        before_ref[...] = jnp.where(earlier, 1.0, 0.0).astype(BF16)

    h = _rms(x_ref[...]) * nw_ref[...] * (1.0 + mod_ref[0, 4:5, :]) + mod_ref[0, 3:4, :]
    h_ref[...] = h.reshape(h_ref.shape)
    nr = br_ref.shape[0]
    h_hi = h.astype(BF16)
    h_lo = (h - h_hi.astype(F32)).astype(BF16)
    head = _dot_nt(wr_ref[...], h_hi)
    logits = head[:nr] + head[nr:] + _dot_nt(wr_ref[0:nr, :], h_lo) + br_ref[...]
    lg = logits[N_EXPERTS:N_EXPERTS + MOE_GROUPS, :]
    rows_g = lax.broadcasted_iota(jnp.int32, lg.shape, 0)
    g_max, g_idx = _first_argmax(lg, rows_g, MOE_GROUPS)
    g_w = 1.0 / jnp.sum(jnp.exp(lg - g_max), axis=0, keepdims=True)
    le = jnp.zeros((EXPERTS_PER_GROUP, tm), F32)
    for g in range(MOE_GROUPS):
        le = jnp.where(g_idx == g, logits[g * EXPERTS_PER_GROUP:(g + 1) * EXPERTS_PER_GROUP, :], le)
    rows_e = lax.broadcasted_iota(jnp.int32, le.shape, 0)
    m1, i1 = _first_argmax(le, rows_e, EXPERTS_PER_GROUP)
    m2, i2 = _first_argmax(jnp.where(rows_e == i1, -jnp.inf, le), rows_e, EXPERTS_PER_GROUP)
    r = jnp.exp(m2 - m1)
    w1 = g_w / (1.0 + r)
    e1 = g_idx * EXPERTS_PER_GROUP + i1
    e2 = g_idx * EXPERTS_PER_GROUP + i2
    rows_x = lax.broadcasted_iota(jnp.int32, (N_EXPERTS, tm), 0)
    hit1 = rows_x == e1
    hit2 = rows_x == e2
    onehot = jnp.where(hit1 | hit2, 1.0, 0.0)
    ahead = _dot(onehot.astype(BF16), before_ref[...]) + carry_ref[:, 0:1]
    rank1 = jnp.sum(jnp.where(hit1, ahead, 0.0), axis=0, keepdims=True)
    rank2 = jnp.sum(jnp.where(hit2, ahead, 0.0), axis=0, keepdims=True)
    idx_ref[0, 0:1, :] = e1
    idx_ref[0, 1:2, :] = e2
    idx_ref[0, 2:3, :] = rank1.astype(jnp.int32)
    idx_ref[0, 3:4, :] = rank2.astype(jnp.int32)
    wgt_ref[0:1, :] = w1
    wgt_ref[1:2, :] = w1 * r
    carry_ref[...] = carry_ref[...] + jnp.sum(onehot, axis=1, keepdims=True)
    cnt_ref[...] = carry_ref[...]


def moe_route(x2d, mod, norm_w, w_router_t, b_router, seq):
    m, d = x2d.shape
    tm = ROUTE_TILE
    per_seq = seq // tm
    nr = w_router_t.shape[0]
    return pl.pallas_call(
        _route_kernel,
        grid=(m // tm,),
        in_specs=[pl.BlockSpec((tm, d), lambda i: (i, 0)),
                  pl.BlockSpec((1, 6, d), lambda i: (i // per_seq, 0, 0)),
                  pl.BlockSpec((1, d), lambda i: (0, 0)),
                  pl.BlockSpec((nr, d), lambda i: (0, 0)),
                  pl.BlockSpec((nr // 2, 1), lambda i: (0, 0))],
        out_specs=[pl.BlockSpec((tm,) + SLAB, lambda i: (i, 0, 0)),
                   pl.BlockSpec((1, 4, tm), lambda i: (i, 0, 0)),
                   pl.BlockSpec((2, tm), lambda i: (0, i)),
                   pl.BlockSpec((N_EXPERTS, LANES), lambda i: (0, 0))],
        out_shape=[jax.ShapeDtypeStruct((m,) + SLAB, F32),
                   jax.ShapeDtypeStruct((m // tm, 4, tm), jnp.int32),
                   jax.ShapeDtypeStruct((2, m), F32),
                   jax.ShapeDtypeStruct((N_EXPERTS, LANES), F32)],
        scratch_shapes=[pltpu.VMEM((N_EXPERTS, LANES), F32), pltpu.VMEM((tm, tm), BF16)],
        compiler_params=_cparams("arbitrary"),
        name="moe_route",
    )(x2d, mod, norm_w, w_router_t, b_router)


def _row_copy(src_ref, src_row, dst_ref, dst_row, sem):
    return pltpu.make_async_copy(src_ref.at[pl.ds(src_row, 1)], dst_ref.at[pl.ds(dst_row, 1)], sem)


def _scatter_kernel(last_ref, slot_ref, h_ref, xs_ref, zero_ref, sem):
    tm = h_ref.shape[0]

    @pl.when(pl.program_id(0) == 0)
    def _():
        zero_ref[...] = jnp.zeros_like(zero_ref)

        def fill(e, c):
            @pl.when(last_ref[e] >= 0)
            def _():
                pltpu.make_async_copy(zero_ref, xs_ref.at[pl.ds(last_ref[e], MOE_BLOCK)], sem).start()
            return c

        def drain(e, c):
            @pl.when(last_ref[e] >= 0)
            def _():
                pltpu.make_async_copy(zero_ref, xs_ref.at[pl.ds(0, MOE_BLOCK)], sem).wait()
            return c

        lax.fori_loop(0, N_EXPERTS, fill, 0)
        lax.fori_loop(0, N_EXPERTS, drain, 0)

        n_blocks = xs_ref.shape[0] // MOE_BLOCK

        def fill_unused(b, c):
            pltpu.make_async_copy(zero_ref, xs_ref.at[pl.ds(b * MOE_BLOCK, MOE_BLOCK)], sem).start()
            return c

        def drain_unused(b, c):
            pltpu.make_async_copy(zero_ref, xs_ref.at[pl.ds(0, MOE_BLOCK)], sem).wait()
            return c

        lax.fori_loop(last_ref[N_EXPERTS], n_blocks, fill_unused, 0)
        lax.fori_loop(last_ref[N_EXPERTS], n_blocks, drain_unused, 0)

    def start(r, c):
        _row_copy(h_ref, r, xs_ref, slot_ref[0, 0, r], sem).start(priority=0)
        _row_copy(h_ref, r, xs_ref, slot_ref[0, 1, r], sem).start(priority=1)
        return c

    def wait(r, c):
        _row_copy(h_ref, 0, xs_ref, 0, sem).wait()
        _row_copy(h_ref, 0, xs_ref, 0, sem).wait()
        return c

    lax.fori_loop(0, tm, start, 0, unroll=DMA_UNROLL)
    lax.fori_loop(0, tm, wait, 0, unroll=DMA_UNROLL)


def moe_scatter(h_slab, slots, last_block_start, n_slots):
    m = h_slab.shape[0]
    tm = ROUTE_TILE
    return pl.pallas_call(
        _scatter_kernel,
        grid_spec=pltpu.PrefetchScalarGridSpec(
            num_scalar_prefetch=1,
            grid=(m // tm,),
            in_specs=[pl.BlockSpec((1, 2, tm), lambda i, last: (i, 0, 0), memory_space=pltpu.SMEM),
                      pl.BlockSpec((tm,) + SLAB, lambda i, last: (i, 0, 0))],
            out_specs=pl.BlockSpec(memory_space=pl.ANY),
            scratch_shapes=[pltpu.VMEM((MOE_BLOCK,) + SLAB, F32), pltpu.SemaphoreType.DMA(())]),
        out_shape=jax.ShapeDtypeStruct((n_slots,) + SLAB, F32),
        compiler_params=_cparams("arbitrary"),
        name="moe_scatter",
    )(last_block_start, slots, h_slab)


def _expert_kernel(be_ref, nused_ref, xs_ref, wgu_ref, wd_ref, y_ref, wgu16_ref, wd16_ref):
    i = pl.program_id(0)
    changed = jnp.logical_or(i == 0, be_ref[i] != be_ref[jnp.maximum(i - 1, 0)])

    @pl.when(changed)
    def _():
        wgu16_ref[...] = wgu_ref[0, 0].astype(BF16)
        wd16_ref[...] = wd_ref[0, 0].astype(BF16)

    @pl.when(i < nused_ref[0])
    def _():
        x = xs_ref[...].reshape(MOE_BLOCK, D_MODEL).astype(BF16)
        gu = _dot(x, wgu16_ref[...])
        act = _silu(gu[:, :EXPERT_FF]) * gu[:, EXPERT_FF:]
        y = _dot(act.astype(BF16), wd16_ref[...])
        y_ref[...] = y.reshape(y_ref.shape)

    @pl.when(i >= nused_ref[0])
    def _():
        y_ref[...] = jnp.zeros_like(y_ref)


def moe_experts(xs, blk_expert, n_used, w_gate_up, w_down, layer):
    n_slots = xs.shape[0]
    n_blocks = n_slots // MOE_BLOCK
    d = D_MODEL
    return pl.pallas_call(
        _expert_kernel,
        grid_spec=pltpu.PrefetchScalarGridSpec(
            num_scalar_prefetch=2,
            grid=(n_blocks,),
            in_specs=[pl.BlockSpec((MOE_BLOCK,) + SLAB, lambda i, be, nu: (jnp.minimum(i, nu[0] - 1), 0, 0)),
                      pl.BlockSpec((1, 1, d, 2 * EXPERT_FF), lambda i, be, nu: (layer, be[i], 0, 0)),
                      pl.BlockSpec((1, 1, EXPERT_FF, d), lambda i, be, nu: (layer, be[i], 0, 0))],
            out_specs=pl.BlockSpec((MOE_BLOCK,) + SLAB, lambda i, be, nu: (i, 0, 0)),
            scratch_shapes=[pltpu.VMEM((d, 2 * EXPERT_FF), BF16), pltpu.VMEM((EXPERT_FF, d), BF16)]),
        out_shape=jax.ShapeDtypeStruct((n_slots,) + SLAB, F32),
        compiler_params=_cparams("arbitrary"),
        name="moe_experts",
    )(blk_expert, n_used, xs, w_gate_up, w_down)


def _combine_kernel(slot_ref, next_slot_ref, x_ref, mod_ref, wgt_ref, fnw_ref, yb_ref, o_ref, y1_ref, y2_ref, sem, *,
                    final):
    tm = x_ref.shape[0]
    i = pl.program_id(0)
    cur = i % 2

    def gather(slots, buf):
        def start(r, c):
            _row_copy(yb_ref, slots[0, 0, r], y1_ref.at[buf], r, sem.at[buf]).start(priority=0)
            _row_copy(yb_ref, slots[0, 1, r], y2_ref.at[buf], r, sem.at[buf]).start(priority=1)
            return c
        lax.fori_loop(0, tm, start, 0, unroll=DMA_UNROLL)

    @pl.when(i == 0)
    def _():
        gather(slot_ref, 0)

    @pl.when(i + 1 < pl.num_programs(0))
    def _():
        gather(next_slot_ref, 1 - cur)

    def wait(r, c):
        _row_copy(yb_ref, 0, y1_ref.at[cur], 0, sem.at[cur]).wait()
        _row_copy(yb_ref, 0, y2_ref.at[cur], 0, sem.at[cur]).wait()
        return c

    lax.fori_loop(0, tm, wait, 0, unroll=DMA_UNROLL)
    w1 = wgt_ref[:, 0:1]
    w2 = wgt_ref[:, 1:2]
    y = y1_ref[cur].reshape(x_ref.shape) * w1 + y2_ref[cur].reshape(x_ref.shape) * w2
    o_ref[...] = x_ref[...] + mod_ref[0, 5:6, :] * y
    if final:
        o_ref[...] = _rms(o_ref[...]) * fnw_ref[...]


def moe_combine(x2d, mod, yb, slots, weights, final_norm_w, seq, final):
    m, d = x2d.shape
    tm = ROUTE_TILE
    per_seq = seq // tm
    n = m // tm
    slot_spec = lambda pos: pl.BlockSpec((1, 2, tm), lambda i: (pos(i), 0, 0), memory_space=pltpu.SMEM)
    return pl.pallas_call(
        functools.partial(_combine_kernel, final=final),
        grid=(n,),
        in_specs=[slot_spec(lambda i: i), slot_spec(lambda i: jnp.minimum(i + 1, n - 1)),
                  pl.BlockSpec((tm, d), lambda i: (i, 0)),
                  pl.BlockSpec((1, 6, d), lambda i: (i // per_seq, 0, 0)),
                  pl.BlockSpec((tm, 2), lambda i: (i, 0)),
                  pl.BlockSpec((1, d), lambda i: (0, 0)),
                  pl.BlockSpec(memory_space=pl.ANY)],
        out_specs=pl.BlockSpec((tm, d), lambda i: (i, 0)),
        out_shape=jax.ShapeDtypeStruct((m, d), F32),
        scratch_shapes=[pltpu.VMEM((2, tm) + SLAB, F32), pltpu.VMEM((2, tm) + SLAB, F32),
                        pltpu.SemaphoreType.DMA((2,))],
        compiler_params=_cparams("arbitrary"),
        name="moe_combine",
    )(slots, slots, x2d, mod, weights, final_norm_w, yb)


def hier_moe_layer(x2d, mod, norm_w, w_router_t, b_router, w_gate_up, w_down, layer, final_norm_w, seq, final):
    m = x2d.shape[0]
    h_slab, idx, wgt, counts = moe_route(x2d, mod, norm_w, w_router_t, b_router, seq)
    counts = counts[:, 0].astype(jnp.int32)
    padded = (counts + MOE_BLOCK - 1) // MOE_BLOCK * MOE_BLOCK
    pad_end = jnp.cumsum(padded)
    pad_start = (pad_end - padded).astype(jnp.int32)
    n_blocks = 2 * m // MOE_BLOCK + N_EXPERTS
    blk_start = jnp.arange(n_blocks, dtype=jnp.int32) * MOE_BLOCK
    blk_expert = jnp.minimum(jnp.sum((pad_end[None, :] <= blk_start[:, None]).astype(jnp.int32), axis=1),
                             N_EXPERTS - 1)
    n_used = (pad_end[-1:] // MOE_BLOCK).astype(jnp.int32)
    fill_plan = jnp.concatenate([jnp.where(padded > 0, pad_end - MOE_BLOCK, -1), n_used]).astype(jnp.int32)
    is_expert = idx[:, 0:2, :, None] == jnp.arange(N_EXPERTS, dtype=jnp.int32)
    slots = jnp.sum(jnp.where(is_expert, pad_start, 0), axis=-1) + idx[:, 2:4, :]
    xs = moe_scatter(h_slab, slots, fill_plan, n_blocks * MOE_BLOCK)
    yb = moe_experts(xs, blk_expert, n_used, w_gate_up, w_down, layer)
    return moe_combine(x2d, mod, yb, slots, wgt.T, final_norm_w, seq, final)


def _layer_params(l, p):
    w_in = p['w_in'][l]
    w_gate = w_in[:, PROJ_MAIN:].T
    w_router_t = jnp.zeros((ROUTER_ROWS, D_MODEL), F32)
    w_router_t = w_router_t.at[:N_EXPERTS].set(p['moe_w_expert'][l].T)
    w_router_t = w_router_t.at[N_EXPERTS:N_EXPERTS + MOE_GROUPS].set(p['moe_w_group'][l].T)
    w_router_hi = w_router_t.astype(BF16)
    w_router_t = jnp.concatenate([w_router_hi, (w_router_t - w_router_hi.astype(F32)).astype(BF16)], axis=0)
    b_router = jnp.zeros((ROUTER_ROWS, 1), F32)
    b_router = b_router.at[:N_EXPERTS, 0].set(p['moe_b_expert'][l])
    b_router = b_router.at[N_EXPERTS:N_EXPERTS + MOE_GROUPS, 0].set(p['moe_b_group'][l])
    s5 = [s5_discretise(p['s5_lam_re'][l, d], p['s5_lam_im'][l, d], p['s5_log_step'][l, d],
                        p['s5_b_re'][l, d], p['s5_b_im'][l, d], p['s5_c_re'][l, d], p['s5_c_im'][l, d])
          for d in range(2)]
    return dict(
        norm_mix=p['norm_mix'][l][None, :], norm_ffn=p['norm_ffn'][l][None, :],
        w_main=w_in[:, :PROJ_MAIN].astype(BF16), w_gate=w_gate.astype(BF16),
        w_out=p['w_out'][l].astype(BF16), s5=s5,
        s5_d=p['s5_d'][l][None, :], s5_w_glu=p['s5_w_glu'][l].astype(BF16), s5_b_glu=p['s5_b_glu'][l][None, :],
        conv_w=p['conv_w'][l], gdn_conv_w=p['gdn_conv_w'][l], gdn_a_log=p['gdn_a_log'][l],
        gdn_dt_bias=p['gdn_dt_bias'][l], gdn_norm_w=p['gdn_norm_w'][l][None, :],
        w_router_t=w_router_t, b_router=b_router,
        w_gate_up=p['moe_w_gate_up'], w_down=p['moe_w_down'])


def _trunk(x, mods, layers, final_norm_w):
    nb, seq, d = x.shape
    m = nb * seq
    x2d = x.reshape(m, d)
    for l, lp in enumerate(layers):
        mod = mods[l].reshape(nb, 6, d)
        proj, u_t, pre_t = in_projection(x2d, mod, lp['norm_mix'], lp['w_main'], lp['w_gate'], seq)
        (a_f, bm_f, cm_f), (a_b, bm_b, cm_b) = lp['s5']
        y_s5 = s5_mixer(u_t, a_f, bm_f, cm_f, a_b, bm_b, cm_b, lp['s5_d'], lp['s5_w_glu'], lp['s5_b_glu'])
        qkv = gdn_prepare(proj, lp['gdn_conv_w'], seq).reshape(nb, seq, 3 * GDN_WIDTH)
        gates = gdn_gates(pre_t, lp['gdn_a_log'], lp['gdn_dt_bias'])
        uwq_f, uwq_b, kt_f, kt_b, intra, eg = gdn_local(qkv, gates.transpose(0, 2, 1), gates)
        o_f, o_b = gdn_state_scan(uwq_f, uwq_b, kt_f, kt_b, intra, eg)
        x2d = out_projection(x2d, mod, y_s5, proj, lp['conv_w'], o_f.reshape(m, GDN_WIDTH),
                             o_b.reshape(m, GDN_WIDTH), lp['gdn_norm_w'], lp['w_out'], seq)
        x2d = hier_moe_layer(x2d, mod, lp['norm_ffn'], lp['w_router_t'], lp['b_router'],
                             lp['w_gate_up'], lp['w_down'], l, final_norm_w, seq, final=(l == len(layers) - 1))
    return x2d.reshape(nb, seq, d)


def kernel(x_prompt, x_sample, c_prompt, c_sample, w_ada, b_ada, norm_mix, norm_ffn, w_in, w_out, s5_lam_re, s5_lam_im, s5_log_step, s5_b_re, s5_b_im, s5_c_re, s5_c_im, s5_d, s5_w_glu, s5_b_glu, conv_w, gdn_conv_w, gdn_a_log, gdn_dt_bias, gdn_norm_w, moe_w_group, moe_b_group, moe_w_expert, moe_b_expert, moe_w_gate_up, moe_w_down, final_norm):
    p = dict(norm_mix=norm_mix, norm_ffn=norm_ffn, w_in=w_in, w_out=w_out, s5_lam_re=s5_lam_re,
             s5_lam_im=s5_lam_im, s5_log_step=s5_log_step, s5_b_re=s5_b_re, s5_b_im=s5_b_im, s5_c_re=s5_c_re,
             s5_c_im=s5_c_im, s5_d=s5_d, s5_w_glu=s5_w_glu, s5_b_glu=s5_b_glu, conv_w=conv_w,
             gdn_conv_w=gdn_conv_w, gdn_a_log=gdn_a_log, gdn_dt_bias=gdn_dt_bias, gdn_norm_w=gdn_norm_w,
             moe_w_group=moe_w_group, moe_b_group=moe_b_group, moe_w_expert=moe_w_expert,
             moe_b_expert=moe_b_expert, moe_w_gate_up=moe_w_gate_up, moe_w_down=moe_w_down)
    depth = w_ada.shape[0]
    layers = [_layer_params(l, p) for l in range(depth)]
    nbp = c_prompt.shape[0]
    mods = ada_modulation(jnp.concatenate([c_prompt, c_sample], axis=0), w_ada, b_ada)
    fnw = final_norm[None, :]
    y_prompt = _trunk(x_prompt, mods[:, :nbp], layers, fnw)
    y_sample = _trunk(x_sample, mods[:, nbp:], layers, fnw)
    return (y_prompt, y_sample)
```
